```python
import math
import jax
import jax.numpy as jnp
from jax import lax
import numpy as np

D_MODEL = 2048
BATCH = 8
SEQ = 4096
DEPTH = 2

GRID_W = 64
HEAD_DIM = 128
Q_BLOCK = 128
ROPE_THETA = 10000.0
RMS_EPS = 1e-6
LN_EPS = 1e-5

A_HEADS = 8
A_KV_HEADS = 2
A_Q_DIM = A_HEADS * HEAD_DIM
A_KV_DIM = A_KV_HEADS * HEAD_DIM

B_HEADS = 8
B_DIM = B_HEADS * HEAD_DIM
B_BRANCHES = ((128, 1), (512, 4), (2048, 16))

C_HEADS = 8
C_DIM = C_HEADS * HEAD_DIM
NA_ROWS = 8
NA_COLS = 16

D_HEADS = 8
D_Q_RANK = 512
D_KV_RANK = 256
D_NOPE = 128
D_ROPE = 64
D_V = 128

IN0_DIM = A_Q_DIM + 2 * A_KV_DIM + 3 * B_DIM
IN1_DIM = 3 * C_DIM + D_Q_RANK + D_KV_RANK + D_ROPE
MIX_DIM = A_Q_DIM + B_DIM

N_EXPERTS = 32
N_GROUPS = 8
EXPERTS_PER_GROUP = N_EXPERTS // N_GROUPS
TOP_K = 2
D_EXPERT = 512
MOE_BLOCK = 256

ALPHA = (2.0 * DEPTH) ** 0.25
BETA = (8.0 * DEPTH) ** -0.25

kernel_name = 'hybrid_axial_dilated_na_mla_moe_encoder'


def rms_norm(x, g):
    xf = x.astype(jnp.float32)
    y = xf * lax.rsqrt(jnp.mean(xf * xf, axis=-1, keepdims=True) + RMS_EPS)
    return (y * g.astype(jnp.float32)).astype(x.dtype)


def layer_norm(x, g, b):
    xf = x.astype(jnp.float32)
    mu = jnp.mean(xf, axis=-1, keepdims=True)
    xc = xf - mu
    var = jnp.mean(xc * xc, axis=-1, keepdims=True)
    return (xc * lax.rsqrt(var + LN_EPS) * g.astype(jnp.float32) + b.astype(jnp.float32)).astype(x.dtype)


def rope_angles(pos, dim):
    inv_freq = ROPE_THETA ** (-jnp.arange(0, dim, 2, dtype=jnp.float32) / dim)
    return pos.astype(jnp.float32)[:, None] * inv_freq[None, :]


def apply_rope(x, ang):
    xf = x.astype(jnp.float32)
    x1, x2 = jnp.split(xf, 2, axis=-1)
    cos = jnp.cos(ang)[:, None, :]
    sin = jnp.sin(ang)[:, None, :]
    return jnp.concatenate([x1 * cos - x2 * sin, x2 * cos + x1 * sin], axis=-1).astype(x.dtype)


def alibi_slopes(n):
    return jnp.asarray(2.0 ** (-8.0 * np.arange(1, n + 1) / n), dtype=jnp.float32)


def dense_blocked_attention(q, k, v, scale):
    Bsz, S, Hkv, G, dq = q.shape
    dv = v.shape[-1]
    nb = S // Q_BLOCK
    qb = q.reshape(Bsz, nb, Q_BLOCK, Hkv, G, dq).transpose(1, 0, 2, 3, 4, 5)

    def block(qblk):
        s = jnp.einsum('bqhgd,bkhd->bhgqk', qblk, k).astype(jnp.float32) * scale
        p = jax.nn.softmax(s, axis=-1).astype(v.dtype)
        return jnp.einsum('bhgqk,bkhd->bqhgd', p, v)

    o = lax.map(block, qb)
    return o.transpose(1, 0, 2, 3, 4, 5).reshape(Bsz, S, Hkv, G, dv)


def dilated_branch(q, k, v, window, dil, slopes):
    Bsz, S, H, hd = q.shape
    half = window // (2 * dil)
    L = S // dil
    N = Bsz * dil
    nb = -(-L // half)
    Lp = nb * half

    def strided(t):
        return t.reshape(Bsz, L, dil, H, hd).transpose(0, 2, 1, 3, 4).reshape(N, L, H, hd)

    qs = jnp.pad(strided(q), ((0, 0), (0, Lp - L), (0, 0), (0, 0))).reshape(N, nb, half, H, hd)
    kpad = ((0, 0), (half, Lp - L + half), (0, 0), (0, 0))
    ks = jnp.pad(strided(k), kpad).reshape(N, nb + 2, half, H, hd)
    vs = jnp.pad(strided(v), kpad).reshape(N, nb + 2, half, H, hd)
    kw = jnp.concatenate([ks[:, :-2], ks[:, 1:-1], ks[:, 2:]], axis=2)
    vw = jnp.concatenate([vs[:, :-2], vs[:, 1:-1], vs[:, 2:]], axis=2)

    qpos = jnp.arange(Lp).reshape(nb, half)
    kpos = jnp.arange(nb)[:, None] * half - half + jnp.arange(3 * half)[None, :]
    rel = kpos[:, None, :] - qpos[:, :, None]
    valid = (jnp.abs(rel) <= half) & (kpos[:, None, :] >= 0) & (kpos[:, None, :] < L)
    dist = (jnp.abs(rel) * dil).astype(jnp.float32)
    bias = -slopes[:, None, None, None] * dist[None]

    s = jnp.einsum('nbqhd,nbkhd->nhbqk', qs, kw).astype(jnp.float32) * (hd ** -0.5) + bias[None]
    s = jnp.where(valid[None, None], s, -jnp.inf)
    m = jnp.max(s, axis=-1, keepdims=True)
    p = jnp.exp(s - m)
    den = jnp.sum(p, axis=-1, keepdims=True)
    o = jnp.einsum('nhbqk,nbkhd->nbqhd', (p / den).astype(v.dtype), vw)
    lse = (m + jnp.log(den))[..., 0]

    o = o.reshape(Bsz, dil, Lp, H, hd)[:, :, :L].transpose(0, 2, 1, 3, 4).reshape(Bsz, S, H, hd)
    lse = lse.reshape(Bsz, dil, H, Lp)[..., :L].transpose(0, 3, 1, 2).reshape(Bsz, S, H)
    return o, lse


def dilated_attention(q, k, v):
    slopes = alibi_slopes(q.shape[2])
    outs = []
    lses = []
    for window, dil in B_BRANCHES:
        o, l = dilated_branch(q, k, v, window, dil, slopes)
        outs.append(o)
        lses.append(l)
    w = jax.nn.softmax(jnp.stack(lses, axis=0), axis=0)
    o = jnp.stack(outs, axis=0).astype(jnp.float32)
    return jnp.sum(w[..., None] * o, axis=0).astype(q.dtype)


def neighbourhood_attention(q, k, v, rpb):
    Bsz, S, H, hd = q.shape
    rows = S // GRID_W
    kr = min(NA_ROWS, rows)
    nk = kr * GRID_W
    col = jnp.arange(GRID_W)
    c0 = jnp.clip(col - NA_COLS // 2, 0, GRID_W - NA_COLS)
    kcol = jnp.tile(col, kr)
    krow = jnp.repeat(jnp.arange(kr), GRID_W)
    col_ok = (kcol[None, :] >= c0[:, None]) & (kcol[None, :] < c0[:, None] + NA_COLS)
    dcol = jnp.clip(kcol[None, :] - col[:, None] + NA_COLS - 1, 0, 2 * NA_COLS - 2)
    qr = q.reshape(Bsz, rows, GRID_W, H, hd).transpose(1, 0, 2, 3, 4)

    def row_block(args):
        r, qrow = args
        r0 = jnp.clip(r - NA_ROWS // 2, 0, rows - kr)
        kb = lax.dynamic_slice_in_dim(k, r0 * GRID_W, nk, axis=1)
        vb = lax.dynamic_slice_in_dim(v, r0 * GRID_W, nk, axis=1)
        drow = r0 + krow - r + NA_ROWS - 1
        bias = rpb[:, drow[None, :], dcol].astype(jnp.float32)
        s = jnp.einsum('bqhd,bkhd->bhqk', qrow, kb).astype(jnp.float32) * (hd ** -0.5) + bias[None]
        s = jnp.where(col_ok[None, None], s, -jnp.inf)
        p = jax.nn.softmax(s, axis=-1).astype(v.dtype)
        return jnp.einsum('bhqk,bkhd->bqhd', p, vb)

    o = lax.map(row_block, (jnp.arange(rows), qr))
    return o.transpose(1, 0, 2, 3, 4).reshape(Bsz, S, H, hd)


def mixer_ab(x, w_in, a_q_norm, a_k_norm, w_out):
    Bsz, S, _ = x.shape
    h = x @ w_in
    cuts = [A_Q_DIM, A_Q_DIM + A_KV_DIM, A_Q_DIM + 2 * A_KV_DIM,
            A_Q_DIM + 2 * A_KV_DIM + B_DIM, A_Q_DIM + 2 * A_KV_DIM + 2 * B_DIM]
    qa, ka, va, qb, kb, vb = jnp.split(h, cuts, axis=-1)
    pos = jnp.arange(S)
    half = HEAD_DIM // 2
    ang_row = rope_angles(pos // GRID_W, half)
    ang_col = rope_angles(pos % GRID_W, half)

    def axial(t):
        return jnp.concatenate([apply_rope(t[..., :half], ang_row), apply_rope(t[..., half:], ang_col)], axis=-1)

    qa = axial(rms_norm(qa.reshape(Bsz, S, A_HEADS, HEAD_DIM), a_q_norm))
    ka = axial(rms_norm(ka.reshape(Bsz, S, A_KV_HEADS, HEAD_DIM), a_k_norm))
    va = va.reshape(Bsz, S, A_KV_HEADS, HEAD_DIM)
    groups = A_HEADS // A_KV_HEADS
    oa = dense_blocked_attention(qa.reshape(Bsz, S, A_KV_HEADS, groups, HEAD_DIM), ka, va,
                                 HEAD_DIM ** -0.5).reshape(Bsz, S, A_Q_DIM)
    shp = (Bsz, S, B_HEADS, HEAD_DIM)
    ob = dilated_attention(qb.reshape(shp), kb.reshape(shp), vb.reshape(shp)).reshape(Bsz, S, B_DIM)
    return jnp.concatenate([oa, ob], axis=-1) @ w_out


def mixer_cd(x, w_in, c_rpb, d_q_norm, d_w_q_up, d_kv_norm, d_w_kv_up, w_out):
    Bsz, S, _ = x.shape
    h = x @ w_in
    cuts = [C_DIM, 2 * C_DIM, 3 * C_DIM, 3 * C_DIM + D_Q_RANK, 3 * C_DIM + D_Q_RANK + D_KV_RANK]
    qc, kc, vc, cq, ckv, kr = jnp.split(h, cuts, axis=-1)
    shp = (Bsz, S, C_HEADS, HEAD_DIM)
    oc = neighbourhood_attention(qc.reshape(shp), kc.reshape(shp), vc.reshape(shp), c_rpb).reshape(Bsz, S, C_DIM)
    ang = rope_angles(jnp.arange(S), D_ROPE)
    q = (rms_norm(cq, d_q_norm) @ d_w_q_up).reshape(Bsz, S, D_HEADS, D_NOPE + D_ROPE)
    kv = (rms_norm(ckv, d_kv_norm) @ d_w_kv_up).reshape(Bsz, S, D_HEADS, D_NOPE + D_V)
    q = jnp.concatenate([q[..., :D_NOPE], apply_rope(q[..., D_NOPE:], ang)], axis=-1)
    k_rope = jnp.broadcast_to(apply_rope(kr[:, :, None, :], ang), (Bsz, S, D_HEADS, D_ROPE))
    k = jnp.concatenate([kv[..., :D_NOPE], k_rope], axis=-1)
    v = kv[..., D_NOPE:]
    od = dense_blocked_attention(q[:, :, :, None, :], k, v,
                                 (D_NOPE + D_ROPE) ** -0.5).reshape(Bsz, S, D_HEADS * D_V)
    return jnp.concatenate([oc, od], axis=-1) @ w_out


def moe(x, router_w, router_b, w_gate, w_up, w_down):
    Bsz, S, D = x.shape
    T = Bsz * S
    xf = x.reshape(T, D)
    scores = jax.nn.sigmoid(xf.astype(jnp.float32) @ router_w.astype(jnp.float32))
    sel = scores + router_b.astype(jnp.float32)
    grp_score = jnp.sum(lax.top_k(sel.reshape(T, N_GROUPS, EXPERTS_PER_GROUP), TOP_K)[0], axis=-1)
    g = jnp.argmax(grp_score, axis=-1)
    in_grp = (jnp.arange(N_EXPERTS) // EXPERTS_PER_GROUP)[None, :] == g[:, None]
    _, eid = lax.top_k(jnp.where(in_grp, sel, -jnp.inf), TOP_K)
    gate = jnp.take_along_axis(scores, eid, axis=-1)
    gate = gate / jnp.sum(gate, axis=-1, keepdims=True)

    n_assign = T * TOP_K
    flat_e = eid.reshape(n_assign)
    flat_tok = jnp.repeat(jnp.arange(T, dtype=jnp.int32), TOP_K)
    flat_w = gate.reshape(n_assign)
    order = jnp.argsort(flat_e)
    se = flat_e[order]
    stok = flat_tok[order]
    sw = flat_w[order]
    counts = jnp.zeros((N_EXPERTS,), jnp.int32).at[flat_e].add(1)
    starts = jnp.cumsum(counts) - counts
    pcounts = (counts + MOE_BLOCK - 1) // MOE_BLOCK * MOE_BLOCK
    pends = jnp.cumsum(pcounts)
    pstarts = pends - pcounts
    dest = pstarts[se] + (jnp.arange(n_assign, dtype=jnp.int32) - starts[se])
    n_blocks = (n_assign + N_EXPERTS * (MOE_BLOCK - 1) + MOE_BLOCK - 1) // MOE_BLOCK
    n_rows = n_blocks * MOE_BLOCK
    row_tok = jnp.zeros((n_rows,), jnp.int32).at[dest].set(stok)
    blk_e = jnp.minimum(jnp.searchsorted(pends, jnp.arange(n_blocks) * MOE_BLOCK, side='right'), N_EXPERTS - 1)

    def expert_block(args):
        e, toks = args
        xb = xf[toks]
        hb = jax.nn.silu(xb @ w_gate[e]) * (xb @ w_up[e])
        return hb @ w_down[e]

    y_rows = lax.map(expert_block, (blk_e, row_tok.reshape(n_blocks, MOE_BLOCK))).reshape(n_rows, D)
    y = y_rows[dest] * sw[:, None].astype(y_rows.dtype)
    out = jax.ops.segment_sum(y, stok, num_segments=T)
    return out.reshape(Bsz, S, D)


def setup_inputs(seed: int = 0) -> dict:
    key = jax.random.key(seed)
    ks = iter(jax.random.split(key, 32))
    d = D_MODEL

    def nrm(shape, scale):
        return jax.random.normal(next(ks), shape, jnp.float32) * scale

    def gain(n):
        return 1.0 + nrm((n,), 0.02)

    inp = {}
    inp['x'] = nrm((BATCH, SEQ, d), 1.0)
    inp['router_w'] = nrm((d, N_EXPERTS), d ** -0.5)
    inp['router_b'] = nrm((N_EXPERTS,), 0.01)
    inp['l0_w_in'] = nrm((d, IN0_DIM), d ** -0.5)
    inp['l0_a_q_norm'] = gain(HEAD_DIM)
    inp['l0_a_k_norm'] = gain(HEAD_DIM)
    inp['l0_w_out'] = nrm((MIX_DIM, d), MIX_DIM ** -0.5 * BETA)
    inp['l0_ln1_g'] = gain(d)
    inp['l0_ln1_b'] = nrm((d,), 0.02)
    inp['l0_w_gate'] = nrm((N_EXPERTS, d, D_EXPERT), d ** -0.5)
    inp['l0_w_up'] = nrm((N_EXPERTS, d, D_EXPERT), d ** -0.5)
    inp['l0_w_down'] = nrm((N_EXPERTS, D_EXPERT, d), D_EXPERT ** -0.5 * BETA)
    inp['l0_ln2_g'] = gain(d)
    inp['l0_ln2_b'] = nrm((d,), 0.02)
    inp['l1_w_in'] = nrm((d, IN1_DIM), d ** -0.5)
    inp['l1_c_rpb'] = nrm((C_HEADS, 2 * NA_ROWS - 1, 2 * NA_COLS - 1), 0.1)
    inp['l1_d_q_norm'] = gain(D_Q_RANK)
    inp['l1_d_w_q_up'] = nrm((D_Q_RANK, D_HEADS * (D_NOPE + D_ROPE)), D_Q_RANK ** -0.5)
    inp['l1_d_kv_norm'] = gain(D_KV_RANK)
    inp['l1_d_w_kv_up'] = nrm((D_KV_RANK, D_HEADS * (D_NOPE + D_V)), D_KV_RANK ** -0.5)
    inp['l1_w_out'] = nrm((MIX_DIM, d), MIX_DIM ** -0.5 * BETA)
    inp['l1_ln1_g'] = gain(d)
    inp['l1_ln1_b'] = nrm((d,), 0.02)
    inp['l1_w_gate'] = nrm((N_EXPERTS, d, D_EXPERT), d ** -0.5)
    inp['l1_w_up'] = nrm((N_EXPERTS, d, D_EXPERT), d ** -0.5)
    inp['l1_w_down'] = nrm((N_EXPERTS, D_EXPERT, d), D_EXPERT ** -0.5 * BETA)
    inp['l1_ln2_g'] = gain(d)
    inp['l1_ln2_b'] = nrm((d,), 0.02)
    return inp


def reference(x, router_w, router_b,
              l0_w_in, l0_a_q_norm, l0_a_k_norm, l0_w_out, l0_ln1_g, l0_ln1_b,
              l0_w_gate, l0_w_up, l0_w_down, l0_ln2_g, l0_ln2_b,
              l1_w_in, l1_c_rpb, l1_d_q_norm, l1_d_w_q_up, l1_d_kv_norm, l1_d_w_kv_up, l1_w_out,
              l1_ln1_g, l1_ln1_b, l1_w_gate, l1_w_up, l1_w_down, l1_ln2_g, l1_ln2_b):
    mixers = (
        lambda h: mixer_ab(h, l0_w_in, l0_a_q_norm, l0_a_k_norm, l0_w_out),
        lambda h: mixer_cd(h, l1_w_in, l1_c_rpb, l1_d_q_norm, l1_d_w_q_up, l1_d_kv_norm, l1_d_w_kv_up, l1_w_out),
    )
    post = (
        (l0_ln1_g, l0_ln1_b, l0_w_gate, l0_w_up, l0_w_down, l0_ln2_g, l0_ln2_b),
        (l1_ln1_g, l1_ln1_b, l1_w_gate, l1_w_up, l1_w_down, l1_ln2_g, l1_ln2_b),
    )
    for layer in range(DEPTH):
        ln1_g, ln1_b, w_gate, w_up, w_down, ln2_g, ln2_b = post[layer]
        x = layer_norm(ALPHA * x + mixers[layer % 2](x), ln1_g, ln1_b)
        x = layer_norm(ALPHA * x + moe(x, router_w, router_b, w_gate, w_up, w_down), ln2_g, ln2_b)
    return x
```

```python
import functools
import math

import jax
import jax.numpy as jnp
import numpy as np
from jax import lax
from jax.experimental import pallas as pl
from jax.experimental.pallas import tpu as pltpu

D_MODEL = 2048
DEPTH = 2
GRID_W = 64
HEAD_DIM = 128
ROPE_THETA = 10000.0
RMS_EPS = 1e-6
LN_EPS = 1e-5

A_HEADS = 8
A_KV_HEADS = 2
B_HEADS = 8
B_BRANCHES = ((128, 1), (512, 4), (2048, 16))
C_HEADS = 8
NA_ROWS = 8
NA_COLS = 16
D_HEADS = 8
D_Q_RANK = 512
D_KV_RANK = 256
D_NOPE = 128
D_ROPE = 64
D_V = 128

N_EXPERTS = 32
N_GROUPS = 8
EXPERTS_PER_GROUP = N_EXPERTS // N_GROUPS
TOP_K = 2
D_EXPERT = 512
MOE_BLOCK = 256

ALPHA = (2.0 * DEPTH) ** 0.25

LANES = 128
VMEM_LIMIT = 56 * 1024 * 1024

BF16 = jnp.bfloat16
F32 = jnp.float32


def _params(n_axes):
    return pltpu.CompilerParams(dimension_semantics=("arbitrary",) * n_axes, vmem_limit_bytes=VMEM_LIMIT)


def _rot_half32(y):
    lane = lax.broadcasted_iota(jnp.int32, y.shape, y.ndim - 1)
    up = pltpu.roll(y, 96, axis=y.ndim - 1)
    dn = pltpu.roll(y, 32, axis=y.ndim - 1)
    return jnp.where((lane % 64) < 32, up, dn)


def _proj_heads_kernel(x_ref, w_ref, o_ref, *, heads_per_tile):
    acc = jnp.dot(x_ref[...], w_ref[...], preferred_element_type=F32)
    for h in range(heads_per_tile):
        o_ref[h] = acc[:, h * LANES:(h + 1) * LANES].astype(o_ref.dtype)


def proj_heads(x, w, *, heads_per_tile, bm=512):
    T, D = x.shape
    H = w.shape[1] // LANES
    bn = heads_per_tile * LANES
    grid = (H // heads_per_tile, T // bm)
    return pl.pallas_call(
        functools.partial(_proj_heads_kernel, heads_per_tile=heads_per_tile),
        grid=grid,
        in_specs=[pl.BlockSpec((bm, D), lambda j, i: (i, 0)),
                  pl.BlockSpec((D, bn), lambda j, i: (0, j))],
        out_specs=pl.BlockSpec((heads_per_tile, bm, LANES), lambda j, i: (j, i, 0)),
        out_shape=jax.ShapeDtypeStruct((H, T, LANES), BF16),
        compiler_params=_params(2),
        name="proj_heads",
    )(x, w)


def _proj_norm_rope_kernel(x_ref, w_ref, g_ref, cos_ref, sin_ref, o_ref):
    acc = jnp.dot(x_ref[...], w_ref[...], preferred_element_type=F32)
    cos = cos_ref[...]
    sin = sin_ref[...]
    for h in range(2):
        t = acc[:, h * LANES:(h + 1) * LANES]
        ms = jnp.mean(t * t, axis=-1, keepdims=True)
        y = t * lax.rsqrt(ms + RMS_EPS) * g_ref[:, h * LANES:(h + 1) * LANES]
        o_ref[h] = (y * cos + _rot_half32(y) * sin).astype(o_ref.dtype)


def proj_norm_rope(x, w, gain, cos, sin, *, seq, bm=512):
    T, D = x.shape
    H = w.shape[1] // LANES
    nsb = seq // bm
    grid = (H // 2, T // bm)
    return pl.pallas_call(
        _proj_norm_rope_kernel,
        grid=grid,
        in_specs=[pl.BlockSpec((bm, D), lambda j, i: (i, 0)),
                  pl.BlockSpec((D, 2 * LANES), lambda j, i: (0, j)),
                  pl.BlockSpec((1, 2 * LANES), lambda j, i: (0, j)),
                  pl.BlockSpec((bm, LANES), lambda j, i: (i % nsb, 0)),
                  pl.BlockSpec((bm, LANES), lambda j, i: (i % nsb, 0))],
        out_specs=pl.BlockSpec((2, bm, LANES), lambda j, i: (j, i, 0)),
        out_shape=jax.ShapeDtypeStruct((H, T, LANES), BF16),
        compiler_params=_params(2),
        name="proj_norm_rope",
    )(x, w, gain, cos, sin)


def _mla_proj_kernel(x_ref, wc_ref, gq_ref, gkv_ref, wq_ref, wkv_ref, cos_ref, sin_ref,
                     q_ref, k_ref, v_ref, *, scale):
    c = jnp.dot(x_ref[...], wc_ref[...], preferred_element_type=F32)
    cos = cos_ref[...]
    sin = sin_ref[...]

    def rms(t, g):
        ms = jnp.mean(t * t, axis=-1, keepdims=True)
        return t * lax.rsqrt(ms + RMS_EPS) * g

    cq = rms(c[:, :D_Q_RANK], gq_ref[...]).astype(BF16)
    ckv = rms(c[:, D_Q_RANK:D_Q_RANK + D_KV_RANK], gkv_ref[...]).astype(BF16)
    kr = c[:, D_Q_RANK + D_KV_RANK:]
    kr = (kr * cos + _rot_half32(kr) * sin).astype(k_ref.dtype)
    q = jnp.dot(cq, wq_ref[...], preferred_element_type=F32) * scale
    kv = jnp.dot(ckv, wkv_ref[...], preferred_element_type=F32)
    for h in range(D_HEADS):
        base = h * 2 * LANES
        q_ref[h, :, :LANES] = q[:, base:base + LANES].astype(q_ref.dtype)
        qr = q[:, base + LANES:base + 2 * LANES]
        q_ref[h, :, LANES:] = (qr * cos + _rot_half32(qr) * sin).astype(q_ref.dtype)
        k_ref[h, :, :LANES] = kv[:, base:base + LANES].astype(k_ref.dtype)
        k_ref[h, :, LANES:] = kr
        v_ref[h] = kv[:, base + LANES:base + 2 * LANES].astype(v_ref.dtype)


def mla_proj(x, wc, gq, gkv, wq, wkv, cos, sin, *, seq, scale, bm=512):
    T, D = x.shape
    nsb = seq // bm
    full = lambda shape: pl.BlockSpec(shape, lambda i: (0,) * len(shape))
    return pl.pallas_call(
        functools.partial(_mla_proj_kernel, scale=scale),
        grid=(T // bm,),
        in_specs=[pl.BlockSpec((bm, D), lambda i: (i, 0)),
                  full(wc.shape), full(gq.shape), full(gkv.shape), full(wq.shape), full(wkv.shape),
                  pl.BlockSpec((bm, LANES), lambda i: (i % nsb, 0)),
                  pl.BlockSpec((bm, LANES), lambda i: (i % nsb, 0))],
        out_specs=[pl.BlockSpec((D_HEADS, bm, 2 * LANES), lambda i: (0, i, 0)),
                   pl.BlockSpec((D_HEADS, bm, 2 * LANES), lambda i: (0, i, 0)),
                   pl.BlockSpec((D_HEADS, bm, LANES), lambda i: (0, i, 0))],
        out_shape=[jax.ShapeDtypeStruct((D_HEADS, T, 2 * LANES), BF16),
                   jax.ShapeDtypeStruct((D_HEADS, T, 2 * LANES), BF16),
                   jax.ShapeDtypeStruct((D_HEADS, T, LANES), BF16)],
        compiler_params=_params(1),
        name="mla_proj",
    )(x, wc, gq, gkv, wq, wkv, cos, sin)


def _fill_vext(vext_ref, v_ref):
    vext_ref[:, :LANES] = v_ref[...]
    vext_ref[:, LANES:] = jnp.ones((v_ref.shape[0], LANES), vext_ref.dtype)


def _normalise(o_ext):
    return o_ext[:, :LANES] / o_ext[:, LANES:]


def _dense_attn_kernel(q_ref, k_ref, v_ref, o_ref, vext_ref, *, tq):
    _fill_vext(vext_ref, v_ref)
    S = q_ref.shape[0]

    def step(i, carry):
        r = pl.multiple_of(i * tq, tq)
        q = q_ref[pl.ds(r, tq), :]
        s = lax.dot_general(q, k_ref[...], (((1,), (1,)), ((), ())), preferred_element_type=F32)
        m = jnp.max(s, axis=-1, keepdims=True)
        p = jnp.exp(s - m).astype(BF16)
        o_ext = jnp.dot(p, vext_ref[...], preferred_element_type=F32)
        o_ref[pl.ds(r, tq), :] = _normalise(o_ext).astype(o_ref.dtype)
        return carry

    lax.fori_loop(0, S // tq, step, 0)


def dense_attention(q, k, v, *, heads, q0, k0, v0, batch, seq, group, tq=256):
    T, dq = q.shape[1:]
    return pl.pallas_call(
        functools.partial(_dense_attn_kernel, tq=tq),
        grid=(batch, heads),
        in_specs=[pl.BlockSpec((None, seq, dq), lambda b, h: (q0 + h, b, 0)),
                  pl.BlockSpec((None, seq, dq), lambda b, h: (k0 + h // group, b, 0)),
                  pl.BlockSpec((None, seq, LANES), lambda b, h: (v0 + h // group, b, 0))],
        out_specs=pl.BlockSpec((seq, LANES), lambda b, h: (b, h)),
        out_shape=jax.ShapeDtypeStruct((T, heads * LANES), BF16),
        scratch_shapes=[pltpu.VMEM((seq, 2 * LANES), BF16)],
        compiler_params=_params(2),
        name="dense_attention",
    )(q, k, v)


DIL_REACH = max(w // 2 for w, _ in B_BRANCHES)
DIL_TQ = 128
DIL_WIN = 2 * DIL_REACH + DIL_TQ
DIL_TABLE_ROWS = DIL_WIN + 2 * DIL_REACH


def _dilated_attn_kernel(q_ref, k_ref, v_ref, bias_ref, o_ref, vext_ref, *, scale):
    _fill_vext(vext_ref, v_ref)
    S = q_ref.shape[0]

    def step(i, carry):
        t0 = pl.multiple_of(i * DIL_TQ, DIL_TQ)
        w0 = pl.multiple_of(jnp.clip(t0 - DIL_REACH, 0, S - DIL_WIN), DIL_TQ)
        boff = pl.multiple_of(w0 - t0 + 2 * DIL_REACH, DIL_TQ)
        q = q_ref[pl.ds(t0, DIL_TQ), :]
        kw = k_ref[pl.ds(w0, DIL_WIN), :]
        st = lax.dot_general(kw, q, (((1,), (1,)), ((), ())), preferred_element_type=F32)
        st = st * scale + bias_ref[pl.ds(boff, DIL_WIN), :]
        m = jnp.max(st, axis=0, keepdims=True)
        p = jnp.exp(st - m).astype(BF16)
        o_ext = lax.dot_general(p, vext_ref[pl.ds(w0, DIL_WIN), :], (((0,), (0,)), ((), ())),
                                preferred_element_type=F32)
        o_ref[pl.ds(t0, DIL_TQ), :] = _normalise(o_ext).astype(o_ref.dtype)
        return carry

    lax.fori_loop(0, S // DIL_TQ, step, 0)


def dilated_bias_table(n_heads):
    r = np.arange(DIL_TABLE_ROWS)[:, None]
    c = np.arange(DIL_TQ)[None, :]
    d = np.abs(r - c - 2 * DIL_REACH)
    mult = np.zeros(d.shape, np.float64)
    for window, dil in B_BRANCHES:
        mult += ((d % dil == 0) & (d <= window // 2)).astype(np.float64)
    with np.errstate(divide="ignore"):
        logm = np.log(mult)
    slopes = jnp.asarray(2.0 ** (-8.0 * np.arange(1, n_heads + 1) / n_heads), dtype=F32)
    return jnp.asarray(logm, dtype=F32)[None] - slopes[:, None, None] * jnp.asarray(d, dtype=F32)[None]


def dilated_attention(qkv, bias, *, heads, q0, k0, v0, batch, seq, scale):
    T = qkv.shape[1]
    return pl.pallas_call(
        functools.partial(_dilated_attn_kernel, scale=scale),
        grid=(batch, heads),
        in_specs=[pl.BlockSpec((None, seq, LANES), lambda b, h: (q0 + h, b, 0)),
                  pl.BlockSpec((None, seq, LANES), lambda b, h: (k0 + h, b, 0)),
                  pl.BlockSpec((None, seq, LANES), lambda b, h: (v0 + h, b, 0)),
                  pl.BlockSpec((None, DIL_TABLE_ROWS, DIL_TQ), lambda b, h: (h, 0, 0))],
        out_specs=pl.BlockSpec((seq, LANES), lambda b, h: (b, h)),
        out_shape=jax.ShapeDtypeStruct((T, heads * LANES), BF16),
        scratch_shapes=[pltpu.VMEM((seq, 2 * LANES), BF16)],
        compiler_params=_params(2),
        name="dilated_attention",
    )(qkv, qkv, qkv, bias)


def _na_attn_kernel(q_ref, k_ref, v_ref, bias_ref, o_ref, vext_ref, *, scale, rows):
    _fill_vext(vext_ref, v_ref)
    nk = NA_ROWS * GRID_W

    def step(r, carry):
        r0 = jnp.clip(r - NA_ROWS // 2, 0, rows - NA_ROWS)
        q0 = pl.multiple_of(r * GRID_W, GRID_W)
        k0 = pl.multiple_of(r0 * GRID_W, GRID_W)
        q = q_ref[pl.ds(q0, GRID_W), :]
        kw = k_ref[pl.ds(k0, nk), :]
        s = lax.dot_general(q, kw, (((1,), (1,)), ((), ())), preferred_element_type=F32)
        s = s * scale + bias_ref[r - r0]
        m = jnp.max(s, axis=-1, keepdims=True)
        p = jnp.exp(s - m).astype(BF16)
        o_ext = jnp.dot(p, vext_ref[pl.ds(k0, nk), :], preferred_element_type=F32)
        o_ref[pl.ds(q0, GRID_W), :] = _normalise(o_ext).astype(o_ref.dtype)
        return carry

    lax.fori_loop(0, rows, step, 0)


def na_bias_table(rpb):
    col = np.arange(GRID_W)
    c0 = np.clip(col - NA_COLS // 2, 0, GRID_W - NA_COLS)
    kcol = np.tile(col, NA_ROWS)
    krow = np.repeat(np.arange(NA_ROWS), GRID_W)
    col_ok = (kcol[None, :] >= c0[:, None]) & (kcol[None, :] < c0[:, None] + NA_COLS)
    dcol = np.clip(kcol[None, :] - col[:, None] + NA_COLS - 1, 0, 2 * NA_COLS - 2)
    a = np.arange(NA_ROWS)
    drow = krow[None, :] - a[:, None] + NA_ROWS - 1
    bias = rpb.astype(F32)[:, drow[:, None, :], dcol[None, :, :]]
    return jnp.where(jnp.asarray(col_ok)[None, None], bias, -jnp.inf)


def na_attention(qkv, bias, *, heads, q0, k0, v0, batch, seq, scale):
    T = qkv.shape[1]
    rows = seq // GRID_W
    nk = NA_ROWS * GRID_W
    return pl.pallas_call(
        functools.partial(_na_attn_kernel, scale=scale, rows=rows),
        grid=(batch, heads),
        in_specs=[pl.BlockSpec((None, seq, LANES), lambda b, h: (q0 + h, b, 0)),
                  pl.BlockSpec((None, seq, LANES), lambda b, h: (k0 + h, b, 0)),
                  pl.BlockSpec((None, seq, LANES), lambda b, h: (v0 + h, b, 0)),
                  pl.BlockSpec((None, NA_ROWS, GRID_W, nk), lambda b, h: (h, 0, 0, 0))],
        out_specs=pl.BlockSpec((seq, LANES), lambda b, h: (b, h)),
        out_shape=jax.ShapeDtypeStruct((T, heads * LANES), BF16),
        scratch_shapes=[pltpu.VMEM((seq, 2 * LANES), BF16)],
        compiler_params=_params(2),
        name="na_attention",
    )(qkv, qkv, qkv, bias)


def _layer_norm(z, g, b):
    mu = jnp.mean(z, axis=-1, keepdims=True)
    zc = z - mu
    var = jnp.mean(zc * zc, axis=-1, keepdims=True)
    return zc * lax.rsqrt(var + LN_EPS) * g + b


def _top2_of4(v0, v1, v2, v3):
    a = jnp.maximum(v0, v1)
    b = jnp.minimum(v0, v1)
    c = jnp.maximum(v2, v3)
    d = jnp.minimum(v2, v3)
    return jnp.maximum(a, c), jnp.maximum(jnp.minimum(a, c), jnp.maximum(b, d))


def _route(logits_t, rb):
    G = N_GROUPS
    scores = jax.nn.sigmoid(logits_t)
    sel = scores + rb
    sj = [sel[j * G:(j + 1) * G] for j in range(EXPERTS_PER_GROUP)]
    cj = [scores[j * G:(j + 1) * G] for j in range(EXPERTS_PER_GROUP)]
    t1, t2 = _top2_of4(*sj)
    grp = t1 + t2
    gi = lax.broadcasted_iota(jnp.int32, grp.shape, 0).astype(F32)
    gbest = jnp.min(jnp.where(grp == jnp.max(grp, axis=0, keepdims=True), gi, float(G)), axis=0, keepdims=True)
    pick = gi == gbest
    v = [jnp.sum(jnp.where(pick, s, 0.0), axis=0, keepdims=True) for s in sj]
    c = [jnp.sum(jnp.where(pick, s, 0.0), axis=0, keepdims=True) for s in cj]
    neg = jnp.float32(-jnp.inf)
    m1 = jnp.maximum(jnp.maximum(v[0], v[1]), jnp.maximum(v[2], v[3]))
    j1 = jnp.where(v[0] == m1, 0, jnp.where(v[1] == m1, 1, jnp.where(v[2] == m1, 2, 3)))
    w = [jnp.where(j1 == j, neg, v[j]) for j in range(4)]
    m2 = jnp.maximum(jnp.maximum(w[0], w[1]), jnp.maximum(w[2], w[3]))
    j2 = jnp.where(w[0] == m2, 0, jnp.where(w[1] == m2, 1, jnp.where(w[2] == m2, 2, 3)))
    g1 = jnp.where(j1 == 0, c[0], jnp.where(j1 == 1, c[1], jnp.where(j1 == 2, c[2], c[3])))
    g2 = jnp.where(j2 == 0, c[0], jnp.where(j2 == 1, c[1], jnp.where(j2 == 2, c[2], c[3])))
    tot = g1 + g2
    e0 = gbest.astype(jnp.int32) * EXPERTS_PER_GROUP
    return (e0 + j1, e0 + j2), (g1 / tot, g2 / tot)


def _out_ln_router_kernel(a_ref, b_ref, w_ref, x_ref, g_ref, beta_ref, rw_ref, rb_ref,
                          x1_ref, eid_ref, gate_ref):
    half = a_ref.shape[1]
    mix = jnp.dot(a_ref[...], w_ref[:half, :], preferred_element_type=F32)
    mix = mix + jnp.dot(b_ref[...], w_ref[half:, :], preferred_element_type=F32)
    x1 = _layer_norm(ALPHA * x_ref[...] + mix, g_ref[...], beta_ref[...])
    x1_ref[...] = x1
    logits_t = lax.dot_general(rw_ref[...], x1, (((1,), (1,)), ((), ())),
                               precision=lax.Precision.HIGHEST, preferred_element_type=F32)
    eid, gate = _route(logits_t, rb_ref[...])
    for k in range(TOP_K):
        eid_ref[k:k + 1, :] = eid[k]
        gate_ref[k:k + 1, :] = gate[k]


def out_ln_router(a, b, w, x, g, beta, rw_t, rb, *, bm=256):
    T, D = x.shape
    half = a.shape[1]
    full = lambda shape: pl.BlockSpec(shape, lambda i: (0,) * len(shape))
    return pl.pallas_call(
        _out_ln_router_kernel,
        grid=(T // bm,),
        in_specs=[pl.BlockSpec((bm, half), lambda i: (i, 0)),
                  pl.BlockSpec((bm, half), lambda i: (i, 0)),
                  full(w.shape),
                  pl.BlockSpec((bm, D), lambda i: (i, 0)),
                  full(g.shape), full(beta.shape), full(rw_t.shape), full(rb.shape)],
        out_specs=[pl.BlockSpec((bm, D), lambda i: (i, 0)),
                   pl.BlockSpec((TOP_K, bm), lambda i: (0, i)),
                   pl.BlockSpec((TOP_K, bm), lambda i: (0, i))],
        out_shape=[jax.ShapeDtypeStruct((T, D), F32),
                   jax.ShapeDtypeStruct((TOP_K, T), jnp.int32),
                   jax.ShapeDtypeStruct((TOP_K, T), F32)],
        compiler_params=_params(1),
        name="out_ln_router",
    )(a, b, w, x, g, beta, rw_t, rb)


def _row_copy(src_hbm, row, dst_vmem, j, sem):
    return pltpu.make_async_copy(src_hbm.at[pl.ds(row, 1), :], dst_vmem.at[pl.ds(j, 1), :], sem)


def _moe_ffn_kernel(blk_e_ref, nblk_ref, tok_ref, x_hbm, wg_ref, wu_ref, wd_ref, y_ref, xbuf, sem):
    i = pl.program_id(0)

    @pl.when(i < nblk_ref[0])
    def _():
        base = i * MOE_BLOCK

        def issue(j, carry):
            _row_copy(x_hbm, tok_ref[base + j], xbuf, j, sem).start()
            return carry

        lax.fori_loop(0, MOE_BLOCK, issue, 0)

        def drain(j, carry):
            _row_copy(x_hbm, 0, xbuf, j, sem).wait()
            return carry

        lax.fori_loop(0, MOE_BLOCK, drain, 0)
        xb = xbuf[...].astype(BF16)
        hg = jnp.dot(xb, wg_ref[...], preferred_element_type=F32)
        hu = jnp.dot(xb, wu_ref[...], preferred_element_type=F32)
        hb = (jax.nn.silu(hg) * hu).astype(BF16)
        y_ref[...] = jnp.dot(hb, wd_ref[...], preferred_element_type=F32)

    @pl.when(i >= nblk_ref[0])
    def _():
        y_ref[...] = jnp.zeros(y_ref.shape, y_ref.dtype)


def moe_ffn(x1, w_gate, w_up, w_down, blk_e, nblk, row_tok):
    T, D = x1.shape
    n_blocks = blk_e.shape[0]
    n_rows = n_blocks * MOE_BLOCK
    grid_spec = pltpu.PrefetchScalarGridSpec(
        num_scalar_prefetch=3,
        grid=(n_blocks,),
        in_specs=[pl.BlockSpec(memory_space=pl.ANY),
                  pl.BlockSpec((None, D, D_EXPERT), lambda i, be, nb, tk: (be[i], 0, 0)),
                  pl.BlockSpec((None, D, D_EXPERT), lambda i, be, nb, tk: (be[i], 0, 0)),
                  pl.BlockSpec((None, D_EXPERT, D), lambda i, be, nb, tk: (be[i], 0, 0))],
        out_specs=pl.BlockSpec((MOE_BLOCK, D), lambda i, be, nb, tk: (i, 0)),
        scratch_shapes=[pltpu.VMEM((MOE_BLOCK, D), F32), pltpu.SemaphoreType.DMA(())],
    )
    return pl.pallas_call(
        _moe_ffn_kernel,
        grid_spec=grid_spec,
        out_shape=jax.ShapeDtypeStruct((n_rows, D), F32),
        compiler_params=_params(1),
        name="moe_ffn",
    )(blk_e, nblk, row_tok, x1, w_gate, w_up, w_down)


def _combine_ln_kernel(dest_ref, y_hbm, x1_ref, gate_ref, g_ref, beta_ref, x2_ref, x2b_ref, ybuf, sem, *, bm, n_tok):
    i = pl.program_id(0)
    base = i * bm

    def issue(j, carry):
        for k in range(TOP_K):
            _row_copy(y_hbm, dest_ref[k * n_tok + base + j], ybuf.at[k], j, sem).start()
        return carry

    lax.fori_loop(0, bm, issue, 0)

    def drain(j, carry):
        for k in range(TOP_K):
            _row_copy(y_hbm, 0, ybuf.at[k], j, sem).wait()
        return carry

    lax.fori_loop(0, bm, drain, 0)
    gate = gate_ref[...]
    moe = ybuf[0] * gate[:, 0:1] + ybuf[1] * gate[:, 1:2]
    x2 = _layer_norm(ALPHA * x1_ref[...] + moe, g_ref[...], beta_ref[...])
    x2_ref[...] = x2
    x2b_ref[...] = x2.astype(BF16)


def combine_ln(y_rows, dest, x1, gate_t, g, beta, *, bm=256):
    T, D = x1.shape
    grid_spec = pltpu.PrefetchScalarGridSpec(
        num_scalar_prefetch=1,
        grid=(T // bm,),
        in_specs=[pl.BlockSpec(memory_space=pl.ANY),
                  pl.BlockSpec((bm, D), lambda i, d: (i, 0)),
                  pl.BlockSpec((bm, TOP_K), lambda i, d: (i, 0)),
                  pl.BlockSpec((1, D), lambda i, d: (0, 0)),
                  pl.BlockSpec((1, D), lambda i, d: (0, 0))],
        out_specs=[pl.BlockSpec((bm, D), lambda i, d: (i, 0)),
                   pl.BlockSpec((bm, D), lambda i, d: (i, 0))],
        scratch_shapes=[pltpu.VMEM((TOP_K, bm, D), F32), pltpu.SemaphoreType.DMA(())],
    )
    return pl.pallas_call(
        functools.partial(_combine_ln_kernel, bm=bm, n_tok=T),
        grid_spec=grid_spec,
        out_shape=[jax.ShapeDtypeStruct((T, D), F32), jax.ShapeDtypeStruct((T, D), BF16)],
        compiler_params=_params(1),
        name="combine_ln",
    )(dest, y_rows, x1, gate_t, g, beta)


def moe_plan(eid):
    T = eid.shape[1]
    n_assign = TOP_K * T
    n_blocks = (n_assign + N_EXPERTS * (MOE_BLOCK - 1) + MOE_BLOCK - 1) // MOE_BLOCK
    flat_e = eid.reshape(n_assign)
    onehot = (flat_e[:, None] == jnp.arange(N_EXPERTS, dtype=jnp.int32)[None, :]).astype(jnp.int32)
    csum = jnp.cumsum(onehot, axis=0)
    rank = jnp.sum(csum * onehot, axis=1) - 1
    counts = csum[-1]
    pcounts = (counts + MOE_BLOCK - 1) // MOE_BLOCK * MOE_BLOCK
    pends = jnp.cumsum(pcounts)
    pstarts = pends - pcounts
    dest = pstarts[flat_e] + rank
    tok = jnp.tile(jnp.arange(T, dtype=jnp.int32), TOP_K)
    row_tok = jnp.zeros((n_blocks * MOE_BLOCK,), jnp.int32).at[dest].set(tok)
    blk_start = jnp.arange(n_blocks, dtype=jnp.int32) * MOE_BLOCK
    blk_e = jnp.minimum(jnp.searchsorted(pends, blk_start, side="right"), N_EXPERTS - 1).astype(jnp.int32)
    nblk = (pends[-1:] // MOE_BLOCK).astype(jnp.int32)
    return dest.astype(jnp.int32), row_tok, blk_e, nblk


def moe_layer(x1, eid, gate, w_gate, w_up, w_down, ln_g, ln_b):
    dest, row_tok, blk_e, nblk = moe_plan(eid)
    y_rows = moe_ffn(x1, w_gate.astype(BF16), w_up.astype(BF16), w_down.astype(BF16), blk_e, nblk, row_tok)
    return combine_ln(y_rows, dest, x1, gate.T, ln_g.reshape(1, -1), ln_b.reshape(1, -1))


def _rope_cos_sin(pos, dim):
    inv_freq = ROPE_THETA ** (-jnp.arange(0, dim, 2, dtype=F32) / dim)
    ang = pos.astype(F32)[:, None] * inv_freq[None, :]
    cos = jnp.cos(ang)
    sin = jnp.sin(ang)
    return jnp.concatenate([cos, cos], axis=-1), jnp.concatenate([-sin, sin], axis=-1)


def axial_rope_tables(seq):
    pos = jnp.arange(seq)
    half = HEAD_DIM // 2
    cr, sr = _rope_cos_sin(pos // GRID_W, half)
    cc, sc = _rope_cos_sin(pos % GRID_W, half)
    return jnp.concatenate([cr, cc], axis=-1), jnp.concatenate([sr, sc], axis=-1)


def mla_rope_tables(seq):
    c, s = _rope_cos_sin(jnp.arange(seq), D_ROPE)
    z = jnp.zeros((seq, LANES - D_ROPE), F32)
    return jnp.concatenate([c, z], axis=-1), jnp.concatenate([s, z], axis=-1)


def router_layout(router_w, router_b):
    perm = np.array([g * EXPERTS_PER_GROUP + j for j in range(EXPERTS_PER_GROUP) for g in range(N_GROUPS)])
    return router_w.astype(F32).T[perm], router_b.astype(F32)[perm].reshape(N_EXPERTS, 1)


def mla_up_layout(w_q_up, w_kv_up, w_in_tail):
    rq = w_q_up.shape[0]
    wq = w_q_up.reshape(rq, D_HEADS, D_NOPE + D_ROPE)
    wq = jnp.pad(wq, ((0, 0), (0, 0), (0, 2 * LANES - (D_NOPE + D_ROPE)))).reshape(rq, D_HEADS * 2 * LANES)
    wc = jnp.pad(w_in_tail, ((0, 0), (0, LANES - D_ROPE)))
    return wq.astype(BF16), w_kv_up.astype(BF16), wc.astype(BF16)


def kernel(x, router_w, router_b, l0_w_in, l0_a_q_norm, l0_a_k_norm, l0_w_out, l0_ln1_g, l0_ln1_b, l0_w_gate, l0_w_up, l0_w_down, l0_ln2_g, l0_ln2_b, l1_w_in, l1_c_rpb, l1_d_q_norm, l1_d_w_q_up, l1_d_kv_norm, l1_d_w_kv_up, l1_w_out, l1_ln1_g, l1_ln1_b, l1_w_gate, l1_w_up, l1_w_down, l1_ln2_g, l1_ln2_b):
    batch, seq, d = x.shape
    T = batch * seq
    xf = x.reshape(T, d)
    rw_t, rb = router_layout(router_w, router_b)
    row = lambda v: v.astype(F32).reshape(1, -1)

    xb = xf.astype(BF16)
    a_q_dim = A_HEADS * HEAD_DIM
    a_qk_dim = a_q_dim + A_KV_HEADS * HEAD_DIM
    w0 = l0_w_in.astype(BF16)
    gain = jnp.concatenate([jnp.tile(l0_a_q_norm.astype(F32), A_HEADS) * (HEAD_DIM ** -0.5),
                            jnp.tile(l0_a_k_norm.astype(F32), A_KV_HEADS)]).reshape(1, -1)
    cos_a, sin_a = axial_rope_tables(seq)
    qk_a = proj_norm_rope(xb, w0[:, :a_qk_dim], gain, cos_a, sin_a, seq=seq)
    rest0 = proj_heads(xb, w0[:, a_qk_dim:], heads_per_tile=13)
    oa = dense_attention(qk_a, qk_a, rest0, heads=A_HEADS, q0=0, k0=A_HEADS, v0=0, batch=batch, seq=seq,
                         group=A_HEADS // A_KV_HEADS)
    nb = A_KV_HEADS
    ob = dilated_attention(rest0, dilated_bias_table(B_HEADS), heads=B_HEADS, q0=nb, k0=nb + B_HEADS,
                           v0=nb + 2 * B_HEADS, batch=batch, seq=seq, scale=HEAD_DIM ** -0.5)
    x1, eid, gate = out_ln_router(oa, ob, l0_w_out.astype(BF16), xf, row(l0_ln1_g), row(l0_ln1_b), rw_t, rb)
    x2, x2b = moe_layer(x1, eid, gate, l0_w_gate, l0_w_up, l0_w_down, l0_ln2_g, l0_ln2_b)

    c_dim = C_HEADS * HEAD_DIM
    w1 = l1_w_in
    qkv_c = proj_heads(x2b, w1[:, :3 * c_dim].astype(BF16), heads_per_tile=12)
    oc = na_attention(qkv_c, na_bias_table(l1_c_rpb), heads=C_HEADS, q0=0, k0=C_HEADS, v0=2 * C_HEADS,
                      batch=batch, seq=seq, scale=HEAD_DIM ** -0.5)
    wq, wkv, wc = mla_up_layout(l1_d_w_q_up, l1_d_w_kv_up, w1[:, 3 * c_dim:])
    cos_d, sin_d = mla_rope_tables(seq)
    qd, kd, vd = mla_proj(x2b, wc, row(l1_d_q_norm), row(l1_d_kv_norm), wq, wkv, cos_d, sin_d, seq=seq,
                          scale=(D_NOPE + D_ROPE) ** -0.5)
    od = dense_attention(qd, kd, vd, heads=D_HEADS, q0=0, k0=0, v0=0, batch=batch, seq=seq, group=1)
    x3, eid, gate = out_ln_router(oc, od, l1_w_out.astype(BF16), x2, row(l1_ln1_g), row(l1_ln1_b), rw_t, rb)
    x4, _ = moe_layer(x3, eid, gate, l1_w_gate, l1_w_up, l1_w_down, l1_ln2_g, l1_ln2_b)
    return x4.reshape(batch, seq, d)
```

```python
import functools
import math

import jax
import jax.numpy as jnp
import numpy as np
from jax import lax
from jax.experimental import pallas as pl
from jax.experimental.pallas import tpu as pltpu

D_MODEL = 2048
DEPTH = 2
GRID_W = 64
HEAD_DIM = 128
ROPE_THETA = 10000.0
RMS_EPS = 1e-6
LN_EPS = 1e-5

A_HEADS = 8
A_KV_HEADS = 2
B_HEADS = 8
B_BRANCHES = ((128, 1), (512, 4), (2048, 16))
C_HEADS = 8
NA_ROWS = 8
NA_COLS = 16
D_HEADS = 8
D_Q_RANK = 512
D_KV_RANK = 256
D_NOPE = 128
D_ROPE = 64
D_V = 128

N_EXPERTS = 32
N_GROUPS = 8
EXPERTS_PER_GROUP = N_EXPERTS // N_GROUPS
TOP_K = 2
D_EXPERT = 512
MOE_BLOCK = 256

ALPHA = (2.0 * DEPTH) ** 0.25

LANES = 128
VMEM_LIMIT = 56 * 1024 * 1024

BF16 = jnp.bfloat16
F32 = jnp.float32


def _params(n_axes):
    return pltpu.CompilerParams(dimension_semantics=("arbitrary",) * n_axes, vmem_limit_bytes=VMEM_LIMIT)


def _rot_half32(y):
    lane = lax.broadcasted_iota(jnp.int32, y.shape, y.ndim - 1)
    up = pltpu.roll(y, 96, axis=y.ndim - 1)
    dn = pltpu.roll(y, 32, axis=y.ndim - 1)
    return jnp.where((lane % 64) < 32, up, dn)


def _proj_heads_kernel(x_ref, w_ref, c_ref, o_ref, *, heads_per_tile):
    acc = jnp.dot(x_ref[...], w_ref[...], preferred_element_type=F32) * c_ref[...]
    for h in range(heads_per_tile):
        o_ref[h] = acc[:, h * LANES:(h + 1) * LANES].astype(o_ref.dtype)


def proj_heads(x, w, col_scale, *, heads_per_tile, bm=512):
    T, D = x.shape
    H = w.shape[1] // LANES
    bn = heads_per_tile * LANES
    grid = (H // heads_per_tile, T // bm)
    return pl.pallas_call(
        functools.partial(_proj_heads_kernel, heads_per_tile=heads_per_tile),
        grid=grid,
        in_specs=[pl.BlockSpec((bm, D), lambda j, i: (i, 0)),
                  pl.BlockSpec((D, bn), lambda j, i: (0, j)),
                  pl.BlockSpec((1, bn), lambda j, i: (0, j))],
        out_specs=pl.BlockSpec((heads_per_tile, bm, LANES), lambda j, i: (j, i, 0)),
        out_shape=jax.ShapeDtypeStruct((H, T, LANES), BF16),
        compiler_params=_params(2),
        name="proj_heads",
    )(x, w, col_scale)


def _resident(shape):
    return pl.BlockSpec(shape, lambda *_: (0,) * len(shape), pipeline_mode=pl.Buffered(1))


def _proj_norm_rope_kernel(x_ref, w_ref, g_ref, cos_ref, sin_ref, o_ref):
    acc = jnp.dot(x_ref[...], w_ref[...], preferred_element_type=F32)
    cos = cos_ref[...]
    sin = sin_ref[...]
    for h in range(o_ref.shape[0]):
        t = acc[:, h * LANES:(h + 1) * LANES]
        ms = jnp.mean(t * t, axis=-1, keepdims=True)
        y = t * lax.rsqrt(ms + RMS_EPS) * g_ref[:, h * LANES:(h + 1) * LANES]
        o_ref[h] = (y * cos + _rot_half32(y) * sin).astype(o_ref.dtype)


def proj_norm_rope(x, w, gain, cos, sin, *, seq, bm=512):
    T, D = x.shape
    H = w.shape[1] // LANES
    nsb = seq // bm
    return pl.pallas_call(
        _proj_norm_rope_kernel,
        grid=(T // bm,),
        in_specs=[pl.BlockSpec((bm, D), lambda i: (i, 0)),
                  _resident(w.shape), _resident(gain.shape),
                  pl.BlockSpec((bm, LANES), lambda i: (i % nsb, 0)),
                  pl.BlockSpec((bm, LANES), lambda i: (i % nsb, 0))],
        out_specs=pl.BlockSpec((H, bm, LANES), lambda i: (0, i, 0)),
        out_shape=jax.ShapeDtypeStruct((H, T, LANES), BF16),
        compiler_params=_params(1),
        name="proj_norm_rope",
    )(x, w, gain, cos, sin)


def _mla_proj_kernel(x_ref, wc_ref, gq_ref, gkv_ref, wq_ref, wkv_ref, cos_ref, sin_ref,
                     q_ref, k_ref, v_ref, *, scale):
    c = jnp.dot(x_ref[...], wc_ref[...], preferred_element_type=F32)
    cos = cos_ref[...]
    sin = sin_ref[...]

    def rms(t, g):
        ms = jnp.mean(t * t, axis=-1, keepdims=True)
        return t * lax.rsqrt(ms + RMS_EPS) * g

    cq = rms(c[:, :D_Q_RANK], gq_ref[...]).astype(BF16)
    ckv = rms(c[:, D_Q_RANK:D_Q_RANK + D_KV_RANK], gkv_ref[...]).astype(BF16)
    kr = c[:, D_Q_RANK + D_KV_RANK:]
    kr = (kr * cos + _rot_half32(kr) * sin).astype(k_ref.dtype)
    q = jnp.dot(cq, wq_ref[...], preferred_element_type=F32) * scale
    kv = jnp.dot(ckv, wkv_ref[...], preferred_element_type=F32)
    for h in range(D_HEADS):
        base = h * 2 * LANES
        q_ref[h, :, :LANES] = q[:, base:base + LANES].astype(q_ref.dtype)
        qr = q[:, base + LANES:base + 2 * LANES]
        q_ref[h, :, LANES:] = (qr * cos + _rot_half32(qr) * sin).astype(q_ref.dtype)
        k_ref[h, :, :LANES] = kv[:, base:base + LANES].astype(k_ref.dtype)
        k_ref[h, :, LANES:] = kr
        v_ref[h] = kv[:, base + LANES:base + 2 * LANES].astype(v_ref.dtype)


def mla_proj(x, wc, gq, gkv, wq, wkv, cos, sin, *, seq, scale, bm=512):
    T, D = x.shape
    nsb = seq // bm
    return pl.pallas_call(
        functools.partial(_mla_proj_kernel, scale=scale),
        grid=(T // bm,),
        in_specs=[pl.BlockSpec((bm, D), lambda i: (i, 0)),
                  _resident(wc.shape), _resident(gq.shape), _resident(gkv.shape), _resident(wq.shape),
                  _resident(wkv.shape),
                  pl.BlockSpec((bm, LANES), lambda i: (i % nsb, 0)),
                  pl.BlockSpec((bm, LANES), lambda i: (i % nsb, 0))],
        out_specs=[pl.BlockSpec((D_HEADS, bm, 2 * LANES), lambda i: (0, i, 0)),
                   pl.BlockSpec((D_HEADS, bm, 2 * LANES), lambda i: (0, i, 0)),
                   pl.BlockSpec((D_HEADS, bm, LANES), lambda i: (0, i, 0))],
        out_shape=[jax.ShapeDtypeStruct((D_HEADS, T, 2 * LANES), BF16),
                   jax.ShapeDtypeStruct((D_HEADS, T, 2 * LANES), BF16),
                   jax.ShapeDtypeStruct((D_HEADS, T, LANES), BF16)],
        compiler_params=_params(1),
        name="mla_proj",
    )(x, wc, gq, gkv, wq, wkv, cos, sin)


ATTN_TQ = 256


def _attn_pipeline(q_ref, k_ref, v_ref, o_ref, vext_ref, s_buf, m_buf, p_buf, *, heads, group, win, win_start, bias):
    tq = ATTN_TQ
    S = q_ref.shape[1]
    nblk = S // tq
    total = heads * nblk
    vext_ref[:, :, :LANES] = v_ref[...]
    vext_ref[:, :, LANES:] = jnp.ones(v_ref.shape, vext_ref.dtype)

    def locate(n):
        h, blk = (0, n) if heads == 1 else (n // nblk, n % nblk)
        rows = pl.ds(blk * tq, tq) if isinstance(blk, int) else pl.ds(pl.multiple_of(blk * tq, tq), tq)
        keys = slice(None) if win == S else pl.ds(win_start(blk), win)
        return h, blk, rows, keys

    def scores(n, par):
        h, blk, rows, keys = locate(n)
        s = lax.dot_general(q_ref[h, rows, :], k_ref[h // group, keys, :], (((1,), (1,)), ((), ())),
                            preferred_element_type=F32)
        if bias is not None:
            s = s + bias(h, blk)
        s_buf[par][...] = s
        m_buf[par][...] = jnp.max(s, axis=-1, keepdims=True)

    def exps(par):
        p_buf[par][...] = jnp.exp(s_buf[par][...] - m_buf[par][...]).astype(BF16)

    def wsum(n, par):
        h, blk, rows, keys = locate(n)
        o_ext = jnp.dot(p_buf[par][...], vext_ref[h // group, keys, :], preferred_element_type=F32)
        o_ref[h, rows, :] = (o_ext[:, :LANES] / o_ext[:, LANES:]).astype(o_ref.dtype)

    scores(0, 0)
    exps(0)
    scores(1, 1)

    def step(j, carry):
        n = 2 * j + 1
        exps(1)
        wsum(n - 1, 0)
        scores(n + 1, 0)
        exps(0)
        wsum(n, 1)
        scores(n + 2, 1)
        return carry

    lax.fori_loop(0, total // 2 - 1, step, 0)
    exps(1)
    wsum(total - 2, 0)
    wsum(total - 1, 1)


def _attn_scratch(kv_heads, seq, win):
    tq = ATTN_TQ
    return [pltpu.VMEM((kv_heads, seq, 2 * LANES), BF16),
            pltpu.VMEM((tq, win), F32), pltpu.VMEM((tq, win), F32),
            pltpu.VMEM((tq, 1), F32), pltpu.VMEM((tq, 1), F32),
            pltpu.VMEM((tq, win), BF16), pltpu.VMEM((tq, win), BF16)]


def _dense_attn_kernel(q_ref, k_ref, v_ref, o_ref, vext_ref, s0, s1, m0, m1, p0, p1, *, heads, group):
    _attn_pipeline(q_ref, k_ref, v_ref, o_ref, vext_ref, (s0, s1), (m0, m1), (p0, p1), heads=heads, group=group,
                   win=q_ref.shape[1], win_start=None, bias=None)


def dense_attention(q, k, v, *, n_heads, heads, k0, v0, batch, seq, group):
    T, dq = q.shape[1:]
    kvh = heads // group
    assert n_heads % heads == 0 and heads % group == 0 and k0 % kvh == 0 and v0 % kvh == 0
    return pl.pallas_call(
        functools.partial(_dense_attn_kernel, heads=heads, group=group),
        grid=(batch, n_heads // heads),
        in_specs=[pl.BlockSpec((heads, seq, dq), lambda b, g: (g, b, 0)),
                  pl.BlockSpec((kvh, seq, dq), lambda b, g: (k0 // kvh + g, b, 0)),
                  pl.BlockSpec((kvh, seq, LANES), lambda b, g: (v0 // kvh + g, b, 0))],
        out_specs=pl.BlockSpec((heads, seq, LANES), lambda b, g: (g, b, 0)),
        out_shape=jax.ShapeDtypeStruct((n_heads, T, LANES), BF16),
        scratch_shapes=_attn_scratch(kvh, seq, seq),
        compiler_params=_params(2),
        name="dense_attention",
    )(q, k, v)


DIL_REACH = max(w // 2 for w, _ in B_BRANCHES)
DIL_WIN = 2 * DIL_REACH + ATTN_TQ
DIL_TABLE_TILES = (DIL_WIN + 2 * DIL_REACH) // LANES


def _dilated_attn_kernel(q_ref, k_ref, v_ref, bias_ref, o_ref, vext_ref, s0, s1, m0, m1, p0, p1, *, heads):
    S = q_ref.shape[1]

    def win_start(blk):
        return pl.multiple_of(jnp.clip(blk * ATTN_TQ - DIL_REACH, 0, S - DIL_WIN), ATTN_TQ)

    def bias(h, blk):
        tile0 = (win_start(blk) - blk * ATTN_TQ + 2 * DIL_REACH) // LANES
        return jnp.concatenate([bias_ref[h, tile0 + t] for t in range(DIL_WIN // LANES)], axis=1)

    _attn_pipeline(q_ref, k_ref, v_ref, o_ref, vext_ref, (s0, s1), (m0, m1), (p0, p1), heads=heads, group=1,
                   win=DIL_WIN, win_start=win_start, bias=bias)


def dilated_bias_table(n_heads):
    col = (np.arange(DIL_TABLE_TILES)[:, None, None] * LANES + np.arange(LANES)[None, None, :])
    d = np.abs(col - np.arange(ATTN_TQ)[None, :, None] - 2 * DIL_REACH)
    mult = np.zeros(d.shape, np.float64)
    for window, dil in B_BRANCHES:
        mult += ((d % dil == 0) & (d <= window // 2)).astype(np.float64)
    with np.errstate(divide="ignore"):
        logm = np.log(mult)
    slopes = jnp.asarray(2.0 ** (-8.0 * np.arange(1, n_heads + 1) / n_heads), dtype=F32)
    return jnp.asarray(logm, dtype=F32)[None] - slopes[:, None, None, None] * jnp.asarray(d, dtype=F32)[None]


def dilated_attention(qkv, bias, *, n_heads, heads, q0, k0, v0, batch, seq):
    T = qkv.shape[1]
    assert n_heads % heads == 0 and q0 % heads == 0 and k0 % heads == 0 and v0 % heads == 0
    qkv_spec = lambda h0: pl.BlockSpec((heads, seq, LANES), lambda b, g: (h0 // heads + g, b, 0))
    return pl.pallas_call(
        functools.partial(_dilated_attn_kernel, heads=heads),
        grid=(batch, n_heads // heads),
        in_specs=[qkv_spec(q0), qkv_spec(k0), qkv_spec(v0),
                  pl.BlockSpec((heads, DIL_TABLE_TILES, ATTN_TQ, LANES), lambda b, g: (g, 0, 0, 0))],
        out_specs=pl.BlockSpec((heads, seq, LANES), lambda b, g: (g, b, 0)),
        out_shape=jax.ShapeDtypeStruct((n_heads, T, LANES), BF16),
        scratch_shapes=_attn_scratch(heads, seq, DIL_WIN),
        compiler_params=_params(2),
        name="dilated_attention",
    )(qkv, qkv, qkv, bias)


NA_QROWS = ATTN_TQ // GRID_W
NA_KROWS = NA_QROWS + NA_ROWS
NA_WIN = NA_KROWS * GRID_W


def _na_window_row(blk, rows):
    lo = blk * NA_QROWS - NA_ROWS // 2
    return jnp.clip(lo, 0, rows - NA_KROWS) if not isinstance(blk, int) else min(max(lo, 0), rows - NA_KROWS)


def _na_attn_kernel(q_ref, k_ref, v_ref, bias_ref, o_ref, vext_ref, s0, s1, m0, m1, p0, p1, *, heads, rows):
    nblk = rows // NA_QROWS

    def win_start(blk):
        return pl.multiple_of(_na_window_row(blk, rows) * GRID_W, GRID_W)

    def bias(h, blk):
        cls = (blk > 0).astype(jnp.int32) + (blk == nblk - 1).astype(jnp.int32) if not isinstance(blk, int) \
            else int(blk > 0) + int(blk == nblk - 1)
        return bias_ref[h, cls]

    _attn_pipeline(q_ref, k_ref, v_ref, o_ref, vext_ref, (s0, s1), (m0, m1), (p0, p1), heads=heads, group=1,
                   win=NA_WIN, win_start=win_start, bias=bias)


def na_bias_table(rpb, rows):
    H = rpb.shape[0]
    col = np.arange(GRID_W)
    c0 = np.clip(col - NA_COLS // 2, 0, GRID_W - NA_COLS)
    col_ok = (col[None, :] >= c0[:, None]) & (col[None, :] < c0[:, None] + NA_COLS)
    dcol = np.clip(col[None, :] - col[:, None] + NA_COLS - 1, 0, 2 * NA_COLS - 2)
    pick = ((dcol[:, :, None] == np.arange(2 * NA_COLS - 1)) & col_ok[:, :, None]).astype(np.float32)
    t = jnp.einsum("hrd,qkd->hrqk", rpb.astype(F32), jnp.asarray(pick), precision=lax.Precision.HIGHEST)
    t = jnp.where(jnp.asarray(col_ok)[None, None], t, -jnp.inf)
    nblk = rows // NA_QROWS

    def geometry(blk):
        r = blk * NA_QROWS + np.arange(NA_QROWS)
        kr = _na_window_row(blk, rows) + np.arange(NA_KROWS)
        r0 = np.clip(r - NA_ROWS // 2, 0, rows - NA_ROWS)
        valid = (kr[None, :] >= r0[:, None]) & (kr[None, :] < r0[:, None] + NA_ROWS)
        drow = np.clip(kr[None, :] - r[:, None] + NA_ROWS - 1, 0, 2 * NA_ROWS - 2)
        return valid, np.where(valid, drow, 0)

    inner = geometry(1)
    assert all(np.array_equal(a, b) for blk in range(1, nblk - 1) for a, b in zip(geometry(blk), inner))
    tables = []
    for blk in (0, 1, nblk - 1):
        valid, drow = geometry(blk)
        slab = jnp.where(jnp.asarray(valid)[None, :, :, None, None], t[:, drow], -jnp.inf)
        tables.append(slab.transpose(0, 1, 3, 2, 4).reshape(H, ATTN_TQ, NA_WIN))
    return jnp.stack(tables, axis=1)


def na_attention(qkv, bias, *, n_heads, heads, q0, k0, v0, batch, seq):
    T = qkv.shape[1]
    rows = seq // GRID_W
    assert n_heads % heads == 0 and q0 % heads == 0 and k0 % heads == 0 and v0 % heads == 0
    assert rows % NA_QROWS == 0 and rows // NA_QROWS >= 3
    qkv_spec = lambda h0: pl.BlockSpec((heads, seq, LANES), lambda b, g: (h0 // heads + g, b, 0))
    return pl.pallas_call(
        functools.partial(_na_attn_kernel, heads=heads, rows=rows),
        grid=(batch, n_heads // heads),
        in_specs=[qkv_spec(q0), qkv_spec(k0), qkv_spec(v0),
                  pl.BlockSpec((heads, 3, ATTN_TQ, NA_WIN), lambda b, g: (g, 0, 0, 0))],
        out_specs=pl.BlockSpec((heads, seq, LANES), lambda b, g: (g, b, 0)),
        out_shape=jax.ShapeDtypeStruct((n_heads, T, LANES), BF16),
        scratch_shapes=_attn_scratch(heads, seq, NA_WIN),
        compiler_params=_params(2),
        name="na_attention",
    )(qkv, qkv, qkv, bias)


def _layer_norm(z, g, b):
    mu = jnp.mean(z, axis=-1, keepdims=True)
    zc = z - mu
    var = jnp.mean(zc * zc, axis=-1, keepdims=True)
    return zc * lax.rsqrt(var + LN_EPS) * g + b


def _top2_of4(v0, v1, v2, v3):
    a = jnp.maximum(v0, v1)
    b = jnp.minimum(v0, v1)
    c = jnp.maximum(v2, v3)
    d = jnp.minimum(v2, v3)
    return jnp.maximum(a, c), jnp.maximum(jnp.minimum(a, c), jnp.maximum(b, d))


def _route(logits_t, rb):
    G = N_GROUPS
    scores = jax.nn.sigmoid(logits_t)
    sel = scores + rb
    sj = [sel[j * G:(j + 1) * G] for j in range(EXPERTS_PER_GROUP)]
    cj = [scores[j * G:(j + 1) * G] for j in range(EXPERTS_PER_GROUP)]
    t1, t2 = _top2_of4(*sj)
    grp = t1 + t2
    gi = lax.broadcasted_iota(jnp.int32, grp.shape, 0).astype(F32)
    gbest = jnp.min(jnp.where(grp == jnp.max(grp, axis=0, keepdims=True), gi, float(G)), axis=0, keepdims=True)
    pick = gi == gbest
    v = [jnp.sum(jnp.where(pick, s, 0.0), axis=0, keepdims=True) for s in sj]
    c = [jnp.sum(jnp.where(pick, s, 0.0), axis=0, keepdims=True) for s in cj]
    neg = jnp.float32(-jnp.inf)
    m1 = jnp.maximum(jnp.maximum(v[0], v[1]), jnp.maximum(v[2], v[3]))
    j1 = jnp.where(v[0] == m1, 0, jnp.where(v[1] == m1, 1, jnp.where(v[2] == m1, 2, 3)))
    w = [jnp.where(j1 == j, neg, v[j]) for j in range(4)]
    m2 = jnp.maximum(jnp.maximum(w[0], w[1]), jnp.maximum(w[2], w[3]))
    j2 = jnp.where(w[0] == m2, 0, jnp.where(w[1] == m2, 1, jnp.where(w[2] == m2, 2, 3)))
    g1 = jnp.where(j1 == 0, c[0], jnp.where(j1 == 1, c[1], jnp.where(j1 == 2, c[2], c[3])))
    g2 = jnp.where(j2 == 0, c[0], jnp.where(j2 == 1, c[1], jnp.where(j2 == 2, c[2], c[3])))
    tot = g1 + g2
    e0 = gbest.astype(jnp.int32) * EXPERTS_PER_GROUP
    return (e0 + j1, e0 + j2), (g1 / tot, g2 / tot)


OUT_SUBTILES = 2


def _out_ln_kernel(a_ref, b_ref, w_ref, x_ref, g_ref, beta_ref, x1_ref):
    sub = x_ref.shape[0] // OUT_SUBTILES
    for t in range(OUT_SUBTILES):
        rows = slice(t * sub, (t + 1) * sub)
        heads = [a_ref[h, rows, :] for h in range(a_ref.shape[0])] + [b_ref[h, rows, :] for h in range(b_ref.shape[0])]
        mix = jnp.dot(jnp.concatenate(heads, axis=1), w_ref[...], preferred_element_type=F32)
        x1_ref[rows, :] = _layer_norm(ALPHA * x_ref[rows, :] + mix, g_ref[...], beta_ref[...])


def out_ln(a, b, w, x, g, beta, *, bm=512):
    T, D = x.shape
    return pl.pallas_call(
        _out_ln_kernel,
        grid=(T // bm,),
        in_specs=[pl.BlockSpec((a.shape[0], bm, LANES), lambda i: (0, i, 0)),
                  pl.BlockSpec((b.shape[0], bm, LANES), lambda i: (0, i, 0)),
                  _resident(w.shape),
                  pl.BlockSpec((bm, D), lambda i: (i, 0)),
                  _resident(g.shape), _resident(beta.shape)],
        out_specs=pl.BlockSpec((bm, D), lambda i: (i, 0)),
        out_shape=jax.ShapeDtypeStruct((T, D), F32),
        compiler_params=_params(1),
        name="out_ln",
    )(a, b, w, x, g, beta)


def _route_kernel(x_ref, rwh_ref, rwl_ref, rb_ref, eid_ref, gate_ref):
    x = x_ref[...]
    hi = x.astype(BF16)
    lo = (x - hi.astype(F32)).astype(BF16)
    nt = (((1,), (1,)), ((), ()))
    logits_t = (lax.dot_general(rwh_ref[...], hi, nt, preferred_element_type=F32)
                + lax.dot_general(rwl_ref[...], hi, nt, preferred_element_type=F32)
                + lax.dot_general(rwh_ref[...], lo, nt, preferred_element_type=F32))
    eid, gate = _route(logits_t, rb_ref[...])
    for k in range(TOP_K):
        eid_ref[k:k + 1, :] = eid[k]
        gate_ref[k:k + 1, :] = gate[k]


def route(x1, rw_hi, rw_lo, rb, *, bm=1024):
    T, D = x1.shape
    return pl.pallas_call(
        _route_kernel,
        grid=(T // bm,),
        in_specs=[pl.BlockSpec((bm, D), lambda i: (i, 0)),
                  _resident(rw_hi.shape), _resident(rw_lo.shape), _resident(rb.shape)],
        out_specs=[pl.BlockSpec((TOP_K, bm), lambda i: (0, i)),
                   pl.BlockSpec((TOP_K, bm), lambda i: (0, i))],
        out_shape=[jax.ShapeDtypeStruct((TOP_K, T), jnp.int32),
                   jax.ShapeDtypeStruct((TOP_K, T), F32)],
        compiler_params=_params(1),
        name="route",
    )(x1, rw_hi, rw_lo, rb)


def _row_copy(src_hbm, row, dst_vmem, j, sem):
    return pltpu.make_async_copy(src_hbm.at[pl.ds(row, 1), :], dst_vmem.at[pl.ds(j, 1), :], sem)


def _moe_ffn_kernel(blk_e_ref, tok_ref, x_hbm, wg_ref, wu_ref, wd_ref, y_ref, xbuf, sem):
    i = pl.program_id(0)
    last = pl.num_programs(0) - 1
    slot = i % 2

    def gather(blk, s):
        base = blk * MOE_BLOCK
        for j in range(MOE_BLOCK):
            _row_copy(x_hbm, tok_ref[base + j], xbuf.at[s], j, sem.at[s]).start()

    def ffn():
        xb = xbuf[slot].astype(BF16)
        hg = jnp.dot(xb, wg_ref[...].astype(BF16), preferred_element_type=F32)
        hu = jnp.dot(xb, wu_ref[...].astype(BF16), preferred_element_type=F32)
        hb = (jax.nn.silu(hg) * hu).astype(BF16)
        y_ref[...] = jnp.dot(hb, wd_ref[...].astype(BF16), preferred_element_type=F32)

    @pl.when(i == 0)
    def _():
        gather(0, 0)

    pltpu.make_async_copy(x_hbm.at[pl.ds(0, MOE_BLOCK), :], xbuf.at[slot], sem.at[slot]).wait()

    @pl.when(i < last)
    def _():
        gather(i + 1, 1 - slot)
        ffn()

    @pl.when(i == last)
    def _():
        ffn()


def moe_ffn(x1, w_gate, w_up, w_down, blk_e, row_tok):
    T, D = x1.shape
    n_blocks = blk_e.shape[0]
    n_rows = n_blocks * MOE_BLOCK
    grid_spec = pltpu.PrefetchScalarGridSpec(
        num_scalar_prefetch=2,
        grid=(n_blocks,),
        in_specs=[pl.BlockSpec(memory_space=pl.ANY),
                  pl.BlockSpec((None, D, D_EXPERT), lambda i, be, tk: (be[i], 0, 0)),
                  pl.BlockSpec((None, D, D_EXPERT), lambda i, be, tk: (be[i], 0, 0)),
                  pl.BlockSpec((None, D_EXPERT, D), lambda i, be, tk: (be[i], 0, 0))],
        out_specs=pl.BlockSpec((MOE_BLOCK, D), lambda i, be, tk: (i, 0)),
        scratch_shapes=[pltpu.VMEM((2, MOE_BLOCK, D), F32), pltpu.SemaphoreType.DMA((2,))],
    )
    return pl.pallas_call(
        _moe_ffn_kernel,
        grid_spec=grid_spec,
        out_shape=jax.ShapeDtypeStruct((n_rows, D), F32),
        compiler_params=_params(1),
        name="moe_ffn",
    )(blk_e, row_tok, x1, w_gate, w_up, w_down)


def _combine_ln_kernel(dest_ref, y_hbm, x1_ref, gate_ref, g_ref, beta_ref, x2_ref, x2b_ref, ybuf, sem, *, bm, n_tok):
    i = pl.program_id(0)
    last = pl.num_programs(0) - 1
    slot = i % 2

    def gather(tile, s):
        base = tile * bm
        for j in range(bm):
            for k in range(TOP_K):
                _row_copy(y_hbm, dest_ref[k * n_tok + base + j], ybuf.at[s], k * bm + j, sem.at[s]).start()

    def finish():
        gate = gate_ref[...]
        moe = ybuf[slot, :bm] * gate[:, 0:1] + ybuf[slot, bm:] * gate[:, 1:2]
        x2 = _layer_norm(ALPHA * x1_ref[...] + moe, g_ref[...], beta_ref[...])
        x2_ref[...] = x2
        x2b_ref[...] = x2.astype(BF16)

    @pl.when(i == 0)
    def _():
        gather(0, 0)

    pltpu.make_async_copy(y_hbm.at[pl.ds(0, TOP_K * bm), :], ybuf.at[slot], sem.at[slot]).wait()

    @pl.when(i < last)
    def _():
        gather(i + 1, 1 - slot)
        finish()

    @pl.when(i == last)
    def _():
        finish()


def combine_ln(y_rows, dest, x1, gate_t, g, beta, *, bm=256):
    T, D = x1.shape
    grid_spec = pltpu.PrefetchScalarGridSpec(
        num_scalar_prefetch=1,
        grid=(T // bm,),
        in_specs=[pl.BlockSpec(memory_space=pl.ANY),
                  pl.BlockSpec((bm, D), lambda i, d: (i, 0)),
                  pl.BlockSpec((bm, TOP_K), lambda i, d: (i, 0)),
                  pl.BlockSpec((1, D), lambda i, d: (0, 0)),
                  pl.BlockSpec((1, D), lambda i, d: (0, 0))],
        out_specs=[pl.BlockSpec((bm, D), lambda i, d: (i, 0)),
                   pl.BlockSpec((bm, D), lambda i, d: (i, 0))],
        scratch_shapes=[pltpu.VMEM((2, TOP_K * bm, D), F32), pltpu.SemaphoreType.DMA((2,))],
    )
    return pl.pallas_call(
        functools.partial(_combine_ln_kernel, bm=bm, n_tok=T),
        grid_spec=grid_spec,
        out_shape=[jax.ShapeDtypeStruct((T, D), F32), jax.ShapeDtypeStruct((T, D), BF16)],
        compiler_params=_params(1),
        name="combine_ln",
    )(dest, y_rows, x1, gate_t, g, beta)


def moe_plan(eid):
    T = eid.shape[1]
    n_assign = TOP_K * T
    n_blocks = (n_assign + N_EXPERTS * (MOE_BLOCK - 1) + MOE_BLOCK - 1) // MOE_BLOCK
    flat_e = eid.reshape(n_assign)
    onehot = (flat_e[:, None] == jnp.arange(N_EXPERTS, dtype=jnp.int32)[None, :]).astype(jnp.int32)
    csum = jnp.cumsum(onehot, axis=0)
    rank = jnp.sum(csum * onehot, axis=1) - 1
    counts = csum[-1]
    pcounts = (counts + MOE_BLOCK - 1) // MOE_BLOCK * MOE_BLOCK
    pends = jnp.cumsum(pcounts)
    pstarts = pends - pcounts
    dest = pstarts[flat_e] + rank
    tok = jnp.tile(jnp.arange(T, dtype=jnp.int32), TOP_K)
    row_tok = jnp.zeros((n_blocks * MOE_BLOCK,), jnp.int32).at[dest].set(tok)
    blk_start = jnp.arange(n_blocks, dtype=jnp.int32) * MOE_BLOCK
    blk_e = jnp.minimum(jnp.sum((pends[None, :] <= blk_start[:, None]).astype(jnp.int32), axis=1), N_EXPERTS - 1)
    return dest.astype(jnp.int32), row_tok, blk_e.astype(jnp.int32)


def moe_layer(x1, eid, gate, w_gate, w_up, w_down, ln_g, ln_b):
    dest, row_tok, blk_e = moe_plan(eid)
    y_rows = moe_ffn(x1, w_gate, w_up, w_down, blk_e, row_tok)
    return combine_ln(y_rows, dest, x1, gate.T, ln_g.reshape(1, -1), ln_b.reshape(1, -1))


def _rope_cos_sin(pos, dim):
    inv_freq = ROPE_THETA ** (-jnp.arange(0, dim, 2, dtype=F32) / dim)
    ang = pos.astype(F32)[:, None] * inv_freq[None, :]
    cos = jnp.cos(ang)
    sin = jnp.sin(ang)
    return jnp.concatenate([cos, cos], axis=-1), jnp.concatenate([-sin, sin], axis=-1)


def axial_rope_tables(seq):
    pos = jnp.arange(seq)
    half = HEAD_DIM // 2
    cr, sr = _rope_cos_sin(pos // GRID_W, half)
    cc, sc = _rope_cos_sin(pos % GRID_W, half)
    return jnp.concatenate([cr, cc], axis=-1), jnp.concatenate([sr, sc], axis=-1)


def mla_rope_tables(seq):
    c, s = _rope_cos_sin(jnp.arange(seq), D_ROPE)
    z = jnp.zeros((seq, LANES - D_ROPE), F32)
    return jnp.concatenate([c, z], axis=-1), jnp.concatenate([s, z], axis=-1)


def router_layout(router_w, router_b):
    perm = np.array([g * EXPERTS_PER_GROUP + j for j in range(EXPERTS_PER_GROUP) for g in range(N_GROUPS)])
    rw = router_w.astype(F32).T[perm]
    hi = rw.astype(BF16)
    lo = (rw - hi.astype(F32)).astype(BF16)
    return hi, lo, router_b.astype(F32)[perm].reshape(N_EXPERTS, 1)


def mla_up_layout(w_q_up, w_kv_up, w_in_tail):
    rq = w_q_up.shape[0]
    wq = w_q_up.reshape(rq, D_HEADS, D_NOPE + D_ROPE)
    wq = jnp.pad(wq, ((0, 0), (0, 0), (0, 2 * LANES - (D_NOPE + D_ROPE)))).reshape(rq, D_HEADS * 2 * LANES)
    wc = jnp.pad(w_in_tail, ((0, 0), (0, LANES - D_ROPE)))
    return wq.astype(BF16), w_kv_up.astype(BF16), wc.astype(BF16)


def kernel(x, router_w, router_b, l0_w_in, l0_a_q_norm, l0_a_k_norm, l0_w_out, l0_ln1_g, l0_ln1_b, l0_w_gate, l0_w_up, l0_w_down, l0_ln2_g, l0_ln2_b, l1_w_in, l1_c_rpb, l1_d_q_norm, l1_d_w_q_up, l1_d_kv_norm, l1_d_w_kv_up, l1_w_out, l1_ln1_g, l1_ln1_b, l1_w_gate, l1_w_up, l1_w_down, l1_ln2_g, l1_ln2_b):
    batch, seq, d = x.shape
    T = batch * seq
    xf = x.reshape(T, d)
    rw_hi, rw_lo, rb = router_layout(router_w, router_b)
    row = lambda v: v.astype(F32).reshape(1, -1)

    xb = xf.astype(BF16)
    a_q_dim = A_HEADS * HEAD_DIM
    a_qk_dim = a_q_dim + A_KV_HEADS * HEAD_DIM
    w0 = l0_w_in.astype(BF16)
    gain = jnp.concatenate([jnp.tile(l0_a_q_norm.astype(F32), A_HEADS) * (HEAD_DIM ** -0.5),
                            jnp.tile(l0_a_k_norm.astype(F32), A_KV_HEADS)]).reshape(1, -1)
    cos_a, sin_a = axial_rope_tables(seq)
    qk_a = proj_norm_rope(xb, w0[:, :a_qk_dim], gain, cos_a, sin_a, seq=seq)
    nb = A_KV_HEADS
    q_scale = jnp.full((B_HEADS * HEAD_DIM,), HEAD_DIM ** -0.5, F32)
    scale0 = jnp.concatenate([jnp.ones((nb * HEAD_DIM,), F32), q_scale, jnp.ones((2 * B_HEADS * HEAD_DIM,), F32)])
    rest0 = proj_heads(xb, w0[:, a_qk_dim:], scale0.reshape(1, -1), heads_per_tile=13)
    group = A_HEADS // A_KV_HEADS
    oa = dense_attention(qk_a, qk_a, rest0, n_heads=A_HEADS, heads=group, k0=A_HEADS, v0=0, batch=batch, seq=seq,
                         group=group)
    ob = dilated_attention(rest0, dilated_bias_table(B_HEADS), n_heads=B_HEADS, heads=1, q0=nb, k0=nb + B_HEADS,
                           v0=nb + 2 * B_HEADS, batch=batch, seq=seq)
    x1 = out_ln(oa, ob, l0_w_out.astype(BF16), xf, row(l0_ln1_g), row(l0_ln1_b))
    eid, gate = route(x1, rw_hi, rw_lo, rb)
    x2, x2b = moe_layer(x1, eid, gate, l0_w_gate, l0_w_up, l0_w_down, l0_ln2_g, l0_ln2_b)

    c_dim = C_HEADS * HEAD_DIM
    w1 = l1_w_in
    scale1 = jnp.concatenate([jnp.full((c_dim,), HEAD_DIM ** -0.5, F32), jnp.ones((2 * c_dim,), F32)])
    qkv_c = proj_heads(x2b, w1[:, :3 * c_dim].astype(BF16), scale1.reshape(1, -1), heads_per_tile=12)
    oc = na_attention(qkv_c, na_bias_table(l1_c_rpb, seq // GRID_W), n_heads=C_HEADS, heads=2, q0=0, k0=C_HEADS,
                      v0=2 * C_HEADS, batch=batch, seq=seq)
    wq, wkv, wc = mla_up_layout(l1_d_w_q_up, l1_d_w_kv_up, w1[:, 3 * c_dim:])
    cos_d, sin_d = mla_rope_tables(seq)
    qd, kd, vd = mla_proj(x2b, wc, row(l1_d_q_norm), row(l1_d_kv_norm), wq, wkv, cos_d, sin_d, seq=seq,
                          scale=(D_NOPE + D_ROPE) ** -0.5)
    od = dense_attention(qd, kd, vd, n_heads=D_HEADS, heads=2, k0=0, v0=0, batch=batch, seq=seq, group=1)
    x3 = out_ln(oc, od, l1_w_out.astype(BF16), x2, row(l1_ln1_g), row(l1_ln1_b))
    eid, gate = route(x3, rw_hi, rw_lo, rb)
    x4, _ = moe_layer(x3, eid, gate, l1_w_gate, l1_w_up, l1_w_down, l1_ln2_g, l1_ln2_b)
    return x4.reshape(batch, seq, d)
```

```python
import functools
import math

import jax
import jax.numpy as jnp
import numpy as np
from jax import lax
from jax.experimental import pallas as pl
from jax.experimental.pallas import tpu as pltpu

D_MODEL = 2048
DEPTH = 2
GRID_W = 64
HEAD_DIM = 128
ROPE_THETA = 10000.0
RMS_EPS = 1e-6
LN_EPS = 1e-5

A_HEADS = 8
A_KV_HEADS = 2
B_HEADS = 8
B_BRANCHES = ((128, 1), (512, 4), (2048, 16))
C_HEADS = 8
NA_ROWS = 8
NA_COLS = 16
D_HEADS = 8
D_Q_RANK = 512
D_KV_RANK = 256
D_NOPE = 128
D_ROPE = 64
D_V = 128

N_EXPERTS = 32
N_GROUPS = 8
EXPERTS_PER_GROUP = N_EXPERTS // N_GROUPS
TOP_K = 2
D_EXPERT = 512
MOE_BLOCK = 256

ALPHA = (2.0 * DEPTH) ** 0.25
LOG2E = math.log2(math.e)

LANES = 128
VMEM_LIMIT = 56 * 1024 * 1024

BF16 = jnp.bfloat16
F32 = jnp.float32


def _params(n_axes, **flags):
    return pltpu.CompilerParams(dimension_semantics=("arbitrary",) * n_axes, vmem_limit_bytes=VMEM_LIMIT,
                                flags=flags or None)


def _rot_half32(y):
    lane = lax.broadcasted_iota(jnp.int32, y.shape, y.ndim - 1)
    up = pltpu.roll(y, 96, axis=y.ndim - 1)
    dn = pltpu.roll(y, 32, axis=y.ndim - 1)
    return jnp.where((lane % 64) < 32, up, dn)


def _proj_heads_kernel(x_ref, w_ref, c_ref, o_ref, *, heads_per_tile):
    acc = jnp.dot(x_ref[...].astype(BF16), w_ref[...], preferred_element_type=F32) * c_ref[...]
    for h in range(heads_per_tile):
        o_ref[h] = acc[:, h * LANES:(h + 1) * LANES].astype(o_ref.dtype)


def proj_heads(x, w, col_scale, *, heads_per_tile, bm=512):
    T, D = x.shape
    H = w.shape[1] // LANES
    bn = heads_per_tile * LANES
    grid = (H // heads_per_tile, T // bm)
    return pl.pallas_call(
        functools.partial(_proj_heads_kernel, heads_per_tile=heads_per_tile),
        grid=grid,
        in_specs=[pl.BlockSpec((bm, D), lambda j, i: (i, 0)),
                  pl.BlockSpec((D, bn), lambda j, i: (0, j)),
                  pl.BlockSpec((1, bn), lambda j, i: (0, j))],
        out_specs=pl.BlockSpec((heads_per_tile, bm, LANES), lambda j, i: (j, i, 0)),
        out_shape=jax.ShapeDtypeStruct((H, T, LANES), BF16),
        compiler_params=_params(2),
        name="proj_heads",
    )(x, w, col_scale)


def _resident(shape):
    return pl.BlockSpec(shape, lambda *_: (0,) * len(shape), pipeline_mode=pl.Buffered(1))


def _proj_norm_rope_kernel(x_ref, w_ref, g_ref, cos_ref, sin_ref, o_ref):
    acc = jnp.dot(x_ref[...].astype(BF16), w_ref[...], preferred_element_type=F32)
    cos = cos_ref[...]
    sin = sin_ref[...]
    for h in range(o_ref.shape[0]):
        t = acc[:, h * LANES:(h + 1) * LANES]
        ms = jnp.mean(t * t, axis=-1, keepdims=True)
        y = t * lax.rsqrt(ms + RMS_EPS) * g_ref[:, h * LANES:(h + 1) * LANES]
        o_ref[h] = (y * cos + _rot_half32(y) * sin).astype(o_ref.dtype)


def proj_norm_rope(x, w, gain, cos, sin, *, seq, bm=512):
    T, D = x.shape
    H = w.shape[1] // LANES
    nsb = seq // bm
    return pl.pallas_call(
        _proj_norm_rope_kernel,
        grid=(T // bm,),
        in_specs=[pl.BlockSpec((bm, D), lambda i: (i, 0)),
                  _resident(w.shape), _resident(gain.shape),
                  pl.BlockSpec((bm, LANES), lambda i: (i % nsb, 0)),
                  pl.BlockSpec((bm, LANES), lambda i: (i % nsb, 0))],
        out_specs=pl.BlockSpec((H, bm, LANES), lambda i: (0, i, 0)),
        out_shape=jax.ShapeDtypeStruct((H, T, LANES), BF16),
        compiler_params=_params(1),
        name="proj_norm_rope",
    )(x, w, gain, cos, sin)


def _mla_proj_kernel(x_ref, wc_ref, gq_ref, gkv_ref, wq_ref, wkv_ref, cos_ref, sin_ref,
                     q_ref, k_ref, v_ref, *, scale):
    c = jnp.dot(x_ref[...], wc_ref[...], preferred_element_type=F32)
    cos = cos_ref[...]
    sin = sin_ref[...]

    def rms(t, g):
        ms = jnp.mean(t * t, axis=-1, keepdims=True)
        return t * lax.rsqrt(ms + RMS_EPS) * g

    cq = rms(c[:, :D_Q_RANK], gq_ref[...]).astype(BF16)
    ckv = rms(c[:, D_Q_RANK:D_Q_RANK + D_KV_RANK], gkv_ref[...]).astype(BF16)
    kr = c[:, D_Q_RANK + D_KV_RANK:]
    kr = (kr * cos + _rot_half32(kr) * sin).astype(k_ref.dtype)
    q = jnp.dot(cq, wq_ref[...], preferred_element_type=F32) * scale
    kv = jnp.dot(ckv, wkv_ref[...], preferred_element_type=F32)
    for h in range(D_HEADS):
        base = h * 2 * LANES
        q_ref[h, :, :LANES] = q[:, base:base + LANES].astype(q_ref.dtype)
        qr = q[:, base + LANES:base + 2 * LANES]
        q_ref[h, :, LANES:] = (qr * cos + _rot_half32(qr) * sin).astype(q_ref.dtype)
        k_ref[h, :, :LANES] = kv[:, base:base + LANES].astype(k_ref.dtype)
        k_ref[h, :, LANES:] = kr
        v_ref[h] = kv[:, base + LANES:base + 2 * LANES].astype(v_ref.dtype)


def mla_proj(x, wc, gq, gkv, wq, wkv, cos, sin, *, seq, scale, bm=512):
    T, D = x.shape
    nsb = seq // bm
    return pl.pallas_call(
        functools.partial(_mla_proj_kernel, scale=scale),
        grid=(T // bm,),
        in_specs=[pl.BlockSpec((bm, D), lambda i: (i, 0)),
                  _resident(wc.shape), _resident(gq.shape), _resident(gkv.shape), _resident(wq.shape),
                  _resident(wkv.shape),
                  pl.BlockSpec((bm, LANES), lambda i: (i % nsb, 0)),
                  pl.BlockSpec((bm, LANES), lambda i: (i % nsb, 0))],
        out_specs=[pl.BlockSpec((D_HEADS, bm, 2 * LANES), lambda i: (0, i, 0)),
                   pl.BlockSpec((D_HEADS, bm, 2 * LANES), lambda i: (0, i, 0)),
                   pl.BlockSpec((D_HEADS, bm, LANES), lambda i: (0, i, 0))],
        out_shape=[jax.ShapeDtypeStruct((D_HEADS, T, 2 * LANES), BF16),
                   jax.ShapeDtypeStruct((D_HEADS, T, 2 * LANES), BF16),
                   jax.ShapeDtypeStruct((D_HEADS, T, LANES), BF16)],
        compiler_params=_params(1),
        name="mla_proj",
    )(x, wc, gq, gkv, wq, wkv, cos, sin)


DENSE_TQ = 512
ATTN_TQ = 256


def _attn_pipeline(q_ref, k_ref, v_ref, o_ref, vext_ref, s_buf, m_buf, p_buf, *, heads, group, win, win_start, bias):
    tq = s_buf[0].shape[0]
    S = q_ref.shape[1]
    nblk = S // tq
    total = heads * nblk
    vext_ref[:, :, :LANES] = v_ref[...]
    vext_ref[:, :, LANES:] = jnp.ones(v_ref.shape, vext_ref.dtype)

    def locate(n):
        h, blk = (0, n) if heads == 1 else (n // nblk, n % nblk)
        rows = pl.ds(blk * tq, tq) if isinstance(blk, int) else pl.ds(pl.multiple_of(blk * tq, tq), tq)
        keys = slice(None) if win == S else pl.ds(win_start(blk), win)
        return h, blk, rows, keys

    def scores(n, par):
        h, blk, rows, keys = locate(n)
        s = lax.dot_general(q_ref[h, rows, :], k_ref[h // group, keys, :], (((1,), (1,)), ((), ())),
                            preferred_element_type=F32)
        if bias is not None:
            s = s + bias(h, blk)
        s_buf[par][...] = s
        m_buf[par][...] = jnp.max(s, axis=-1, keepdims=True)

    def exps(par):
        p_buf[par][...] = jnp.exp2(s_buf[par][...] - m_buf[par][...]).astype(BF16)

    def wsum(n, par):
        h, blk, rows, keys = locate(n)
        o_ext = jnp.dot(p_buf[par][...], vext_ref[h // group, keys, :], preferred_element_type=F32)
        o_ref[h, rows, :] = (o_ext[:, :LANES] / o_ext[:, LANES:]).astype(o_ref.dtype)

    scores(0, 0)
    exps(0)
    scores(1, 1)

    def step(j, carry):
        n = 2 * j + 1
        exps(1)
        wsum(n - 1, 0)
        scores(n + 1, 0)
        exps(0)
        wsum(n, 1)
        scores(n + 2, 1)
        return carry

    lax.fori_loop(0, total // 2 - 1, step, 0)
    exps(1)
    wsum(total - 2, 0)
    wsum(total - 1, 1)


def _attn_scratch(kv_heads, seq, win, tq):
    return [pltpu.VMEM((kv_heads, seq, 2 * LANES), BF16),
            pltpu.VMEM((tq, win), F32), pltpu.VMEM((tq, win), F32),
            pltpu.VMEM((tq, 1), F32), pltpu.VMEM((tq, 1), F32),
            pltpu.VMEM((tq, win), BF16), pltpu.VMEM((tq, win), BF16)]


def _dense_attn_kernel(q_ref, k_ref, v_ref, o_ref, vext_ref, s0, s1, m0, m1, p0, p1, *, heads, group):
    _attn_pipeline(q_ref, k_ref, v_ref, o_ref, vext_ref, (s0, s1), (m0, m1), (p0, p1), heads=heads, group=group,
                   win=q_ref.shape[1], win_start=None, bias=None)


def dense_attention(q, k, v, *, n_heads, heads, k0, v0, batch, seq, group):
    T, dq = q.shape[1:]
    kvh = heads // group
    assert n_heads % heads == 0 and heads % group == 0 and k0 % kvh == 0 and v0 % kvh == 0
    return pl.pallas_call(
        functools.partial(_dense_attn_kernel, heads=heads, group=group),
        grid=(batch, n_heads // heads),
        in_specs=[pl.BlockSpec((heads, seq, dq), lambda b, g: (g, b, 0)),
                  pl.BlockSpec((kvh, seq, dq), lambda b, g: (k0 // kvh + g, b, 0)),
                  pl.BlockSpec((kvh, seq, LANES), lambda b, g: (v0 // kvh + g, b, 0))],
        out_specs=pl.BlockSpec((heads, seq, LANES), lambda b, g: (g, b, 0)),
        out_shape=jax.ShapeDtypeStruct((n_heads, T, LANES), BF16),
        scratch_shapes=_attn_scratch(kvh, seq, seq, DENSE_TQ),
        compiler_params=_params(2),
        name="dense_attention",
    )(q, k, v)


DIL_REACH = max(w // 2 for w, _ in B_BRANCHES)
DIL_WIN = 2 * DIL_REACH + ATTN_TQ
DIL_TABLE_TILES = (DIL_WIN + 2 * DIL_REACH) // LANES


def _dilated_attn_kernel(q_ref, k_ref, v_ref, bias_ref, o_ref, vext_ref, s0, s1, m0, m1, p0, p1, *, heads):
    S = q_ref.shape[1]

    def win_start(blk):
        return pl.multiple_of(jnp.clip(blk * ATTN_TQ - DIL_REACH, 0, S - DIL_WIN), ATTN_TQ)

    def bias(h, blk):
        tile0 = (win_start(blk) - blk * ATTN_TQ + 2 * DIL_REACH) // LANES
        return jnp.concatenate([bias_ref[h, tile0 + t] for t in range(DIL_WIN // LANES)], axis=1)

    _attn_pipeline(q_ref, k_ref, v_ref, o_ref, vext_ref, (s0, s1), (m0, m1), (p0, p1), heads=heads, group=1,
                   win=DIL_WIN, win_start=win_start, bias=bias)


def dilated_bias_table(n_heads):
    col = (np.arange(DIL_TABLE_TILES)[:, None, None] * LANES + np.arange(LANES)[None, None, :])
    d = np.abs(col - np.arange(ATTN_TQ)[None, :, None] - 2 * DIL_REACH)
    mult = np.zeros(d.shape, np.float64)
    for window, dil in B_BRANCHES:
        mult += ((d % dil == 0) & (d <= window // 2)).astype(np.float64)
    with np.errstate(divide="ignore"):
        logm = np.log(mult)
    slopes = jnp.asarray(2.0 ** (-8.0 * np.arange(1, n_heads + 1) / n_heads), dtype=F32)
    bias = jnp.asarray(logm, dtype=F32)[None] - slopes[:, None, None, None] * jnp.asarray(d, dtype=F32)[None]
    return bias * LOG2E


def dilated_attention(qkv, bias, *, n_heads, heads, q0, k0, v0, batch, seq):
    T = qkv.shape[1]
    assert n_heads % heads == 0 and q0 % heads == 0 and k0 % heads == 0 and v0 % heads == 0
    qkv_spec = lambda h0: pl.BlockSpec((heads, seq, LANES), lambda b, g: (h0 // heads + g, b, 0))
    return pl.pallas_call(
        functools.partial(_dilated_attn_kernel, heads=heads),
        grid=(batch, n_heads // heads),
        in_specs=[qkv_spec(q0), qkv_spec(k0), qkv_spec(v0),
                  pl.BlockSpec((heads, DIL_TABLE_TILES, ATTN_TQ, LANES), lambda b, g: (g, 0, 0, 0))],
        out_specs=pl.BlockSpec((heads, seq, LANES), lambda b, g: (g, b, 0)),
        out_shape=jax.ShapeDtypeStruct((n_heads, T, LANES), BF16),
        scratch_shapes=_attn_scratch(heads, seq, DIL_WIN, ATTN_TQ),
        compiler_params=_params(2),
        name="dilated_attention",
    )(qkv, qkv, qkv, bias)


NA_QROWS = ATTN_TQ // GRID_W
NA_KROWS = NA_QROWS + NA_ROWS
NA_WIN = NA_KROWS * GRID_W


def _na_window_row(blk, rows):
    lo = blk * NA_QROWS - NA_ROWS // 2
    return jnp.clip(lo, 0, rows - NA_KROWS) if not isinstance(blk, int) else min(max(lo, 0), rows - NA_KROWS)


def _na_attn_kernel(q_ref, k_ref, v_ref, bias_ref, o_ref, vext_ref, s0, s1, m0, m1, p0, p1, *, heads, rows):
    nblk = rows // NA_QROWS

    def win_start(blk):
        return pl.multiple_of(_na_window_row(blk, rows) * GRID_W, GRID_W)

    def bias(h, blk):
        cls = (blk > 0).astype(jnp.int32) + (blk == nblk - 1).astype(jnp.int32) if not isinstance(blk, int) \
            else int(blk > 0) + int(blk == nblk - 1)
        return bias_ref[h, cls]

    _attn_pipeline(q_ref, k_ref, v_ref, o_ref, vext_ref, (s0, s1), (m0, m1), (p0, p1), heads=heads, group=1,
                   win=NA_WIN, win_start=win_start, bias=bias)


def na_bias_table(rpb, rows):
    H = rpb.shape[0]
    col = np.arange(GRID_W)
    c0 = np.clip(col - NA_COLS // 2, 0, GRID_W - NA_COLS)
    col_ok = (col[None, :] >= c0[:, None]) & (col[None, :] < c0[:, None] + NA_COLS)
    dcol = np.clip(col[None, :] - col[:, None] + NA_COLS - 1, 0, 2 * NA_COLS - 2)
    pick = ((dcol[:, :, None] == np.arange(2 * NA_COLS - 1)) & col_ok[:, :, None]).astype(np.float32)
    t = jnp.einsum("hrd,qkd->hrqk", rpb.astype(F32), jnp.asarray(pick), precision=lax.Precision.HIGHEST)
    t = jnp.where(jnp.asarray(col_ok)[None, None], t * LOG2E, -jnp.inf)
    nblk = rows // NA_QROWS

    def geometry(blk):
        r = blk * NA_QROWS + np.arange(NA_QROWS)
        kr = _na_window_row(blk, rows) + np.arange(NA_KROWS)
        r0 = np.clip(r - NA_ROWS // 2, 0, rows - NA_ROWS)
        valid = (kr[None, :] >= r0[:, None]) & (kr[None, :] < r0[:, None] + NA_ROWS)
        drow = np.clip(kr[None, :] - r[:, None] + NA_ROWS - 1, 0, 2 * NA_ROWS - 2)
        return valid, np.where(valid, drow, 0)

    inner = geometry(1)
    assert all(np.array_equal(a, b) for blk in range(1, nblk - 1) for a, b in zip(geometry(blk), inner))
    tables = []
    for blk in (0, 1, nblk - 1):
        valid, drow = geometry(blk)
        slab = jnp.where(jnp.asarray(valid)[None, :, :, None, None], t[:, drow], -jnp.inf)
        tables.append(slab.transpose(0, 1, 3, 2, 4).reshape(H, ATTN_TQ, NA_WIN))
    return jnp.stack(tables, axis=1)


def na_attention(qkv, bias, *, n_heads, heads, q0, k0, v0, batch, seq):
    T = qkv.shape[1]
    rows = seq // GRID_W
    assert n_heads % heads == 0 and q0 % heads == 0 and k0 % heads == 0 and v0 % heads == 0
    assert rows % NA_QROWS == 0 and rows // NA_QROWS >= 3
    qkv_spec = lambda h0: pl.BlockSpec((heads, seq, LANES), lambda b, g: (h0 // heads + g, b, 0))
    return pl.pallas_call(
        functools.partial(_na_attn_kernel, heads=heads, rows=rows),
        grid=(batch, n_heads // heads),
        in_specs=[qkv_spec(q0), qkv_spec(k0), qkv_spec(v0),
                  pl.BlockSpec((heads, 3, ATTN_TQ, NA_WIN), lambda b, g: (g, 0, 0, 0))],
        out_specs=pl.BlockSpec((heads, seq, LANES), lambda b, g: (g, b, 0)),
        out_shape=jax.ShapeDtypeStruct((n_heads, T, LANES), BF16),
        scratch_shapes=_attn_scratch(heads, seq, NA_WIN, ATTN_TQ),
        compiler_params=_params(2),
        name="na_attention",
    )(qkv, qkv, qkv, bias)


def _layer_norm(z, g, b):
    mu = jnp.mean(z, axis=-1, keepdims=True)
    zc = z - mu
    var = jnp.mean(zc * zc, axis=-1, keepdims=True)
    return zc * lax.rsqrt(var + LN_EPS) * g + b


def _top2_of4(v0, v1, v2, v3):
    a = jnp.maximum(v0, v1)
    b = jnp.minimum(v0, v1)
    c = jnp.maximum(v2, v3)
    d = jnp.minimum(v2, v3)
    return jnp.maximum(a, c), jnp.maximum(jnp.minimum(a, c), jnp.maximum(b, d))


def _route(logits_t, rb):
    G = N_GROUPS
    scores = jax.nn.sigmoid(logits_t)
    sel = scores + rb
    sj = [sel[j * G:(j + 1) * G] for j in range(EXPERTS_PER_GROUP)]
    cj = [scores[j * G:(j + 1) * G] for j in range(EXPERTS_PER_GROUP)]
    t1, t2 = _top2_of4(*sj)
    grp = t1 + t2
    gi = lax.broadcasted_iota(jnp.int32, grp.shape, 0).astype(F32)
    gbest = jnp.min(jnp.where(grp == jnp.max(grp, axis=0, keepdims=True), gi, float(G)), axis=0, keepdims=True)
    pick = gi == gbest
    v = [jnp.sum(jnp.where(pick, s, 0.0), axis=0, keepdims=True) for s in sj]
    c = [jnp.sum(jnp.where(pick, s, 0.0), axis=0, keepdims=True) for s in cj]
    neg = jnp.float32(-jnp.inf)
    m1 = jnp.maximum(jnp.maximum(v[0], v[1]), jnp.maximum(v[2], v[3]))
    j1 = jnp.where(v[0] == m1, 0, jnp.where(v[1] == m1, 1, jnp.where(v[2] == m1, 2, 3)))
    w = [jnp.where(j1 == j, neg, v[j]) for j in range(4)]
    m2 = jnp.maximum(jnp.maximum(w[0], w[1]), jnp.maximum(w[2], w[3]))
    j2 = jnp.where(w[0] == m2, 0, jnp.where(w[1] == m2, 1, jnp.where(w[2] == m2, 2, 3)))
    g1 = jnp.where(j1 == 0, c[0], jnp.where(j1 == 1, c[1], jnp.where(j1 == 2, c[2], c[3])))
    g2 = jnp.where(j2 == 0, c[0], jnp.where(j2 == 1, c[1], jnp.where(j2 == 2, c[2], c[3])))
    tot = g1 + g2
    e0 = gbest.astype(jnp.int32) * EXPERTS_PER_GROUP
    return (e0 + j1, e0 + j2), (g1 / tot, g2 / tot)


OUT_SUBTILES = 2


def _out_ln_kernel(a_ref, b_ref, w_ref, x_ref, g_ref, beta_ref, x1_ref):
    sub = x_ref.shape[0] // OUT_SUBTILES
    for t in range(OUT_SUBTILES):
        rows = slice(t * sub, (t + 1) * sub)
        heads = [a_ref[h, rows, :] for h in range(a_ref.shape[0])] + [b_ref[h, rows, :] for h in range(b_ref.shape[0])]
        mix = jnp.dot(jnp.concatenate(heads, axis=1), w_ref[...], preferred_element_type=F32)
        x1_ref[rows, :] = _layer_norm(ALPHA * x_ref[rows, :] + mix, g_ref[...], beta_ref[...])


def out_ln(a, b, w, x, g, beta, *, bm=512):
    T, D = x.shape
    return pl.pallas_call(
        _out_ln_kernel,
        grid=(T // bm,),
        in_specs=[pl.BlockSpec((a.shape[0], bm, LANES), lambda i: (0, i, 0)),
                  pl.BlockSpec((b.shape[0], bm, LANES), lambda i: (0, i, 0)),
                  _resident(w.shape),
                  pl.BlockSpec((bm, D), lambda i: (i, 0)),
                  _resident(g.shape), _resident(beta.shape)],
        out_specs=pl.BlockSpec((bm, D), lambda i: (i, 0)),
        out_shape=jax.ShapeDtypeStruct((T, D), F32),
        compiler_params=_params(1),
        name="out_ln",
    )(a, b, w, x, g, beta)


def _route_kernel(x_ref, rwh_ref, rwl_ref, rb_ref, eid_ref, gate_ref):
    x = x_ref[...]
    hi = x.astype(BF16)
    lo = (x - hi.astype(F32)).astype(BF16)
    nt = (((1,), (1,)), ((), ()))
    logits_t = (lax.dot_general(rwh_ref[...], hi, nt, preferred_element_type=F32)
                + lax.dot_general(rwl_ref[...], hi, nt, preferred_element_type=F32)
                + lax.dot_general(rwh_ref[...], lo, nt, preferred_element_type=F32))
    eid, gate = _route(logits_t, rb_ref[...])
    for k in range(TOP_K):
        eid_ref[k:k + 1, :] = eid[k]
        gate_ref[k:k + 1, :] = gate[k]


def route(x1, rw_hi, rw_lo, rb, *, bm=1024):
    T, D = x1.shape
    return pl.pallas_call(
        _route_kernel,
        grid=(T // bm,),
        in_specs=[pl.BlockSpec((bm, D), lambda i: (i, 0)),
                  _resident(rw_hi.shape), _resident(rw_lo.shape), _resident(rb.shape)],
        out_specs=[pl.BlockSpec((TOP_K, bm), lambda i: (0, i)),
                   pl.BlockSpec((TOP_K, bm), lambda i: (0, i))],
        out_shape=[jax.ShapeDtypeStruct((TOP_K, T), jnp.int32),
                   jax.ShapeDtypeStruct((TOP_K, T), F32)],
        compiler_params=_params(1),
        name="route",
    )(x1, rw_hi, rw_lo, rb)


GATHER_DEPTH = 3


def _row_copy(src_hbm, row, dst_vmem, j, sem):
    return pltpu.make_async_copy(src_hbm.at[pl.ds(row, 1), :], dst_vmem.at[pl.ds(j, 1), :], sem)


def _gather_ring_step(start_gather, wait_gather, compute):
    i = pl.program_id(0)
    n_steps = pl.num_programs(0)
    ahead = GATHER_DEPTH - 1

    @pl.when(i == 0)
    def _():
        for a in range(ahead):
            start_gather(a, a)

    wait_gather(i % GATHER_DEPTH)

    @pl.when(i + ahead < n_steps)
    def _():
        start_gather(i + ahead, (i + ahead) % GATHER_DEPTH)
        compute(i % GATHER_DEPTH)

    @pl.when(i + ahead >= n_steps)
    def _():
        compute(i % GATHER_DEPTH)


def _moe_ffn_kernel(blk_e_ref, tok_ref, x_hbm, wg_ref, wu_ref, wd_ref, y_ref, xbuf, sem):
    def start_gather(blk, s):
        base = blk * MOE_BLOCK
        for j in range(MOE_BLOCK):
            _row_copy(x_hbm, tok_ref[base + j], xbuf.at[s], j, sem.at[s]).start()

    def wait_gather(s):
        pltpu.make_async_copy(x_hbm.at[pl.ds(0, MOE_BLOCK), :], xbuf.at[s], sem.at[s]).wait()

    def ffn(s):
        xb = xbuf[s].astype(BF16)
        hg = jnp.dot(xb, wg_ref[...].astype(BF16), preferred_element_type=F32)
        hu = jnp.dot(xb, wu_ref[...].astype(BF16), preferred_element_type=F32)
        hb = (jax.nn.silu(hg) * hu).astype(BF16)
        y_ref[...] = jnp.dot(hb, wd_ref[...].astype(BF16), preferred_element_type=F32)

    _gather_ring_step(start_gather, wait_gather, ffn)


def moe_ffn(x1, w_gate, w_up, w_down, blk_e, row_tok):
    T, D = x1.shape
    n_blocks = blk_e.shape[0]
    assert n_blocks >= GATHER_DEPTH
    n_rows = n_blocks * MOE_BLOCK
    grid_spec = pltpu.PrefetchScalarGridSpec(
        num_scalar_prefetch=2,
        grid=(n_blocks,),
        in_specs=[pl.BlockSpec(memory_space=pl.ANY),
                  pl.BlockSpec((None, D, D_EXPERT), lambda i, be, tk: (be[i], 0, 0)),
                  pl.BlockSpec((None, D, D_EXPERT), lambda i, be, tk: (be[i], 0, 0)),
                  pl.BlockSpec((None, D_EXPERT, D), lambda i, be, tk: (be[i], 0, 0))],
        out_specs=pl.BlockSpec((MOE_BLOCK, D), lambda i, be, tk: (i, 0)),
        scratch_shapes=[pltpu.VMEM((GATHER_DEPTH, MOE_BLOCK, D), F32), pltpu.SemaphoreType.DMA((GATHER_DEPTH,))],
    )
    return pl.pallas_call(
        _moe_ffn_kernel,
        grid_spec=grid_spec,
        out_shape=jax.ShapeDtypeStruct((n_rows, D), F32),
        compiler_params=_params(1),
        name="moe_ffn",
    )(blk_e, row_tok, x1, w_gate, w_up, w_down)


def _combine_ln_kernel(dest_ref, y_hbm, x1_ref, gate_ref, g_ref, beta_ref, x2_ref, x2b_ref, ybuf, sem, *, bm, n_tok):
    def start_gather(tile, s):
        base = tile * bm
        for j in range(bm):
            for k in range(TOP_K):
                _row_copy(y_hbm, dest_ref[k * n_tok + base + j], ybuf.at[s], k * bm + j, sem.at[s]).start()

    def wait_gather(s):
        pltpu.make_async_copy(y_hbm.at[pl.ds(0, TOP_K * bm), :], ybuf.at[s], sem.at[s]).wait()

    def finish(s):
        gate = gate_ref[...]
        moe = ybuf[s, :bm] * gate[:, 0:1] + ybuf[s, bm:] * gate[:, 1:2]
        x2 = _layer_norm(ALPHA * x1_ref[...] + moe, g_ref[...], beta_ref[...])
        x2_ref[...] = x2
        x2b_ref[...] = x2.astype(BF16)

    _gather_ring_step(start_gather, wait_gather, finish)


def combine_ln(y_rows, dest, x1, gate_t, g, beta, *, bm=256):
    T, D = x1.shape
    assert T // bm >= GATHER_DEPTH
    grid_spec = pltpu.PrefetchScalarGridSpec(
        num_scalar_prefetch=1,
        grid=(T // bm,),
        in_specs=[pl.BlockSpec(memory_space=pl.ANY),
                  pl.BlockSpec((bm, D), lambda i, d: (i, 0)),
                  pl.BlockSpec((bm, TOP_K), lambda i, d: (i, 0)),
                  pl.BlockSpec((1, D), lambda i, d: (0, 0)),
                  pl.BlockSpec((1, D), lambda i, d: (0, 0))],
        out_specs=[pl.BlockSpec((bm, D), lambda i, d: (i, 0)),
                   pl.BlockSpec((bm, D), lambda i, d: (i, 0))],
        scratch_shapes=[pltpu.VMEM((GATHER_DEPTH, TOP_K * bm, D), F32), pltpu.SemaphoreType.DMA((GATHER_DEPTH,))],
    )
    return pl.pallas_call(
        functools.partial(_combine_ln_kernel, bm=bm, n_tok=T),
        grid_spec=grid_spec,
        out_shape=[jax.ShapeDtypeStruct((T, D), F32), jax.ShapeDtypeStruct((T, D), BF16)],
        compiler_params=_params(1),
        name="combine_ln",
    )(dest, y_rows, x1, gate_t, g, beta)


def moe_plan(eid):
    T = eid.shape[1]
    n_assign = TOP_K * T
    n_blocks = (n_assign + N_EXPERTS * (MOE_BLOCK - 1) + MOE_BLOCK - 1) // MOE_BLOCK
    flat_e = eid.reshape(n_assign)
    onehot = (flat_e[:, None] == jnp.arange(N_EXPERTS, dtype=jnp.int32)[None, :]).astype(jnp.int32)
    csum = jnp.cumsum(onehot, axis=0)
    rank = jnp.sum(csum * onehot, axis=1) - 1
    counts = csum[-1]
    pcounts = (counts + MOE_BLOCK - 1) // MOE_BLOCK * MOE_BLOCK
    pends = jnp.cumsum(pcounts)
    pstarts = pends - pcounts
    dest = pstarts[flat_e] + rank
    tok = jnp.tile(jnp.arange(T, dtype=jnp.int32), TOP_K)
    row_tok = jnp.zeros((n_blocks * MOE_BLOCK,), jnp.int32).at[dest].set(tok)
    blk_start = jnp.arange(n_blocks, dtype=jnp.int32) * MOE_BLOCK
    blk_e = jnp.minimum(jnp.sum((pends[None, :] <= blk_start[:, None]).astype(jnp.int32), axis=1), N_EXPERTS - 1)
    return dest.astype(jnp.int32), row_tok, blk_e.astype(jnp.int32)


def moe_layer(x1, eid, gate, w_gate, w_up, w_down, ln_g, ln_b):
    dest, row_tok, blk_e = moe_plan(eid)
    y_rows = moe_ffn(x1, w_gate, w_up, w_down, blk_e, row_tok)
    return combine_ln(y_rows, dest, x1, gate.T, ln_g.reshape(1, -1), ln_b.reshape(1, -1))


def _rope_cos_sin(pos, dim):
    inv_freq = ROPE_THETA ** (-jnp.arange(0, dim, 2, dtype=F32) / dim)
    ang = pos.astype(F32)[:, None] * inv_freq[None, :]
    cos = jnp.cos(ang)
    sin = jnp.sin(ang)
    return jnp.concatenate([cos, cos], axis=-1), jnp.concatenate([-sin, sin], axis=-1)


def axial_rope_tables(seq):
    pos = jnp.arange(seq)
    half = HEAD_DIM // 2
    cr, sr = _rope_cos_sin(pos // GRID_W, half)
    cc, sc = _rope_cos_sin(pos % GRID_W, half)
    return jnp.concatenate([cr, cc], axis=-1), jnp.concatenate([sr, sc], axis=-1)


def mla_rope_tables(seq):
    c, s = _rope_cos_sin(jnp.arange(seq), D_ROPE)
    z = jnp.zeros((seq, LANES - D_ROPE), F32)
    return jnp.concatenate([c, z], axis=-1), jnp.concatenate([s, z], axis=-1)


def router_layout(router_w, router_b):
    perm = np.array([g * EXPERTS_PER_GROUP + j for j in range(EXPERTS_PER_GROUP) for g in range(N_GROUPS)])
    rw = router_w.astype(F32).T[perm]
    hi = rw.astype(BF16)
    lo = (rw - hi.astype(F32)).astype(BF16)
    return hi, lo, router_b.astype(F32)[perm].reshape(N_EXPERTS, 1)


def mla_up_layout(w_q_up, w_kv_up, w_in_tail):
    rq = w_q_up.shape[0]
    wq = w_q_up.reshape(rq, D_HEADS, D_NOPE + D_ROPE)
    wq = jnp.pad(wq, ((0, 0), (0, 0), (0, 2 * LANES - (D_NOPE + D_ROPE)))).reshape(rq, D_HEADS * 2 * LANES)
    wc = jnp.pad(w_in_tail, ((0, 0), (0, LANES - D_ROPE)))
    return wq.astype(BF16), w_kv_up.astype(BF16), wc.astype(BF16)


def kernel(x, router_w, router_b, l0_w_in, l0_a_q_norm, l0_a_k_norm, l0_w_out, l0_ln1_g, l0_ln1_b, l0_w_gate, l0_w_up, l0_w_down, l0_ln2_g, l0_ln2_b, l1_w_in, l1_c_rpb, l1_d_q_norm, l1_d_w_q_up, l1_d_kv_norm, l1_d_w_kv_up, l1_w_out, l1_ln1_g, l1_ln1_b, l1_w_gate, l1_w_up, l1_w_down, l1_ln2_g, l1_ln2_b):
    batch, seq, d = x.shape
    T = batch * seq
    xf = x.reshape(T, d)
    rw_hi, rw_lo, rb = router_layout(router_w, router_b)
    row = lambda v: v.astype(F32).reshape(1, -1)

    a_q_dim = A_HEADS * HEAD_DIM
    a_qk_dim = a_q_dim + A_KV_HEADS * HEAD_DIM
    w0 = l0_w_in.astype(BF16)
    qk_scale = LOG2E * HEAD_DIM ** -0.5
    gain = jnp.concatenate([jnp.tile(l0_a_q_norm.astype(F32), A_HEADS) * qk_scale,
                            jnp.tile(l0_a_k_norm.astype(F32), A_KV_HEADS)]).reshape(1, -1)
    cos_a, sin_a = axial_rope_tables(seq)
    qk_a = proj_norm_rope(xf, w0[:, :a_qk_dim], gain, cos_a, sin_a, seq=seq)
    nb = A_KV_HEADS
    q_scale = jnp.full((B_HEADS * HEAD_DIM,), qk_scale, F32)
    scale0 = jnp.concatenate([jnp.ones((nb * HEAD_DIM,), F32), q_scale, jnp.ones((2 * B_HEADS * HEAD_DIM,), F32)])
    rest0 = proj_heads(xf, w0[:, a_qk_dim:], scale0.reshape(1, -1), heads_per_tile=13)
    group = A_HEADS // A_KV_HEADS
    oa = dense_attention(qk_a, qk_a, rest0, n_heads=A_HEADS, heads=group, k0=A_HEADS, v0=0, batch=batch, seq=seq,
                         group=group)
    ob = dilated_attention(rest0, dilated_bias_table(B_HEADS), n_heads=B_HEADS, heads=1, q0=nb, k0=nb + B_HEADS,
                           v0=nb + 2 * B_HEADS, batch=batch, seq=seq)
    x1 = out_ln(oa, ob, l0_w_out.astype(BF16), xf, row(l0_ln1_g), row(l0_ln1_b))
    eid, gate = route(x1, rw_hi, rw_lo, rb)
    x2, x2b = moe_layer(x1, eid, gate, l0_w_gate, l0_w_up, l0_w_down, l0_ln2_g, l0_ln2_b)

    c_dim = C_HEADS * HEAD_DIM
    w1 = l1_w_in
    scale1 = jnp.concatenate([jnp.full((c_dim,), qk_scale, F32), jnp.ones((2 * c_dim,), F32)])
    qkv_c = proj_heads(x2b, w1[:, :3 * c_dim].astype(BF16), scale1.reshape(1, -1), heads_per_tile=12)
    oc = na_attention(qkv_c, na_bias_table(l1_c_rpb, seq // GRID_W), n_heads=C_HEADS, heads=2, q0=0, k0=C_HEADS,
                      v0=2 * C_HEADS, batch=batch, seq=seq)
    wq, wkv, wc = mla_up_layout(l1_d_w_q_up, l1_d_w_kv_up, w1[:, 3 * c_dim:])
    cos_d, sin_d = mla_rope_tables(seq)
    qd, kd, vd = mla_proj(x2b, wc, row(l1_d_q_norm), row(l1_d_kv_norm), wq, wkv, cos_d, sin_d, seq=seq,
                          scale=LOG2E * (D_NOPE + D_ROPE) ** -0.5)
    od = dense_attention(qd, kd, vd, n_heads=D_HEADS, heads=1, k0=0, v0=0, batch=batch, seq=seq, group=1)
    x3 = out_ln(oc, od, l1_w_out.astype(BF16), x2, row(l1_ln1_g), row(l1_ln1_b))
    eid, gate = route(x3, rw_hi, rw_lo, rb)
    x4, _ = moe_layer(x3, eid, gate, l1_w_gate, l1_w_up, l1_w_down, l1_ln2_g, l1_ln2_b)
    return x4.reshape(batch, seq, d)
```

```python
import functools
import math

import jax
import jax.numpy as jnp
import numpy as np
from jax import lax
from jax.experimental import pallas as pl
from jax.experimental.pallas import tpu as pltpu

D_MODEL = 2048
DEPTH = 2
GRID_W = 64
HEAD_DIM = 128
ROPE_THETA = 10000.0
RMS_EPS = 1e-6
LN_EPS = 1e-5

A_HEADS = 8
A_KV_HEADS = 2
B_HEADS = 8
B_BRANCHES = ((128, 1), (512, 4), (2048, 16))
C_HEADS = 8
NA_ROWS = 8
NA_COLS = 16
D_HEADS = 8
D_Q_RANK = 512
D_KV_RANK = 256
D_NOPE = 128
D_ROPE = 64
D_V = 128

N_EXPERTS = 32
N_GROUPS = 8
EXPERTS_PER_GROUP = N_EXPERTS // N_GROUPS
TOP_K = 2
D_EXPERT = 512
MOE_BLOCK = 256

ALPHA = (2.0 * DEPTH) ** 0.25
LOG2E = math.log2(math.e)

LANES = 128
VMEM_LIMIT = 56 * 1024 * 1024

BF16 = jnp.bfloat16
F32 = jnp.float32


def _params(n_axes, **flags):
    return pltpu.CompilerParams(dimension_semantics=("arbitrary",) * n_axes, vmem_limit_bytes=VMEM_LIMIT,
                                flags=flags or None)


def _rot_half32(y):
    lane = lax.broadcasted_iota(jnp.int32, y.shape, y.ndim - 1)
    up = pltpu.roll(y, 96, axis=y.ndim - 1)
    dn = pltpu.roll(y, 32, axis=y.ndim - 1)
    return jnp.where((lane % 64) < 32, up, dn)


def _proj_heads_kernel(x_ref, w_ref, c_ref, o_ref, *, heads_per_tile):
    acc = jnp.dot(x_ref[...].astype(BF16), w_ref[...], preferred_element_type=F32) * c_ref[...]
    for h in range(heads_per_tile):
        o_ref[h] = acc[:, h * LANES:(h + 1) * LANES].astype(o_ref.dtype)


def proj_heads(x, w, col_scale, *, heads_per_tile, bm=512):
    T, D = x.shape
    H = w.shape[1] // LANES
    bn = heads_per_tile * LANES
    grid = (H // heads_per_tile, T // bm)
    return pl.pallas_call(
        functools.partial(_proj_heads_kernel, heads_per_tile=heads_per_tile),
        grid=grid,
        in_specs=[pl.BlockSpec((bm, D), lambda j, i: (i, 0)),
                  pl.BlockSpec((D, bn), lambda j, i: (0, j)),
                  pl.BlockSpec((1, bn), lambda j, i: (0, j))],
        out_specs=pl.BlockSpec((heads_per_tile, bm, LANES), lambda j, i: (j, i, 0)),
        out_shape=jax.ShapeDtypeStruct((H, T, LANES), BF16),
        compiler_params=_params(2),
        name="proj_heads",
    )(x, w, col_scale)


def _resident(shape):
    return pl.BlockSpec(shape, lambda *_: (0,) * len(shape), pipeline_mode=pl.Buffered(1))


def _proj_norm_rope_kernel(x_ref, w_ref, g_ref, cos_ref, sin_ref, o_ref):
    acc = jnp.dot(x_ref[...].astype(BF16), w_ref[...], preferred_element_type=F32)
    cos = cos_ref[...]
    sin = sin_ref[...]
    for h in range(o_ref.shape[0]):
        t = acc[:, h * LANES:(h + 1) * LANES]
        ms = jnp.mean(t * t, axis=-1, keepdims=True)
        y = t * lax.rsqrt(ms + RMS_EPS) * g_ref[:, h * LANES:(h + 1) * LANES]
        o_ref[h] = (y * cos + _rot_half32(y) * sin).astype(o_ref.dtype)


def proj_norm_rope(x, w, gain, cos, sin, *, seq, bm=512):
    T, D = x.shape
    H = w.shape[1] // LANES
    nsb = seq // bm
    return pl.pallas_call(
        _proj_norm_rope_kernel,
        grid=(T // bm,),
        in_specs=[pl.BlockSpec((bm, D), lambda i: (i, 0)),
                  _resident(w.shape), _resident(gain.shape),
                  pl.BlockSpec((bm, LANES), lambda i: (i % nsb, 0)),
                  pl.BlockSpec((bm, LANES), lambda i: (i % nsb, 0))],
        out_specs=pl.BlockSpec((H, bm, LANES), lambda i: (0, i, 0)),
        out_shape=jax.ShapeDtypeStruct((H, T, LANES), BF16),
        compiler_params=_params(1),
        name="proj_norm_rope",
    )(x, w, gain, cos, sin)


def _mla_proj_kernel(x_ref, wc_ref, gq_ref, gkv_ref, wq_ref, wkv_ref, cos_ref, sin_ref,
                     q_ref, k_ref, v_ref, *, scale):
    c = jnp.dot(x_ref[...], wc_ref[...], preferred_element_type=F32)
    cos = cos_ref[...]
    sin = sin_ref[...]

    def rms(t, g):
        ms = jnp.mean(t * t, axis=-1, keepdims=True)
        return t * lax.rsqrt(ms + RMS_EPS) * g

    cq = rms(c[:, :D_Q_RANK], gq_ref[...]).astype(BF16)
    ckv = rms(c[:, D_Q_RANK:D_Q_RANK + D_KV_RANK], gkv_ref[...]).astype(BF16)
    kr = c[:, D_Q_RANK + D_KV_RANK:]
    kr = (kr * cos + _rot_half32(kr) * sin).astype(k_ref.dtype)
    q = jnp.dot(cq, wq_ref[...], preferred_element_type=F32) * scale
    kv = jnp.dot(ckv, wkv_ref[...], preferred_element_type=F32)
    for h in range(D_HEADS):
        base = h * 2 * LANES
        q_ref[h, :, :LANES] = q[:, base:base + LANES].astype(q_ref.dtype)
        qr = q[:, base + LANES:base + 2 * LANES]
        q_ref[h, :, LANES:] = (qr * cos + _rot_half32(qr) * sin).astype(q_ref.dtype)
        k_ref[h, :, :LANES] = kv[:, base:base + LANES].astype(k_ref.dtype)
        k_ref[h, :, LANES:] = kr
        v_ref[h] = kv[:, base + LANES:base + 2 * LANES].astype(v_ref.dtype)


def mla_proj(x, wc, gq, gkv, wq, wkv, cos, sin, *, seq, scale, bm=512):
    T, D = x.shape
    nsb = seq // bm
    return pl.pallas_call(
        functools.partial(_mla_proj_kernel, scale=scale),
        grid=(T // bm,),
        in_specs=[pl.BlockSpec((bm, D), lambda i: (i, 0)),
                  _resident(wc.shape), _resident(gq.shape), _resident(gkv.shape), _resident(wq.shape),
                  _resident(wkv.shape),
                  pl.BlockSpec((bm, LANES), lambda i: (i % nsb, 0)),
                  pl.BlockSpec((bm, LANES), lambda i: (i % nsb, 0))],
        out_specs=[pl.BlockSpec((D_HEADS, bm, 2 * LANES), lambda i: (0, i, 0)),
                   pl.BlockSpec((D_HEADS, bm, 2 * LANES), lambda i: (0, i, 0)),
                   pl.BlockSpec((D_HEADS, bm, LANES), lambda i: (0, i, 0))],
        out_shape=[jax.ShapeDtypeStruct((D_HEADS, T, 2 * LANES), BF16),
                   jax.ShapeDtypeStruct((D_HEADS, T, 2 * LANES), BF16),
                   jax.ShapeDtypeStruct((D_HEADS, T, LANES), BF16)],
        compiler_params=_params(1),
        name="mla_proj",
    )(x, wc, gq, gkv, wq, wkv, cos, sin)


DENSE_TQ = 512
ATTN_TQ = 256


def _attn_pipeline(q_ref, k_ref, v_ref, o_ref, vext_ref, s_buf, m_buf, p_buf, *, heads, group, win, win_start, bias):
    tq = s_buf[0].shape[0]
    S = q_ref.shape[1]
    nblk = S // tq
    total = heads * nblk
    vext_ref[:, :, :LANES] = v_ref[...]
    vext_ref[:, :, LANES:] = jnp.ones(v_ref.shape, vext_ref.dtype)

    def locate(n):
        h, blk = (0, n) if heads == 1 else (n // nblk, n % nblk)
        rows = pl.ds(blk * tq, tq) if isinstance(blk, int) else pl.ds(pl.multiple_of(blk * tq, tq), tq)
        keys = slice(None) if win == S else pl.ds(win_start(blk), win)
        return h, blk, rows, keys

    def scores(n, par):
        h, blk, rows, keys = locate(n)
        s = lax.dot_general(q_ref[h, rows, :], k_ref[h // group, keys, :], (((1,), (1,)), ((), ())),
                            preferred_element_type=F32)
        if bias is not None:
            s = s + bias(h, blk)
        s_buf[par][...] = s
        m_buf[par][...] = jnp.max(s, axis=-1, keepdims=True)

    def exps(par):
        p_buf[par][...] = jnp.exp2(s_buf[par][...] - m_buf[par][...]).astype(BF16)

    def wsum(n, par):
        h, blk, rows, keys = locate(n)
        o_ext = jnp.dot(p_buf[par][...], vext_ref[h // group, keys, :], preferred_element_type=F32)
        o_ref[h, rows, :] = (o_ext[:, :LANES] / o_ext[:, LANES:]).astype(o_ref.dtype)

    scores(0, 0)
    exps(0)
    scores(1, 1)

    def step(j, carry):
        n = 2 * j + 1
        exps(1)
        wsum(n - 1, 0)
        scores(n + 1, 0)
        exps(0)
        wsum(n, 1)
        scores(n + 2, 1)
        return carry

    lax.fori_loop(0, total // 2 - 1, step, 0)
    exps(1)
    wsum(total - 2, 0)
    wsum(total - 1, 1)


def _attn_scratch(kv_heads, seq, win, tq):
    return [pltpu.VMEM((kv_heads, seq, 2 * LANES), BF16),
            pltpu.VMEM((tq, win), F32), pltpu.VMEM((tq, win), F32),
            pltpu.VMEM((tq, 1), F32), pltpu.VMEM((tq, 1), F32),
            pltpu.VMEM((tq, win), BF16), pltpu.VMEM((tq, win), BF16)]


def _dense_attn_kernel(q_ref, k_ref, v_ref, o_ref, vext_ref, s0, s1, m0, m1, p0, p1, *, heads, group):
    _attn_pipeline(q_ref, k_ref, v_ref, o_ref, vext_ref, (s0, s1), (m0, m1), (p0, p1), heads=heads, group=group,
                   win=q_ref.shape[1], win_start=None, bias=None)


def dense_attention(q, k, v, *, n_heads, heads, k0, v0, batch, seq, group):
    T, dq = q.shape[1:]
    kvh = heads // group
    assert n_heads % heads == 0 and heads % group == 0 and k0 % kvh == 0 and v0 % kvh == 0
    return pl.pallas_call(
        functools.partial(_dense_attn_kernel, heads=heads, group=group),
        grid=(batch, n_heads // heads),
        in_specs=[pl.BlockSpec((heads, seq, dq), lambda b, g: (g, b, 0)),
                  pl.BlockSpec((kvh, seq, dq), lambda b, g: (k0 // kvh + g, b, 0)),
                  pl.BlockSpec((kvh, seq, LANES), lambda b, g: (v0 // kvh + g, b, 0))],
        out_specs=pl.BlockSpec((heads, seq, LANES), lambda b, g: (g, b, 0)),
        out_shape=jax.ShapeDtypeStruct((n_heads, T, LANES), BF16),
        scratch_shapes=_attn_scratch(kvh, seq, seq, DENSE_TQ),
        compiler_params=_params(2),
        name="dense_attention",
    )(q, k, v)


DIL_REACH = max(w // 2 for w, _ in B_BRANCHES)
DIL_WIN = 2 * DIL_REACH + ATTN_TQ
DIL_TABLE_TILES = (DIL_WIN + 2 * DIL_REACH) // LANES


def _dilated_attn_kernel(q_ref, k_ref, v_ref, bias_ref, o_ref, vext_ref, s0, s1, m0, m1, p0, p1, *, heads):
    S = q_ref.shape[1]

    def win_start(blk):
        return pl.multiple_of(jnp.clip(blk * ATTN_TQ - DIL_REACH, 0, S - DIL_WIN), ATTN_TQ)

    def bias(h, blk):
        tile0 = (win_start(blk) - blk * ATTN_TQ + 2 * DIL_REACH) // LANES
        return jnp.concatenate([bias_ref[h, tile0 + t] for t in range(DIL_WIN // LANES)], axis=1)

    _attn_pipeline(q_ref, k_ref, v_ref, o_ref, vext_ref, (s0, s1), (m0, m1), (p0, p1), heads=heads, group=1,
                   win=DIL_WIN, win_start=win_start, bias=bias)


def dilated_bias_table(n_heads):
    col = (np.arange(DIL_TABLE_TILES)[:, None, None] * LANES + np.arange(LANES)[None, None, :])
    d = np.abs(col - np.arange(ATTN_TQ)[None, :, None] - 2 * DIL_REACH)
    mult = np.zeros(d.shape, np.float64)
    for window, dil in B_BRANCHES:
        mult += ((d % dil == 0) & (d <= window // 2)).astype(np.float64)
    with np.errstate(divide="ignore"):
        logm = np.log(mult)
    slopes = jnp.asarray(2.0 ** (-8.0 * np.arange(1, n_heads + 1) / n_heads), dtype=F32)
    bias = jnp.asarray(logm, dtype=F32)[None] - slopes[:, None, None, None] * jnp.asarray(d, dtype=F32)[None]
    return bias * LOG2E


def dilated_attention(qkv, bias, *, n_heads, heads, q0, k0, v0, batch, seq):
    T = qkv.shape[1]
    assert n_heads % heads == 0 and q0 % heads == 0 and k0 % heads == 0 and v0 % heads == 0
    qkv_spec = lambda h0: pl.BlockSpec((heads, seq, LANES), lambda b, g: (h0 // heads + g, b, 0))
    return pl.pallas_call(
        functools.partial(_dilated_attn_kernel, heads=heads),
        grid=(batch, n_heads // heads),
        in_specs=[qkv_spec(q0), qkv_spec(k0), qkv_spec(v0),
                  pl.BlockSpec((heads, DIL_TABLE_TILES, ATTN_TQ, LANES), lambda b, g: (g, 0, 0, 0))],
        out_specs=pl.BlockSpec((heads, seq, LANES), lambda b, g: (g, b, 0)),
        out_shape=jax.ShapeDtypeStruct((n_heads, T, LANES), BF16),
        scratch_shapes=_attn_scratch(heads, seq, DIL_WIN, ATTN_TQ),
        compiler_params=_params(2),
        name="dilated_attention",
    )(qkv, qkv, qkv, bias)


NA_QROWS = ATTN_TQ // GRID_W
NA_KROWS = NA_QROWS + NA_ROWS
NA_WIN = NA_KROWS * GRID_W


def _na_window_row(blk, rows):
    lo = blk * NA_QROWS - NA_ROWS // 2
    return jnp.clip(lo, 0, rows - NA_KROWS) if not isinstance(blk, int) else min(max(lo, 0), rows - NA_KROWS)


def _na_attn_kernel(q_ref, k_ref, v_ref, bias_ref, o_ref, vext_ref, s0, s1, m0, m1, p0, p1, *, heads, rows):
    nblk = rows // NA_QROWS

    def win_start(blk):
        return pl.multiple_of(_na_window_row(blk, rows) * GRID_W, GRID_W)

    def bias(h, blk):
        cls = (blk > 0).astype(jnp.int32) + (blk == nblk - 1).astype(jnp.int32) if not isinstance(blk, int) \
            else int(blk > 0) + int(blk == nblk - 1)
        return bias_ref[h, cls]

    _attn_pipeline(q_ref, k_ref, v_ref, o_ref, vext_ref, (s0, s1), (m0, m1), (p0, p1), heads=heads, group=1,
                   win=NA_WIN, win_start=win_start, bias=bias)


def na_bias_table(rpb, rows):
    H = rpb.shape[0]
    col = np.arange(GRID_W)
    c0 = np.clip(col - NA_COLS // 2, 0, GRID_W - NA_COLS)
    col_ok = (col[None, :] >= c0[:, None]) & (col[None, :] < c0[:, None] + NA_COLS)
    dcol = np.clip(col[None, :] - col[:, None] + NA_COLS - 1, 0, 2 * NA_COLS - 2)
    pick = ((dcol[:, :, None] == np.arange(2 * NA_COLS - 1)) & col_ok[:, :, None]).astype(np.float32)
    t = jnp.einsum("hrd,qkd->hrqk", rpb.astype(F32), jnp.asarray(pick), precision=lax.Precision.HIGHEST)
    t = jnp.where(jnp.asarray(col_ok)[None, None], t * LOG2E, -jnp.inf)
    nblk = rows // NA_QROWS

    def geometry(blk):
        r = blk * NA_QROWS + np.arange(NA_QROWS)
        kr = _na_window_row(blk, rows) + np.arange(NA_KROWS)
        r0 = np.clip(r - NA_ROWS // 2, 0, rows - NA_ROWS)
        valid = (kr[None, :] >= r0[:, None]) & (kr[None, :] < r0[:, None] + NA_ROWS)
        drow = np.clip(kr[None, :] - r[:, None] + NA_ROWS - 1, 0, 2 * NA_ROWS - 2)
        return valid, np.where(valid, drow, 0)

    inner = geometry(1)
    assert all(np.array_equal(a, b) for blk in range(1, nblk - 1) for a, b in zip(geometry(blk), inner))
    tables = []
    for blk in (0, 1, nblk - 1):
        valid, drow = geometry(blk)
        slab = jnp.where(jnp.asarray(valid)[None, :, :, None, None], t[:, drow], -jnp.inf)
        tables.append(slab.transpose(0, 1, 3, 2, 4).reshape(H, ATTN_TQ, NA_WIN))
    return jnp.stack(tables, axis=1)


def na_attention(qkv, bias, *, n_heads, heads, q0, k0, v0, batch, seq):
    T = qkv.shape[1]
    rows = seq // GRID_W
    assert n_heads % heads == 0 and q0 % heads == 0 and k0 % heads == 0 and v0 % heads == 0
    assert rows % NA_QROWS == 0 and rows // NA_QROWS >= 3
    qkv_spec = lambda h0: pl.BlockSpec((heads, seq, LANES), lambda b, g: (h0 // heads + g, b, 0))
    return pl.pallas_call(
        functools.partial(_na_attn_kernel, heads=heads, rows=rows),
        grid=(batch, n_heads // heads),
        in_specs=[qkv_spec(q0), qkv_spec(k0), qkv_spec(v0),
                  pl.BlockSpec((heads, 3, ATTN_TQ, NA_WIN), lambda b, g: (g, 0, 0, 0))],
        out_specs=pl.BlockSpec((heads, seq, LANES), lambda b, g: (g, b, 0)),
        out_shape=jax.ShapeDtypeStruct((n_heads, T, LANES), BF16),
        scratch_shapes=_attn_scratch(heads, seq, NA_WIN, ATTN_TQ),
        compiler_params=_params(2),
        name="na_attention",
    )(qkv, qkv, qkv, bias)


def _layer_norm(z, g, b):
    mu = jnp.mean(z, axis=-1, keepdims=True)
    zc = z - mu
    var = jnp.mean(zc * zc, axis=-1, keepdims=True)
    return zc * lax.rsqrt(var + LN_EPS) * g + b


def _top2_of4(v0, v1, v2, v3):
    a = jnp.maximum(v0, v1)
    b = jnp.minimum(v0, v1)
    c = jnp.maximum(v2, v3)
    d = jnp.minimum(v2, v3)
    return jnp.maximum(a, c), jnp.maximum(jnp.minimum(a, c), jnp.maximum(b, d))


def _route(logits_t, rb):
    G = N_GROUPS
    scores = jax.nn.sigmoid(logits_t)
    sel = scores + rb
    sj = [sel[j * G:(j + 1) * G] for j in range(EXPERTS_PER_GROUP)]
    cj = [scores[j * G:(j + 1) * G] for j in range(EXPERTS_PER_GROUP)]
    t1, t2 = _top2_of4(*sj)
    grp = t1 + t2
    gi = lax.broadcasted_iota(jnp.int32, grp.shape, 0).astype(F32)
    gbest = jnp.min(jnp.where(grp == jnp.max(grp, axis=0, keepdims=True), gi, float(G)), axis=0, keepdims=True)
    pick = gi == gbest
    v = [jnp.sum(jnp.where(pick, s, 0.0), axis=0, keepdims=True) for s in sj]
    c = [jnp.sum(jnp.where(pick, s, 0.0), axis=0, keepdims=True) for s in cj]
    neg = jnp.float32(-jnp.inf)
    m1 = jnp.maximum(jnp.maximum(v[0], v[1]), jnp.maximum(v[2], v[3]))
    j1 = jnp.where(v[0] == m1, 0, jnp.where(v[1] == m1, 1, jnp.where(v[2] == m1, 2, 3)))
    w = [jnp.where(j1 == j, neg, v[j]) for j in range(4)]
    m2 = jnp.maximum(jnp.maximum(w[0], w[1]), jnp.maximum(w[2], w[3]))
    j2 = jnp.where(w[0] == m2, 0, jnp.where(w[1] == m2, 1, jnp.where(w[2] == m2, 2, 3)))
    g1 = jnp.where(j1 == 0, c[0], jnp.where(j1 == 1, c[1], jnp.where(j1 == 2, c[2], c[3])))
    g2 = jnp.where(j2 == 0, c[0], jnp.where(j2 == 1, c[1], jnp.where(j2 == 2, c[2], c[3])))
    tot = g1 + g2
    e0 = gbest.astype(jnp.int32) * EXPERTS_PER_GROUP
    return (e0 + j1, e0 + j2), (g1 / tot, g2 / tot)


OUT_SUBTILES = 2


def _out_ln_kernel(a_ref, b_ref, w_ref, x_ref, g_ref, beta_ref, x1_ref):
    sub = x_ref.shape[0] // OUT_SUBTILES
    for t in range(OUT_SUBTILES):
        rows = slice(t * sub, (t + 1) * sub)
        heads = [a_ref[h, rows, :] for h in range(a_ref.shape[0])] + [b_ref[h, rows, :] for h in range(b_ref.shape[0])]
        mix = jnp.dot(jnp.concatenate(heads, axis=1), w_ref[...], preferred_element_type=F32)
        x1_ref[rows, :] = _layer_norm(ALPHA * x_ref[rows, :] + mix, g_ref[...], beta_ref[...])


def out_ln(a, b, w, x, g, beta, *, bm=512):
    T, D = x.shape
    return pl.pallas_call(
        _out_ln_kernel,
        grid=(T // bm,),
        in_specs=[pl.BlockSpec((a.shape[0], bm, LANES), lambda i: (0, i, 0)),
                  pl.BlockSpec((b.shape[0], bm, LANES), lambda i: (0, i, 0)),
                  _resident(w.shape),
                  pl.BlockSpec((bm, D), lambda i: (i, 0)),
                  _resident(g.shape), _resident(beta.shape)],
        out_specs=pl.BlockSpec((bm, D), lambda i: (i, 0)),
        out_shape=jax.ShapeDtypeStruct((T, D), F32),
        compiler_params=_params(1),
        name="out_ln",
    )(a, b, w, x, g, beta)


def _route_kernel(x_ref, rwh_ref, rwl_ref, rb_ref, eid_ref, gate_ref):
    x = x_ref[...]
    hi = x.astype(BF16)
    lo = (x - hi.astype(F32)).astype(BF16)
    nt = (((1,), (1,)), ((), ()))
    logits_t = (lax.dot_general(rwh_ref[...], hi, nt, preferred_element_type=F32)
                + lax.dot_general(rwl_ref[...], hi, nt, preferred_element_type=F32)
                + lax.dot_general(rwh_ref[...], lo, nt, preferred_element_type=F32))
    eid, gate = _route(logits_t, rb_ref[...])
    for k in range(TOP_K):
        eid_ref[k:k + 1, :] = eid[k]
        gate_ref[k:k + 1, :] = gate[k]


def route(x1, rw_hi, rw_lo, rb, *, bm=1024):
    T, D = x1.shape
    return pl.pallas_call(
        _route_kernel,
        grid=(T // bm,),
        in_specs=[pl.BlockSpec((bm, D), lambda i: (i, 0)),
                  _resident(rw_hi.shape), _resident(rw_lo.shape), _resident(rb.shape)],
        out_specs=[pl.BlockSpec((TOP_K, bm), lambda i: (0, i)),
                   pl.BlockSpec((TOP_K, bm), lambda i: (0, i))],
        out_shape=[jax.ShapeDtypeStruct((TOP_K, T), jnp.int32),
                   jax.ShapeDtypeStruct((TOP_K, T), F32)],
        compiler_params=_params(1),
        name="route",
    )(x1, rw_hi, rw_lo, rb)


GATHER_DEPTH = 3


def _row_copy(src_hbm, row, dst_vmem, j, sem):
    return pltpu.make_async_copy(src_hbm.at[pl.ds(row, 1), :], dst_vmem.at[pl.ds(j, 1), :], sem)


def _gather_ring_step(start_gather, wait_gather, compute, active=None, idle=None):
    i = pl.program_id(0)
    n_steps = pl.num_programs(0)
    ahead = GATHER_DEPTH - 1
    slot = i % GATHER_DEPTH

    @pl.when(i == 0)
    def _():
        for a in range(ahead):
            start_gather(a, a)

    wait_gather(slot)
    more = i + ahead < n_steps
    cases = [(more, compute)] if active is None else [(more & active, compute), (more & ~active, idle)]
    for cond, work in cases:
        @pl.when(cond)
        def _(work=work):
            start_gather(i + ahead, (i + ahead) % GATHER_DEPTH)
            work(slot)
    cases = [(~more, compute)] if active is None else [(~more & active, compute), (~more & ~active, idle)]
    for cond, work in cases:
        @pl.when(cond)
        def _(work=work):
            work(slot)


def _moe_ffn_kernel(blk_lo_ref, blk_hi_ref, nblk_ref, tok_ref, x_hbm,
                    wg_lo, wu_lo, wd_lo, wg_hi, wu_hi, wd_hi, y_ref, xbuf, sem):
    D = xbuf.shape[-1]

    def start_gather(blk, s):
        base = blk * MOE_BLOCK
        for j in range(MOE_BLOCK):
            _row_copy(x_hbm, tok_ref[base + j], xbuf.at[s], j, sem.at[s]).start()

    def wait_gather(s):
        pltpu.make_async_copy(x_hbm.at[pl.ds(0, MOE_BLOCK), :], xbuf.at[s], sem.at[s]).wait()

    def ffn(s):
        xb = xbuf[s].astype(BF16)
        for half, (wg, wu, wd) in enumerate(((wg_lo, wu_lo, wd_lo), (wg_hi, wu_hi, wd_hi))):
            hg = jnp.dot(xb, wg[...], preferred_element_type=F32)
            hu = jnp.dot(xb, wu[...], preferred_element_type=F32)
            hb = (jax.nn.silu(hg) * hu).astype(BF16)
            y_ref[:, half * D:(half + 1) * D] = jnp.dot(hb, wd[...], preferred_element_type=F32)

    def no_tokens(s):
        y_ref[...] = jnp.zeros(y_ref.shape, y_ref.dtype)

    _gather_ring_step(start_gather, wait_gather, ffn, active=pl.program_id(0) < nblk_ref[0], idle=no_tokens)


def moe_ffn(x1, w_gate, w_up, w_down, blk_lo, blk_hi, nblk, row_tok):
    T, D = x1.shape
    n_blocks = blk_lo.shape[0]
    assert n_blocks >= GATHER_DEPTH
    n_rows = n_blocks * MOE_BLOCK
    w_in = lambda which: pl.BlockSpec((None, D, D_EXPERT), lambda i, lo, hi, nb, tk: ((lo, hi)[which][i], 0, 0))
    w_out = lambda which: pl.BlockSpec((None, D_EXPERT, D), lambda i, lo, hi, nb, tk: ((lo, hi)[which][i], 0, 0))
    grid_spec = pltpu.PrefetchScalarGridSpec(
        num_scalar_prefetch=4,
        grid=(n_blocks,),
        in_specs=[pl.BlockSpec(memory_space=pl.ANY), w_in(0), w_in(0), w_out(0), w_in(1), w_in(1), w_out(1)],
        out_specs=pl.BlockSpec((MOE_BLOCK, 2 * D), lambda i, lo, hi, nb, tk: (i, 0)),
        scratch_shapes=[pltpu.VMEM((GATHER_DEPTH, MOE_BLOCK, D), F32), pltpu.SemaphoreType.DMA((GATHER_DEPTH,))],
    )
    return pl.pallas_call(
        _moe_ffn_kernel,
        grid_spec=grid_spec,
        out_shape=jax.ShapeDtypeStruct((n_rows, 2 * D), F32),
        compiler_params=_params(1),
        name="moe_ffn",
    )(blk_lo, blk_hi, nblk, row_tok, x1, w_gate, w_up, w_down, w_gate, w_up, w_down)


def _combine_ln_kernel(dest_ref, y_hbm, x1_ref, gate_ref, g_ref, beta_ref, x2_ref, x2b_ref, ybuf, sem, *, bm):
    D = x1_ref.shape[-1]

    def start_gather(tile, s):
        base = tile * bm
        for j in range(bm):
            _row_copy(y_hbm, dest_ref[base + j], ybuf.at[s], j, sem.at[s]).start()

    def wait_gather(s):
        pltpu.make_async_copy(y_hbm.at[pl.ds(0, bm), :], ybuf.at[s], sem.at[s]).wait()

    def finish(s):
        gate = gate_ref[...]
        moe = ybuf[s, :, :D] * gate[:, 0:1] + ybuf[s, :, D:] * gate[:, 1:2]
        x2 = _layer_norm(ALPHA * x1_ref[...] + moe, g_ref[...], beta_ref[...])
        x2_ref[...] = x2
        x2b_ref[...] = x2.astype(BF16)

    _gather_ring_step(start_gather, wait_gather, finish)


def combine_ln(y_rows, dest, x1, gate_t, g, beta, *, bm=256):
    T, D = x1.shape
    assert T // bm >= GATHER_DEPTH
    grid_spec = pltpu.PrefetchScalarGridSpec(
        num_scalar_prefetch=1,
        grid=(T // bm,),
        in_specs=[pl.BlockSpec(memory_space=pl.ANY),
                  pl.BlockSpec((bm, D), lambda i, d: (i, 0)),
                  pl.BlockSpec((bm, TOP_K), lambda i, d: (i, 0)),
                  pl.BlockSpec((1, D), lambda i, d: (0, 0)),
                  pl.BlockSpec((1, D), lambda i, d: (0, 0))],
        out_specs=[pl.BlockSpec((bm, D), lambda i, d: (i, 0)),
                   pl.BlockSpec((bm, D), lambda i, d: (i, 0))],
        scratch_shapes=[pltpu.VMEM((GATHER_DEPTH, bm, TOP_K * D), F32), pltpu.SemaphoreType.DMA((GATHER_DEPTH,))],
    )
    return pl.pallas_call(
        functools.partial(_combine_ln_kernel, bm=bm),
        grid_spec=grid_spec,
        out_shape=[jax.ShapeDtypeStruct((T, D), F32), jax.ShapeDtypeStruct((T, D), BF16)],
        compiler_params=_params(1),
        name="combine_ln",
    )(dest, y_rows, x1, gate_t, g, beta)


_PAIR_LO = np.array([a for a in range(EXPERTS_PER_GROUP) for b in range(a + 1, EXPERTS_PER_GROUP)], np.int32)
_PAIR_HI = np.array([b for a in range(EXPERTS_PER_GROUP) for b in range(a + 1, EXPERTS_PER_GROUP)], np.int32)
N_PAIRS = len(_PAIR_LO)
N_CLASSES = N_GROUPS * N_PAIRS


def moe_plan(eid, gate):
    T = eid.shape[1]
    n_blocks = (T + N_CLASSES * (MOE_BLOCK - 1) + MOE_BLOCK - 1) // MOE_BLOCK
    first_lower = eid[0] < eid[1]
    e_lo = jnp.minimum(eid[0], eid[1])
    e_hi = jnp.maximum(eid[0], eid[1])
    gate_t = jnp.stack([jnp.where(first_lower, gate[0], gate[1]), jnp.where(first_lower, gate[1], gate[0])], axis=1)
    a = e_lo % EXPERTS_PER_GROUP
    b = e_hi % EXPERTS_PER_GROUP
    pair = a * (2 * EXPERTS_PER_GROUP - 1 - a) // 2 + (b - a - 1)
    cls = (e_lo // EXPERTS_PER_GROUP) * N_PAIRS + pair
    onehot = (cls[:, None] == jnp.arange(N_CLASSES, dtype=jnp.int32)[None, :]).astype(jnp.int32)
    csum = jnp.cumsum(onehot, axis=0)
    rank = jnp.sum(csum * onehot, axis=1) - 1
    counts = csum[-1]
    pcounts = (counts + MOE_BLOCK - 1) // MOE_BLOCK * MOE_BLOCK
    pends = jnp.cumsum(pcounts)
    pstarts = pends - pcounts
    dest = (pstarts[cls] + rank).astype(jnp.int32)
    row_tok = jnp.zeros((n_blocks * MOE_BLOCK,), jnp.int32).at[dest].set(jnp.arange(T, dtype=jnp.int32))
    blk_start = jnp.arange(n_blocks, dtype=jnp.int32) * MOE_BLOCK
    blk_cls = jnp.minimum(jnp.sum((pends[None, :] <= blk_start[:, None]).astype(jnp.int32), axis=1), N_CLASSES - 1)
    blk_base = (blk_cls // N_PAIRS) * EXPERTS_PER_GROUP
    blk_lo = (blk_base + jnp.asarray(_PAIR_LO)[blk_cls % N_PAIRS]).astype(jnp.int32)
    blk_hi = (blk_base + jnp.asarray(_PAIR_HI)[blk_cls % N_PAIRS]).astype(jnp.int32)
    nblk = (pends[-1:] // MOE_BLOCK).astype(jnp.int32)
    return dest, row_tok, blk_lo, blk_hi, nblk, gate_t


def moe_layer(x1, eid, gate, w_gate, w_up, w_down, ln_g, ln_b):
    dest, row_tok, blk_lo, blk_hi, nblk, gate_t = moe_plan(eid, gate)
    y_rows = moe_ffn(x1, w_gate.astype(BF16), w_up.astype(BF16), w_down.astype(BF16), blk_lo, blk_hi, nblk, row_tok)
    return combine_ln(y_rows, dest, x1, gate_t, ln_g.reshape(1, -1), ln_b.reshape(1, -1))


def _rope_cos_sin(pos, dim):
    inv_freq = ROPE_THETA ** (-jnp.arange(0, dim, 2, dtype=F32) / dim)
    ang = pos.astype(F32)[:, None] * inv_freq[None, :]
    cos = jnp.cos(ang)
    sin = jnp.sin(ang)
    return jnp.concatenate([cos, cos], axis=-1), jnp.concatenate([-sin, sin], axis=-1)


def axial_rope_tables(seq):
    pos = jnp.arange(seq)
    half = HEAD_DIM // 2
    cr, sr = _rope_cos_sin(pos // GRID_W, half)
    cc, sc = _rope_cos_sin(pos % GRID_W, half)
    return jnp.concatenate([cr, cc], axis=-1), jnp.concatenate([sr, sc], axis=-1)


def mla_rope_tables(seq):
    c, s = _rope_cos_sin(jnp.arange(seq), D_ROPE)
    z = jnp.zeros((seq, LANES - D_ROPE), F32)
    return jnp.concatenate([c, z], axis=-1), jnp.concatenate([s, z], axis=-1)


def router_layout(router_w, router_b):
    perm = np.array([g * EXPERTS_PER_GROUP + j for j in range(EXPERTS_PER_GROUP) for g in range(N_GROUPS)])
    rw = router_w.astype(F32).T[perm]
    hi = rw.astype(BF16)
    lo = (rw - hi.astype(F32)).astype(BF16)
    return hi, lo, router_b.astype(F32)[perm].reshape(N_EXPERTS, 1)


def mla_up_layout(w_q_up, w_kv_up, w_in_tail):
    rq = w_q_up.shape[0]
    wq = w_q_up.reshape(rq, D_HEADS, D_NOPE + D_ROPE)
    wq = jnp.pad(wq, ((0, 0), (0, 0), (0, 2 * LANES - (D_NOPE + D_ROPE)))).reshape(rq, D_HEADS * 2 * LANES)
    wc = jnp.pad(w_in_tail, ((0, 0), (0, LANES - D_ROPE)))
    return wq.astype(BF16), w_kv_up.astype(BF16), wc.astype(BF16)


def kernel(x, router_w, router_b, l0_w_in, l0_a_q_norm, l0_a_k_norm, l0_w_out, l0_ln1_g, l0_ln1_b, l0_w_gate, l0_w_up, l0_w_down, l0_ln2_g, l0_ln2_b, l1_w_in, l1_c_rpb, l1_d_q_norm, l1_d_w_q_up, l1_d_kv_norm, l1_d_w_kv_up, l1_w_out, l1_ln1_g, l1_ln1_b, l1_w_gate, l1_w_up, l1_w_down, l1_ln2_g, l1_ln2_b):
    batch, seq, d = x.shape
    T = batch * seq
    xf = x.reshape(T, d)
    rw_hi, rw_lo, rb = router_layout(router_w, router_b)
    row = lambda v: v.astype(F32).reshape(1, -1)

    a_q_dim = A_HEADS * HEAD_DIM
    a_qk_dim = a_q_dim + A_KV_HEADS * HEAD_DIM
    w0 = l0_w_in.astype(BF16)
    qk_scale = LOG2E * HEAD_DIM ** -0.5
    gain = jnp.concatenate([jnp.tile(l0_a_q_norm.astype(F32), A_HEADS) * qk_scale,
                            jnp.tile(l0_a_k_norm.astype(F32), A_KV_HEADS)]).reshape(1, -1)
    cos_a, sin_a = axial_rope_tables(seq)
    qk_a = proj_norm_rope(xf, w0[:, :a_qk_dim], gain, cos_a, sin_a, seq=seq)
    nb = A_KV_HEADS
    q_scale = jnp.full((B_HEADS * HEAD_DIM,), qk_scale, F32)
    scale0 = jnp.concatenate([jnp.ones((nb * HEAD_DIM,), F32), q_scale, jnp.ones((2 * B_HEADS * HEAD_DIM,), F32)])
    rest0 = proj_heads(xf, w0[:, a_qk_dim:], scale0.reshape(1, -1), heads_per_tile=13)
    group = A_HEADS // A_KV_HEADS
    oa = dense_attention(qk_a, qk_a, rest0, n_heads=A_HEADS, heads=group, k0=A_HEADS, v0=0, batch=batch, seq=seq,
                         group=group)
    ob = dilated_attention(rest0, dilated_bias_table(B_HEADS), n_heads=B_HEADS, heads=1, q0=nb, k0=nb + B_HEADS,
                           v0=nb + 2 * B_HEADS, batch=batch, seq=seq)
    x1 = out_ln(oa, ob, l0_w_out.astype(BF16), xf, row(l0_ln1_g), row(l0_ln1_b))
    eid, gate = route(x1, rw_hi, rw_lo, rb)
    x2, x2b = moe_layer(x1, eid, gate, l0_w_gate, l0_w_up, l0_w_down, l0_ln2_g, l0_ln2_b)

    c_dim = C_HEADS * HEAD_DIM
    w1 = l1_w_in
    scale1 = jnp.concatenate([jnp.full((c_dim,), qk_scale, F32), jnp.ones((2 * c_dim,), F32)])
    qkv_c = proj_heads(x2b, w1[:, :3 * c_dim].astype(BF16), scale1.reshape(1, -1), heads_per_tile=12)
    oc = na_attention(qkv_c, na_bias_table(l1_c_rpb, seq // GRID_W), n_heads=C_HEADS, heads=2, q0=0, k0=C_HEADS,
                      v0=2 * C_HEADS, batch=batch, seq=seq)
    wq, wkv, wc = mla_up_layout(l1_d_w_q_up, l1_d_w_kv_up, w1[:, 3 * c_dim:])
    cos_d, sin_d = mla_rope_tables(seq)
    qd, kd, vd = mla_proj(x2b, wc, row(l1_d_q_norm), row(l1_d_kv_norm), wq, wkv, cos_d, sin_d, seq=seq,
                          scale=LOG2E * (D_NOPE + D_ROPE) ** -0.5)
    od = dense_attention(qd, kd, vd, n_heads=D_HEADS, heads=1, k0=0, v0=0, batch=batch, seq=seq, group=1)
    x3 = out_ln(oc, od, l1_w_out.astype(BF16), x2, row(l1_ln1_g), row(l1_ln1_b))
    eid, gate = route(x3, rw_hi, rw_lo, rb)
    x4, _ = moe_layer(x3, eid, gate, l1_w_gate, l1_w_up, l1_w_down, l1_ln2_g, l1_ln2_b)
    return x4.reshape(batch, seq, d)
```

```python
import functools
import math

import jax
import jax.numpy as jnp
import numpy as np
from jax import lax
from jax.experimental import pallas as pl
from jax.experimental.pallas import tpu as pltpu

D_MODEL = 2048
DEPTH = 2
GRID_W = 64
HEAD_DIM = 128
ROPE_THETA = 10000.0
RMS_EPS = 1e-6
LN_EPS = 1e-5

A_HEADS = 8
A_KV_HEADS = 2
B_HEADS = 8
B_BRANCHES = ((128, 1), (512, 4), (2048, 16))
C_HEADS = 8
NA_ROWS = 8
NA_COLS = 16
D_HEADS = 8
D_Q_RANK = 512
D_KV_RANK = 256
D_NOPE = 128
D_ROPE = 64
D_V = 128

N_EXPERTS = 32
N_GROUPS = 8
EXPERTS_PER_GROUP = N_EXPERTS // N_GROUPS
TOP_K = 2
D_EXPERT = 512
MOE_BLOCK = 256

ALPHA = (2.0 * DEPTH) ** 0.25
LOG2E = math.log2(math.e)

LANES = 128
VMEM_LIMIT = 56 * 1024 * 1024

BF16 = jnp.bfloat16
F32 = jnp.float32


def _params(n_axes, **flags):
    return pltpu.CompilerParams(dimension_semantics=("arbitrary",) * n_axes, vmem_limit_bytes=VMEM_LIMIT,
                                flags=flags or None)


def _rot_half32(y):
    lane = lax.broadcasted_iota(jnp.int32, y.shape, y.ndim - 1)
    up = pltpu.roll(y, 96, axis=y.ndim - 1)
    dn = pltpu.roll(y, 32, axis=y.ndim - 1)
    return jnp.where((lane % 64) < 32, up, dn)


def _proj_heads_kernel(x_ref, w_ref, c_ref, o_ref, *, heads_per_tile):
    acc = jnp.dot(x_ref[...].astype(BF16), w_ref[...], preferred_element_type=F32) * c_ref[...]
    for h in range(heads_per_tile):
        o_ref[h] = acc[:, h * LANES:(h + 1) * LANES].astype(o_ref.dtype)


def proj_heads(x, w, col_scale, *, heads_per_tile, bm=512):
    T, D = x.shape
    H = w.shape[1] // LANES
    bn = heads_per_tile * LANES
    grid = (H // heads_per_tile, T // bm)
    return pl.pallas_call(
        functools.partial(_proj_heads_kernel, heads_per_tile=heads_per_tile),
        grid=grid,
        in_specs=[pl.BlockSpec((bm, D), lambda j, i: (i, 0)),
                  pl.BlockSpec((D, bn), lambda j, i: (0, j)),
                  pl.BlockSpec((1, bn), lambda j, i: (0, j))],
        out_specs=pl.BlockSpec((heads_per_tile, bm, LANES), lambda j, i: (j, i, 0)),
        out_shape=jax.ShapeDtypeStruct((H, T, LANES), BF16),
        compiler_params=_params(2),
        name="proj_heads",
    )(x, w, col_scale)


def _resident(shape):
    return pl.BlockSpec(shape, lambda *_: (0,) * len(shape), pipeline_mode=pl.Buffered(1))


def _proj_norm_rope_kernel(x_ref, w_ref, g_ref, cos_ref, sin_ref, o_ref):
    acc = jnp.dot(x_ref[...].astype(BF16), w_ref[...], preferred_element_type=F32)
    cos = cos_ref[...]
    sin = sin_ref[...]
    for h in range(o_ref.shape[0]):
        t = acc[:, h * LANES:(h + 1) * LANES]
        ms = jnp.mean(t * t, axis=-1, keepdims=True)
        y = t * lax.rsqrt(ms + RMS_EPS) * g_ref[:, h * LANES:(h + 1) * LANES]
        o_ref[h] = (y * cos + _rot_half32(y) * sin).astype(o_ref.dtype)


def proj_norm_rope(x, w, gain, cos, sin, *, seq, bm=512):
    T, D = x.shape
    H = w.shape[1] // LANES
    nsb = seq // bm
    return pl.pallas_call(
        _proj_norm_rope_kernel,
        grid=(T // bm,),
        in_specs=[pl.BlockSpec((bm, D), lambda i: (i, 0)),
                  _resident(w.shape), _resident(gain.shape),
                  pl.BlockSpec((bm, LANES), lambda i: (i % nsb, 0)),
                  pl.BlockSpec((bm, LANES), lambda i: (i % nsb, 0))],
        out_specs=pl.BlockSpec((H, bm, LANES), lambda i: (0, i, 0)),
        out_shape=jax.ShapeDtypeStruct((H, T, LANES), BF16),
        compiler_params=_params(1),
        name="proj_norm_rope",
    )(x, w, gain, cos, sin)


def _mla_proj_kernel(x_ref, wc_ref, gq_ref, gkv_ref, wq_ref, wkv_ref, cos_ref, sin_ref,
                     q_ref, k_ref, v_ref, *, scale):
    c = jnp.dot(x_ref[...], wc_ref[...], preferred_element_type=F32)
    cos = cos_ref[...]
    sin = sin_ref[...]

    def rms(t, g):
        ms = jnp.mean(t * t, axis=-1, keepdims=True)
        return t * lax.rsqrt(ms + RMS_EPS) * g

    cq = rms(c[:, :D_Q_RANK], gq_ref[...]).astype(BF16)
    ckv = rms(c[:, D_Q_RANK:D_Q_RANK + D_KV_RANK], gkv_ref[...]).astype(BF16)
    kr = c[:, D_Q_RANK + D_KV_RANK:]
    kr = (kr * cos + _rot_half32(kr) * sin).astype(k_ref.dtype)
    q = jnp.dot(cq, wq_ref[...], preferred_element_type=F32) * scale
    kv = jnp.dot(ckv, wkv_ref[...], preferred_element_type=F32)
    for h in range(D_HEADS):
        base = h * 2 * LANES
        q_ref[h, :, :LANES] = q[:, base:base + LANES].astype(q_ref.dtype)
        qr = q[:, base + LANES:base + 2 * LANES]
        q_ref[h, :, LANES:] = (qr * cos + _rot_half32(qr) * sin).astype(q_ref.dtype)
        k_ref[h, :, :LANES] = kv[:, base:base + LANES].astype(k_ref.dtype)
        k_ref[h, :, LANES:] = kr
        v_ref[h] = kv[:, base + LANES:base + 2 * LANES].astype(v_ref.dtype)


def mla_proj(x, wc, gq, gkv, wq, wkv, cos, sin, *, seq, scale, bm=512):
    T, D = x.shape
    nsb = seq // bm
    return pl.pallas_call(
        functools.partial(_mla_proj_kernel, scale=scale),
        grid=(T // bm,),
        in_specs=[pl.BlockSpec((bm, D), lambda i: (i, 0)),
                  _resident(wc.shape), _resident(gq.shape), _resident(gkv.shape), _resident(wq.shape),
                  _resident(wkv.shape),
                  pl.BlockSpec((bm, LANES), lambda i: (i % nsb, 0)),
                  pl.BlockSpec((bm, LANES), lambda i: (i % nsb, 0))],
        out_specs=[pl.BlockSpec((D_HEADS, bm, 2 * LANES), lambda i: (0, i, 0)),
                   pl.BlockSpec((D_HEADS, bm, 2 * LANES), lambda i: (0, i, 0)),
                   pl.BlockSpec((D_HEADS, bm, LANES), lambda i: (0, i, 0))],
        out_shape=[jax.ShapeDtypeStruct((D_HEADS, T, 2 * LANES), BF16),
                   jax.ShapeDtypeStruct((D_HEADS, T, 2 * LANES), BF16),
                   jax.ShapeDtypeStruct((D_HEADS, T, LANES), BF16)],
        compiler_params=_params(1),
        name="mla_proj",
    )(x, wc, gq, gkv, wq, wkv, cos, sin)


DENSE_TQ = 512
ATTN_TQ = 256


def _attn_pipeline(q_ref, k_ref, v_ref, o_ref, vext_ref, s_buf, m_buf, p_buf, *, heads, group, win, win_start, bias):
    tq = s_buf[0].shape[0]
    S = q_ref.shape[1]
    nblk = S // tq
    total = heads * nblk
    vext_ref[:, :, :LANES] = v_ref[...]
    vext_ref[:, :, LANES:] = jnp.ones(v_ref.shape, vext_ref.dtype)

    def locate(n):
        h, blk = (0, n) if heads == 1 else (n // nblk, n % nblk)
        rows = pl.ds(blk * tq, tq) if isinstance(blk, int) else pl.ds(pl.multiple_of(blk * tq, tq), tq)
        keys = slice(None) if win == S else pl.ds(win_start(blk), win)
        return h, blk, rows, keys

    def scores(n, par):
        h, blk, rows, keys = locate(n)
        s = lax.dot_general(q_ref[h, rows, :], k_ref[h // group, keys, :], (((1,), (1,)), ((), ())),
                            preferred_element_type=F32)
        if bias is not None:
            s = s + bias(h, blk)
        s_buf[par][...] = s
        m_buf[par][...] = jnp.max(s, axis=-1, keepdims=True)

    def exps(par):
        p_buf[par][...] = jnp.exp2(s_buf[par][...] - m_buf[par][...]).astype(BF16)

    def wsum(n, par):
        h, blk, rows, keys = locate(n)
        o_ext = jnp.dot(p_buf[par][...], vext_ref[h // group, keys, :], preferred_element_type=F32)
        o_ref[h, rows, :] = (o_ext[:, :LANES] / o_ext[:, LANES:]).astype(o_ref.dtype)

    scores(0, 0)
    exps(0)
    scores(1, 1)

    def step(j, carry):
        n = 2 * j + 1
        exps(1)
        wsum(n - 1, 0)
        scores(n + 1, 0)
        exps(0)
        wsum(n, 1)
        scores(n + 2, 1)
        return carry

    lax.fori_loop(0, total // 2 - 1, step, 0)
    exps(1)
    wsum(total - 2, 0)
    wsum(total - 1, 1)


def _attn_scratch(kv_heads, seq, win, tq):
    return [pltpu.VMEM((kv_heads, seq, 2 * LANES), BF16),
            pltpu.VMEM((tq, win), F32), pltpu.VMEM((tq, win), F32),
            pltpu.VMEM((tq, 1), F32), pltpu.VMEM((tq, 1), F32),
            pltpu.VMEM((tq, win), BF16), pltpu.VMEM((tq, win), BF16)]


def _dense_attn_kernel(q_ref, k_ref, v_ref, o_ref, vext_ref, s0, s1, m0, m1, p0, p1, *, heads, group):
    _attn_pipeline(q_ref, k_ref, v_ref, o_ref, vext_ref, (s0, s1), (m0, m1), (p0, p1), heads=heads, group=group,
                   win=q_ref.shape[1], win_start=None, bias=None)


def dense_attention(q, k, v, *, n_heads, heads, k0, v0, batch, seq, group):
    T, dq = q.shape[1:]
    kvh = heads // group
    assert n_heads % heads == 0 and heads % group == 0 and k0 % kvh == 0 and v0 % kvh == 0
    return pl.pallas_call(
        functools.partial(_dense_attn_kernel, heads=heads, group=group),
        grid=(batch, n_heads // heads),
        in_specs=[pl.BlockSpec((heads, seq, dq), lambda b, g: (g, b, 0)),
                  pl.BlockSpec((kvh, seq, dq), lambda b, g: (k0 // kvh + g, b, 0)),
                  pl.BlockSpec((kvh, seq, LANES), lambda b, g: (v0 // kvh + g, b, 0))],
        out_specs=pl.BlockSpec((heads, seq, LANES), lambda b, g: (g, b, 0)),
        out_shape=jax.ShapeDtypeStruct((n_heads, T, LANES), BF16),
        scratch_shapes=_attn_scratch(kvh, seq, seq, DENSE_TQ),
        compiler_params=_params(2),
        name="dense_attention",
    )(q, k, v)


DIL_REACH = max(w // 2 for w, _ in B_BRANCHES)
DIL_WIN = 2 * DIL_REACH + ATTN_TQ
DIL_TABLE_TILES = (DIL_WIN + 2 * DIL_REACH) // LANES


def _dilated_attn_kernel(q_ref, k_ref, v_ref, bias_ref, o_ref, vext_ref, s0, s1, m0, m1, p0, p1, *, heads):
    S = q_ref.shape[1]

    def win_start(blk):
        return pl.multiple_of(jnp.clip(blk * ATTN_TQ - DIL_REACH, 0, S - DIL_WIN), ATTN_TQ)

    def bias(h, blk):
        tile0 = (win_start(blk) - blk * ATTN_TQ + 2 * DIL_REACH) // LANES
        return jnp.concatenate([bias_ref[h, tile0 + t] for t in range(DIL_WIN // LANES)], axis=1)

    _attn_pipeline(q_ref, k_ref, v_ref, o_ref, vext_ref, (s0, s1), (m0, m1), (p0, p1), heads=heads, group=1,
                   win=DIL_WIN, win_start=win_start, bias=bias)


def dilated_bias_table(n_heads):
    col = (np.arange(DIL_TABLE_TILES)[:, None, None] * LANES + np.arange(LANES)[None, None, :])
    d = np.abs(col - np.arange(ATTN_TQ)[None, :, None] - 2 * DIL_REACH)
    mult = np.zeros(d.shape, np.float64)
    for window, dil in B_BRANCHES:
        mult += ((d % dil == 0) & (d <= window // 2)).astype(np.float64)
    with np.errstate(divide="ignore"):
        logm = np.log(mult)
    slopes = jnp.asarray(2.0 ** (-8.0 * np.arange(1, n_heads + 1) / n_heads), dtype=F32)
    bias = jnp.asarray(logm, dtype=F32)[None] - slopes[:, None, None, None] * jnp.asarray(d, dtype=F32)[None]
    return bias * LOG2E


def dilated_attention(qkv, bias, *, n_heads, heads, q0, k0, v0, batch, seq):
    T = qkv.shape[1]
    assert n_heads % heads == 0 and q0 % heads == 0 and k0 % heads == 0 and v0 % heads == 0
    qkv_spec = lambda h0: pl.BlockSpec((heads, seq, LANES), lambda b, g: (h0 // heads + g, b, 0))
    return pl.pallas_call(
        functools.partial(_dilated_attn_kernel, heads=heads),
        grid=(batch, n_heads // heads),
        in_specs=[qkv_spec(q0), qkv_spec(k0), qkv_spec(v0),
                  pl.BlockSpec((heads, DIL_TABLE_TILES, ATTN_TQ, LANES), lambda b, g: (g, 0, 0, 0))],
        out_specs=pl.BlockSpec((heads, seq, LANES), lambda b, g: (g, b, 0)),
        out_shape=jax.ShapeDtypeStruct((n_heads, T, LANES), BF16),
        scratch_shapes=_attn_scratch(heads, seq, DIL_WIN, ATTN_TQ),
        compiler_params=_params(2),
        name="dilated_attention",
    )(qkv, qkv, qkv, bias)


NA_QROWS = ATTN_TQ // GRID_W
NA_KROWS = NA_QROWS + NA_ROWS
NA_WIN = NA_KROWS * GRID_W


def _na_window_row(blk, rows):
    lo = blk * NA_QROWS - NA_ROWS // 2
    return jnp.clip(lo, 0, rows - NA_KROWS) if not isinstance(blk, int) else min(max(lo, 0), rows - NA_KROWS)


def _na_attn_kernel(q_ref, k_ref, v_ref, bias_ref, o_ref, vext_ref, s0, s1, m0, m1, p0, p1, *, heads, rows):
    nblk = rows // NA_QROWS

    def win_start(blk):
        return pl.multiple_of(_na_window_row(blk, rows) * GRID_W, GRID_W)

    def bias(h, blk):
        cls = (blk > 0).astype(jnp.int32) + (blk == nblk - 1).astype(jnp.int32) if not isinstance(blk, int) \
            else int(blk > 0) + int(blk == nblk - 1)
        return bias_ref[h, cls]

    _attn_pipeline(q_ref, k_ref, v_ref, o_ref, vext_ref, (s0, s1), (m0, m1), (p0, p1), heads=heads, group=1,
                   win=NA_WIN, win_start=win_start, bias=bias)


def na_bias_table(rpb, rows):
    H = rpb.shape[0]
    col = np.arange(GRID_W)
    c0 = np.clip(col - NA_COLS // 2, 0, GRID_W - NA_COLS)
    col_ok = (col[None, :] >= c0[:, None]) & (col[None, :] < c0[:, None] + NA_COLS)
    dcol = np.clip(col[None, :] - col[:, None] + NA_COLS - 1, 0, 2 * NA_COLS - 2)
    pick = ((dcol[:, :, None] == np.arange(2 * NA_COLS - 1)) & col_ok[:, :, None]).astype(np.float32)
    t = jnp.einsum("hrd,qkd->hrqk", rpb.astype(F32), jnp.asarray(pick), precision=lax.Precision.HIGHEST)
    t = jnp.where(jnp.asarray(col_ok)[None, None], t * LOG2E, -jnp.inf)
    nblk = rows // NA_QROWS

    def geometry(blk):
        r = blk * NA_QROWS + np.arange(NA_QROWS)
        kr = _na_window_row(blk, rows) + np.arange(NA_KROWS)
        r0 = np.clip(r - NA_ROWS // 2, 0, rows - NA_ROWS)
        valid = (kr[None, :] >= r0[:, None]) & (kr[None, :] < r0[:, None] + NA_ROWS)
        drow = np.clip(kr[None, :] - r[:, None] + NA_ROWS - 1, 0, 2 * NA_ROWS - 2)
        return valid, np.where(valid, drow, 0)

    inner = geometry(1)
    assert all(np.array_equal(a, b) for blk in range(1, nblk - 1) for a, b in zip(geometry(blk), inner))
    tables = []
    for blk in (0, 1, nblk - 1):
        valid, drow = geometry(blk)
        slab = jnp.where(jnp.asarray(valid)[None, :, :, None, None], t[:, drow], -jnp.inf)
        tables.append(slab.transpose(0, 1, 3, 2, 4).reshape(H, ATTN_TQ, NA_WIN))
    return jnp.stack(tables, axis=1)


def na_attention(qkv, bias, *, n_heads, heads, q0, k0, v0, batch, seq):
    T = qkv.shape[1]
    rows = seq // GRID_W
    assert n_heads % heads == 0 and q0 % heads == 0 and k0 % heads == 0 and v0 % heads == 0
    assert rows % NA_QROWS == 0 and rows // NA_QROWS >= 3
    qkv_spec = lambda h0: pl.BlockSpec((heads, seq, LANES), lambda b, g: (h0 // heads + g, b, 0))
    return pl.pallas_call(
        functools.partial(_na_attn_kernel, heads=heads, rows=rows),
        grid=(batch, n_heads // heads),
        in_specs=[qkv_spec(q0), qkv_spec(k0), qkv_spec(v0),
                  pl.BlockSpec((heads, 3, ATTN_TQ, NA_WIN), lambda b, g: (g, 0, 0, 0))],
        out_specs=pl.BlockSpec((heads, seq, LANES), lambda b, g: (g, b, 0)),
        out_shape=jax.ShapeDtypeStruct((n_heads, T, LANES), BF16),
        scratch_shapes=_attn_scratch(heads, seq, NA_WIN, ATTN_TQ),
        compiler_params=_params(2),
        name="na_attention",
    )(qkv, qkv, qkv, bias)


def _layer_norm(z, g, b):
    mu = jnp.mean(z, axis=-1, keepdims=True)
    zc = z - mu
    var = jnp.mean(zc * zc, axis=-1, keepdims=True)
    return zc * lax.rsqrt(var + LN_EPS) * g + b


def _top2_of4(v0, v1, v2, v3):
    a = jnp.maximum(v0, v1)
    b = jnp.minimum(v0, v1)
    c = jnp.maximum(v2, v3)
    d = jnp.minimum(v2, v3)
    return jnp.maximum(a, c), jnp.maximum(jnp.minimum(a, c), jnp.maximum(b, d))


def _route(logits_t, rb):
    G = N_GROUPS
    scores = jax.nn.sigmoid(logits_t)
    sel = scores + rb
    sj = [sel[j * G:(j + 1) * G] for j in range(EXPERTS_PER_GROUP)]
    cj = [scores[j * G:(j + 1) * G] for j in range(EXPERTS_PER_GROUP)]
    t1, t2 = _top2_of4(*sj)
    grp = t1 + t2
    gi = lax.broadcasted_iota(jnp.int32, grp.shape, 0).astype(F32)
    gbest = jnp.min(jnp.where(grp == jnp.max(grp, axis=0, keepdims=True), gi, float(G)), axis=0, keepdims=True)
    pick = gi == gbest
    v = [jnp.sum(jnp.where(pick, s, 0.0), axis=0, keepdims=True) for s in sj]
    c = [jnp.sum(jnp.where(pick, s, 0.0), axis=0, keepdims=True) for s in cj]
    neg = jnp.float32(-jnp.inf)
    m1 = jnp.maximum(jnp.maximum(v[0], v[1]), jnp.maximum(v[2], v[3]))
    j1 = jnp.where(v[0] == m1, 0, jnp.where(v[1] == m1, 1, jnp.where(v[2] == m1, 2, 3)))
    w = [jnp.where(j1 == j, neg, v[j]) for j in range(4)]
    m2 = jnp.maximum(jnp.maximum(w[0], w[1]), jnp.maximum(w[2], w[3]))
    j2 = jnp.where(w[0] == m2, 0, jnp.where(w[1] == m2, 1, jnp.where(w[2] == m2, 2, 3)))
    g1 = jnp.where(j1 == 0, c[0], jnp.where(j1 == 1, c[1], jnp.where(j1 == 2, c[2], c[3])))
    g2 = jnp.where(j2 == 0, c[0], jnp.where(j2 == 1, c[1], jnp.where(j2 == 2, c[2], c[3])))
    tot = g1 + g2
    e0 = gbest.astype(jnp.int32) * EXPERTS_PER_GROUP
    return (e0 + j1, e0 + j2), (g1 / tot, g2 / tot)


OUT_SUBTILES = 2


def _out_ln_kernel(a_ref, b_ref, w_ref, x_ref, g_ref, beta_ref, x1_ref, x1p_ref):
    sub = x_ref.shape[0] // OUT_SUBTILES
    half = x_ref.shape[1] // 2
    for t in range(OUT_SUBTILES):
        rows = slice(t * sub, (t + 1) * sub)
        heads = [a_ref[h, rows, :] for h in range(a_ref.shape[0])] + [b_ref[h, rows, :] for h in range(b_ref.shape[0])]
        mix = jnp.dot(jnp.concatenate(heads, axis=1), w_ref[...], preferred_element_type=F32)
        x1 = _layer_norm(ALPHA * x_ref[rows, :] + mix, g_ref[...], beta_ref[...])
        x1_ref[rows, :] = x1
        x1p_ref[rows, :] = _pack_bf16_pair(x1[:, :half], x1[:, half:])


def out_ln(a, b, w, x, g, beta, *, bm=512):
    T, D = x.shape
    return pl.pallas_call(
        _out_ln_kernel,
        grid=(T // bm,),
        in_specs=[pl.BlockSpec((a.shape[0], bm, LANES), lambda i: (0, i, 0)),
                  pl.BlockSpec((b.shape[0], bm, LANES), lambda i: (0, i, 0)),
                  _resident(w.shape),
                  pl.BlockSpec((bm, D), lambda i: (i, 0)),
                  _resident(g.shape), _resident(beta.shape)],
        out_specs=[pl.BlockSpec((bm, D), lambda i: (i, 0)), pl.BlockSpec((bm, D // 2), lambda i: (i, 0))],
        out_shape=[jax.ShapeDtypeStruct((T, D), F32), jax.ShapeDtypeStruct((T, D // 2), jnp.uint32)],
        compiler_params=_params(1),
        name="out_ln",
    )(a, b, w, x, g, beta)


def _route_kernel(x_ref, rwh_ref, rwl_ref, rb_ref, eid_ref, gate_ref):
    x = x_ref[...]
    hi = x.astype(BF16)
    lo = (x - hi.astype(F32)).astype(BF16)
    nt = (((1,), (1,)), ((), ()))
    logits_t = (lax.dot_general(rwh_ref[...], hi, nt, preferred_element_type=F32)
                + lax.dot_general(rwl_ref[...], hi, nt, preferred_element_type=F32)
                + lax.dot_general(rwh_ref[...], lo, nt, preferred_element_type=F32))
    eid, gate = _route(logits_t, rb_ref[...])
    for k in range(TOP_K):
        eid_ref[k:k + 1, :] = eid[k]
        gate_ref[k:k + 1, :] = gate[k]


def route(x1, rw_hi, rw_lo, rb, *, bm=1024):
    T, D = x1.shape
    return pl.pallas_call(
        _route_kernel,
        grid=(T // bm,),
        in_specs=[pl.BlockSpec((bm, D), lambda i: (i, 0)),
                  _resident(rw_hi.shape), _resident(rw_lo.shape), _resident(rb.shape)],
        out_specs=[pl.BlockSpec((TOP_K, bm), lambda i: (0, i)),
                   pl.BlockSpec((TOP_K, bm), lambda i: (0, i))],
        out_shape=[jax.ShapeDtypeStruct((TOP_K, T), jnp.int32),
                   jax.ShapeDtypeStruct((TOP_K, T), F32)],
        compiler_params=_params(1),
        name="route",
    )(x1, rw_hi, rw_lo, rb)


GATHER_DEPTH = 3


def _row_copy(src_hbm, row, dst_vmem, j, sem):
    return pltpu.make_async_copy(src_hbm.at[pl.ds(row, 1), :], dst_vmem.at[pl.ds(j, 1), :], sem)


def _gather_ring_step(start_gather, wait_gather, compute, active=None, idle=None):
    i = pl.program_id(0)
    n_steps = pl.num_programs(0)
    ahead = GATHER_DEPTH - 1
    slot = i % GATHER_DEPTH

    @pl.when(i == 0)
    def _():
        for a in range(ahead):
            start_gather(a, a)

    wait_gather(slot)
    more = i + ahead < n_steps
    cases = [(more, compute)] if active is None else [(more & active, compute), (more & ~active, idle)]
    for cond, work in cases:
        @pl.when(cond)
        def _(work=work):
            start_gather(i + ahead, (i + ahead) % GATHER_DEPTH)
            work(slot)
    cases = [(~more, compute)] if active is None else [(~more & active, compute), (~more & ~active, idle)]
    for cond, work in cases:
        @pl.when(cond)
        def _(work=work):
            work(slot)


def _pack_bf16_pair(lo, hi):
    lo_bits = pltpu.bitcast(lo.astype(BF16).astype(F32), jnp.uint32)
    hi_bits = pltpu.bitcast(hi.astype(BF16).astype(F32), jnp.uint32)
    return lax.shift_right_logical(lo_bits, jnp.uint32(16)) | (hi_bits & jnp.uint32(0xFFFF0000))


def _unpack_bf16_pair(word):
    lo = pltpu.bitcast(lax.shift_left(word, jnp.uint32(16)), F32)
    hi = pltpu.bitcast(word & jnp.uint32(0xFFFF0000), F32)
    return lo, hi


def _moe_ffn_kernel(blk_lo_ref, blk_hi_ref, nblk_ref, tok_ref, x_hbm,
                    wg_lo, wu_lo, wd_lo, wg_hi, wu_hi, wd_hi, y_ref, xbuf, sem):
    def start_gather(blk, s):
        base = blk * MOE_BLOCK
        for j in range(MOE_BLOCK):
            _row_copy(x_hbm, tok_ref[base + j], xbuf.at[s], j, sem.at[s]).start()

    def wait_gather(s):
        pltpu.make_async_copy(x_hbm.at[pl.ds(0, MOE_BLOCK), :], xbuf.at[s], sem.at[s]).wait()

    def ffn(s):
        x_left, x_right = _unpack_bf16_pair(xbuf[s])
        xb = jnp.concatenate([x_left.astype(BF16), x_right.astype(BF16)], axis=1)
        ys = []
        for wg, wu, wd in ((wg_lo, wu_lo, wd_lo), (wg_hi, wu_hi, wd_hi)):
            hg = jnp.dot(xb, wg[...], preferred_element_type=F32)
            hu = jnp.dot(xb, wu[...], preferred_element_type=F32)
            hb = (jax.nn.silu(hg) * hu).astype(BF16)
            ys.append(jnp.dot(hb, wd[...], preferred_element_type=F32))
        y_ref[...] = _pack_bf16_pair(ys[0], ys[1])

    def no_tokens(s):
        y_ref[...] = jnp.zeros(y_ref.shape, y_ref.dtype)

    _gather_ring_step(start_gather, wait_gather, ffn, active=pl.program_id(0) < nblk_ref[0], idle=no_tokens)


def moe_ffn(x1p, w_gate, w_up, w_down, blk_lo, blk_hi, nblk, row_tok):
    D = 2 * x1p.shape[1]
    n_blocks = blk_lo.shape[0]
    assert n_blocks >= GATHER_DEPTH
    n_rows = n_blocks * MOE_BLOCK
    w_in = lambda which: pl.BlockSpec((None, D, D_EXPERT), lambda i, lo, hi, nb, tk: ((lo, hi)[which][i], 0, 0))
    w_out = lambda which: pl.BlockSpec((None, D_EXPERT, D), lambda i, lo, hi, nb, tk: ((lo, hi)[which][i], 0, 0))
    grid_spec = pltpu.PrefetchScalarGridSpec(
        num_scalar_prefetch=4,
        grid=(n_blocks,),
        in_specs=[pl.BlockSpec(memory_space=pl.ANY), w_in(0), w_in(0), w_out(0), w_in(1), w_in(1), w_out(1)],
        out_specs=pl.BlockSpec((MOE_BLOCK, D), lambda i, lo, hi, nb, tk: (i, 0)),
        scratch_shapes=[pltpu.VMEM((GATHER_DEPTH, MOE_BLOCK, D // 2), jnp.uint32),
                        pltpu.SemaphoreType.DMA((GATHER_DEPTH,))],
    )
    return pl.pallas_call(
        _moe_ffn_kernel,
        grid_spec=grid_spec,
        out_shape=jax.ShapeDtypeStruct((n_rows, D), jnp.uint32),
        compiler_params=_params(1),
        name="moe_ffn",
    )(blk_lo, blk_hi, nblk, row_tok, x1p, w_gate, w_up, w_down, w_gate, w_up, w_down)


def _combine_ln_kernel(dest_ref, y_hbm, x1_ref, gate_ref, g_ref, beta_ref, x2_ref, x2b_ref, ybuf, sem, *, bm):
    def start_gather(tile, s):
        base = tile * bm
        for j in range(bm):
            _row_copy(y_hbm, dest_ref[base + j], ybuf.at[s], j, sem.at[s]).start()

    def wait_gather(s):
        pltpu.make_async_copy(y_hbm.at[pl.ds(0, bm), :], ybuf.at[s], sem.at[s]).wait()

    def finish(s):
        gate = gate_ref[...]
        y_lo, y_hi = _unpack_bf16_pair(ybuf[s])
        moe = y_lo * gate[:, 0:1] + y_hi * gate[:, 1:2]
        x2 = _layer_norm(ALPHA * x1_ref[...] + moe, g_ref[...], beta_ref[...])
        x2_ref[...] = x2
        x2b_ref[...] = x2.astype(BF16)

    _gather_ring_step(start_gather, wait_gather, finish)


def combine_ln(y_rows, dest, x1, gate_t, g, beta, *, bm=256):
    T, D = x1.shape
    assert T // bm >= GATHER_DEPTH
    grid_spec = pltpu.PrefetchScalarGridSpec(
        num_scalar_prefetch=1,
        grid=(T // bm,),
        in_specs=[pl.BlockSpec(memory_space=pl.ANY),
                  pl.BlockSpec((bm, D), lambda i, d: (i, 0)),
                  pl.BlockSpec((bm, TOP_K), lambda i, d: (i, 0)),
                  pl.BlockSpec((1, D), lambda i, d: (0, 0)),
                  pl.BlockSpec((1, D), lambda i, d: (0, 0))],
        out_specs=[pl.BlockSpec((bm, D), lambda i, d: (i, 0)),
                   pl.BlockSpec((bm, D), lambda i, d: (i, 0))],
        scratch_shapes=[pltpu.VMEM((GATHER_DEPTH, bm, D), jnp.uint32), pltpu.SemaphoreType.DMA((GATHER_DEPTH,))],
    )
    return pl.pallas_call(
        functools.partial(_combine_ln_kernel, bm=bm),
        grid_spec=grid_spec,
        out_shape=[jax.ShapeDtypeStruct((T, D), F32), jax.ShapeDtypeStruct((T, D), BF16)],
        compiler_params=_params(1),
        name="combine_ln",
    )(dest, y_rows, x1, gate_t, g, beta)


_PAIR_LO = np.array([a for a in range(EXPERTS_PER_GROUP) for b in range(a + 1, EXPERTS_PER_GROUP)], np.int32)
_PAIR_HI = np.array([b for a in range(EXPERTS_PER_GROUP) for b in range(a + 1, EXPERTS_PER_GROUP)], np.int32)
N_PAIRS = len(_PAIR_LO)
N_CLASSES = N_GROUPS * N_PAIRS


def moe_plan(eid, gate):
    T = eid.shape[1]
    n_blocks = (T + N_CLASSES * (MOE_BLOCK - 1) + MOE_BLOCK - 1) // MOE_BLOCK
    first_lower = eid[0] < eid[1]
    e_lo = jnp.minimum(eid[0], eid[1])
    e_hi = jnp.maximum(eid[0], eid[1])
    gate_t = jnp.stack([jnp.where(first_lower, gate[0], gate[1]), jnp.where(first_lower, gate[1], gate[0])], axis=1)
    a = e_lo % EXPERTS_PER_GROUP
    b = e_hi % EXPERTS_PER_GROUP
    pair = a * (2 * EXPERTS_PER_GROUP - 1 - a) // 2 + (b - a - 1)
    cls = (e_lo // EXPERTS_PER_GROUP) * N_PAIRS + pair
    onehot = (cls[:, None] == jnp.arange(N_CLASSES, dtype=jnp.int32)[None, :]).astype(jnp.int32)
    csum = jnp.cumsum(onehot, axis=0)
    rank = jnp.sum(csum * onehot, axis=1) - 1
    counts = csum[-1]
    pcounts = (counts + MOE_BLOCK - 1) // MOE_BLOCK * MOE_BLOCK
    pends = jnp.cumsum(pcounts)
    pstarts = pends - pcounts
    dest = (pstarts[cls] + rank).astype(jnp.int32)
    row_tok = jnp.zeros((n_blocks * MOE_BLOCK,), jnp.int32).at[dest].set(jnp.arange(T, dtype=jnp.int32))
    blk_start = jnp.arange(n_blocks, dtype=jnp.int32) * MOE_BLOCK
    blk_cls = jnp.minimum(jnp.sum((pends[None, :] <= blk_start[:, None]).astype(jnp.int32), axis=1), N_CLASSES - 1)
    blk_base = (blk_cls // N_PAIRS) * EXPERTS_PER_GROUP
    blk_lo = (blk_base + jnp.asarray(_PAIR_LO)[blk_cls % N_PAIRS]).astype(jnp.int32)
    blk_hi = (blk_base + jnp.asarray(_PAIR_HI)[blk_cls % N_PAIRS]).astype(jnp.int32)
    nblk = (pends[-1:] // MOE_BLOCK).astype(jnp.int32)
    return dest, row_tok, blk_lo, blk_hi, nblk, gate_t


def moe_layer(x1, x1p, eid, gate, w_gate, w_up, w_down, ln_g, ln_b):
    dest, row_tok, blk_lo, blk_hi, nblk, gate_t = moe_plan(eid, gate)
    y_rows = moe_ffn(x1p, w_gate.astype(BF16), w_up.astype(BF16), w_down.astype(BF16), blk_lo, blk_hi, nblk, row_tok)
    return combine_ln(y_rows, dest, x1, gate_t, ln_g.reshape(1, -1), ln_b.reshape(1, -1))


def _rope_cos_sin(pos, dim):
    inv_freq = ROPE_THETA ** (-jnp.arange(0, dim, 2, dtype=F32) / dim)
    ang = pos.astype(F32)[:, None] * inv_freq[None, :]
    cos = jnp.cos(ang)
    sin = jnp.sin(ang)
    return jnp.concatenate([cos, cos], axis=-1), jnp.concatenate([-sin, sin], axis=-1)


def axial_rope_tables(seq):
    pos = jnp.arange(seq)
    half = HEAD_DIM // 2
    cr, sr = _rope_cos_sin(pos // GRID_W, half)
    cc, sc = _rope_cos_sin(pos % GRID_W, half)
    return jnp.concatenate([cr, cc], axis=-1), jnp.concatenate([sr, sc], axis=-1)


def mla_rope_tables(seq):
    c, s = _rope_cos_sin(jnp.arange(seq), D_ROPE)
    z = jnp.zeros((seq, LANES - D_ROPE), F32)
    return jnp.concatenate([c, z], axis=-1), jnp.concatenate([s, z], axis=-1)


def router_layout(router_w, router_b):
    perm = np.array([g * EXPERTS_PER_GROUP + j for j in range(EXPERTS_PER_GROUP) for g in range(N_GROUPS)])
    rw = router_w.astype(F32).T[perm]
    hi = rw.astype(BF16)
    lo = (rw - hi.astype(F32)).astype(BF16)
    return hi, lo, router_b.astype(F32)[perm].reshape(N_EXPERTS, 1)


def mla_up_layout(w_q_up, w_kv_up, w_in_tail):
    rq = w_q_up.shape[0]
    wq = w_q_up.reshape(rq, D_HEADS, D_NOPE + D_ROPE)
    wq = jnp.pad(wq, ((0, 0), (0, 0), (0, 2 * LANES - (D_NOPE + D_ROPE)))).reshape(rq, D_HEADS * 2 * LANES)
    wc = jnp.pad(w_in_tail, ((0, 0), (0, LANES - D_ROPE)))
    return wq.astype(BF16), w_kv_up.astype(BF16), wc.astype(BF16)


def kernel(x, router_w, router_b, l0_w_in, l0_a_q_norm, l0_a_k_norm, l0_w_out, l0_ln1_g, l0_ln1_b, l0_w_gate, l0_w_up, l0_w_down, l0_ln2_g, l0_ln2_b, l1_w_in, l1_c_rpb, l1_d_q_norm, l1_d_w_q_up, l1_d_kv_norm, l1_d_w_kv_up, l1_w_out, l1_ln1_g, l1_ln1_b, l1_w_gate, l1_w_up, l1_w_down, l1_ln2_g, l1_ln2_b):
    batch, seq, d = x.shape
    T = batch * seq
    xf = x.reshape(T, d)
    rw_hi, rw_lo, rb = router_layout(router_w, router_b)
    row = lambda v: v.astype(F32).reshape(1, -1)

    a_q_dim = A_HEADS * HEAD_DIM
    a_qk_dim = a_q_dim + A_KV_HEADS * HEAD_DIM
    w0 = l0_w_in.astype(BF16)
    qk_scale = LOG2E * HEAD_DIM ** -0.5
    gain = jnp.concatenate([jnp.tile(l0_a_q_norm.astype(F32), A_HEADS) * qk_scale,
                            jnp.tile(l0_a_k_norm.astype(F32), A_KV_HEADS)]).reshape(1, -1)
    cos_a, sin_a = axial_rope_tables(seq)
    qk_a = proj_norm_rope(xf, w0[:, :a_qk_dim], gain, cos_a, sin_a, seq=seq)
    nb = A_KV_HEADS
    q_scale = jnp.full((B_HEADS * HEAD_DIM,), qk_scale, F32)
    scale0 = jnp.concatenate([jnp.ones((nb * HEAD_DIM,), F32), q_scale, jnp.ones((2 * B_HEADS * HEAD_DIM,), F32)])
    rest0 = proj_heads(xf, w0[:, a_qk_dim:], scale0.reshape(1, -1), heads_per_tile=13)
    group = A_HEADS // A_KV_HEADS
    oa = dense_attention(qk_a, qk_a, rest0, n_heads=A_HEADS, heads=group, k0=A_HEADS, v0=0, batch=batch, seq=seq,
                         group=group)
    ob = dilated_attention(rest0, dilated_bias_table(B_HEADS), n_heads=B_HEADS, heads=1, q0=nb, k0=nb + B_HEADS,
                           v0=nb + 2 * B_HEADS, batch=batch, seq=seq)
    x1, x1p = out_ln(oa, ob, l0_w_out.astype(BF16), xf, row(l0_ln1_g), row(l0_ln1_b))
    eid, gate = route(x1, rw_hi, rw_lo, rb)
    x2, x2b = moe_layer(x1, x1p, eid, gate, l0_w_gate, l0_w_up, l0_w_down, l0_ln2_g, l0_ln2_b)

    c_dim = C_HEADS * HEAD_DIM
    w1 = l1_w_in
    scale1 = jnp.concatenate([jnp.full((c_dim,), qk_scale, F32), jnp.ones((2 * c_dim,), F32)])
    qkv_c = proj_heads(x2b, w1[:, :3 * c_dim].astype(BF16), scale1.reshape(1, -1), heads_per_tile=12)
    oc = na_attention(qkv_c, na_bias_table(l1_c_rpb, seq // GRID_W), n_heads=C_HEADS, heads=2, q0=0, k0=C_HEADS,
                      v0=2 * C_HEADS, batch=batch, seq=seq)
    wq, wkv, wc = mla_up_layout(l1_d_w_q_up, l1_d_w_kv_up, w1[:, 3 * c_dim:])
    cos_d, sin_d = mla_rope_tables(seq)
    qd, kd, vd = mla_proj(x2b, wc, row(l1_d_q_norm), row(l1_d_kv_norm), wq, wkv, cos_d, sin_d, seq=seq,
                          scale=LOG2E * (D_NOPE + D_ROPE) ** -0.5)
    od = dense_attention(qd, kd, vd, n_heads=D_HEADS, heads=1, k0=0, v0=0, batch=batch, seq=seq, group=1)
    x3, x3p = out_ln(oc, od, l1_w_out.astype(BF16), x2, row(l1_ln1_g), row(l1_ln1_b))
    eid, gate = route(x3, rw_hi, rw_lo, rb)
    x4, _ = moe_layer(x3, x3p, eid, gate, l1_w_gate, l1_w_up, l1_w_down, l1_ln2_g, l1_ln2_b)
    return x4.reshape(batch, seq, d)
```

```python
import functools
import math

import jax
import jax.numpy as jnp
import numpy as np
from jax import lax
from jax.experimental import pallas as pl
from jax.experimental.pallas import tpu as pltpu

D_MODEL = 2048
DEPTH = 2
GRID_W = 64
HEAD_DIM = 128
ROPE_THETA = 10000.0
RMS_EPS = 1e-6
LN_EPS = 1e-5

A_HEADS = 8
A_KV_HEADS = 2
B_HEADS = 8
B_BRANCHES = ((128, 1), (512, 4), (2048, 16))
C_HEADS = 8
NA_ROWS = 8
NA_COLS = 16
D_HEADS = 8
D_Q_RANK = 512
D_KV_RANK = 256
D_NOPE = 128
D_ROPE = 64
D_V = 128

N_EXPERTS = 32
N_GROUPS = 8
EXPERTS_PER_GROUP = N_EXPERTS // N_GROUPS
TOP_K = 2
D_EXPERT = 512
MOE_BLOCK = 256

ALPHA = (2.0 * DEPTH) ** 0.25
LOG2E = math.log2(math.e)

LANES = 128
VMEM_LIMIT = 56 * 1024 * 1024

BF16 = jnp.bfloat16
F32 = jnp.float32


def _params(n_axes, **flags):
    return pltpu.CompilerParams(dimension_semantics=("arbitrary",) * n_axes, vmem_limit_bytes=VMEM_LIMIT,
                                flags=flags or None)


def _rot_half32(y):
    lane = lax.broadcasted_iota(jnp.int32, y.shape, y.ndim - 1)
    up = pltpu.roll(y, 96, axis=y.ndim - 1)
    dn = pltpu.roll(y, 32, axis=y.ndim - 1)
    return jnp.where((lane % 64) < 32, up, dn)


def _proj_heads_kernel(x_ref, w_ref, c_ref, o_ref, *, heads_per_tile):
    acc = jnp.dot(x_ref[...].astype(BF16), w_ref[...], preferred_element_type=F32) * c_ref[...]
    for h in range(heads_per_tile):
        o_ref[h] = acc[:, h * LANES:(h + 1) * LANES].astype(o_ref.dtype)


def proj_heads(x, w, col_scale, *, heads_per_tile, bm=512):
    T, D = x.shape
    H = w.shape[1] // LANES
    bn = heads_per_tile * LANES
    grid = (H // heads_per_tile, T // bm)
    return pl.pallas_call(
        functools.partial(_proj_heads_kernel, heads_per_tile=heads_per_tile),
        grid=grid,
        in_specs=[pl.BlockSpec((bm, D), lambda j, i: (i, 0)),
                  pl.BlockSpec((D, bn), lambda j, i: (0, j)),
                  pl.BlockSpec((1, bn), lambda j, i: (0, j))],
        out_specs=pl.BlockSpec((heads_per_tile, bm, LANES), lambda j, i: (j, i, 0)),
        out_shape=jax.ShapeDtypeStruct((H, T, LANES), BF16),
        compiler_params=_params(2),
        name="proj_heads",
    )(x, w, col_scale)


def _resident(shape):
    return pl.BlockSpec(shape, lambda *_: (0,) * len(shape), pipeline_mode=pl.Buffered(1))


def _proj_norm_rope_kernel(x_ref, w_ref, g_ref, cos_ref, sin_ref, o_ref):
    acc = jnp.dot(x_ref[...].astype(BF16), w_ref[...], preferred_element_type=F32)
    cos = cos_ref[...]
    sin = sin_ref[...]
    for h in range(o_ref.shape[0]):
        t = acc[:, h * LANES:(h + 1) * LANES]
        ms = jnp.mean(t * t, axis=-1, keepdims=True)
        y = t * lax.rsqrt(ms + RMS_EPS) * g_ref[:, h * LANES:(h + 1) * LANES]
        o_ref[h] = (y * cos + _rot_half32(y) * sin).astype(o_ref.dtype)


def proj_norm_rope(x, w, gain, cos, sin, *, seq, bm=512):
    T, D = x.shape
    H = w.shape[1] // LANES
    nsb = seq // bm
    return pl.pallas_call(
        _proj_norm_rope_kernel,
        grid=(T // bm,),
        in_specs=[pl.BlockSpec((bm, D), lambda i: (i, 0)),
                  _resident(w.shape), _resident(gain.shape),
                  pl.BlockSpec((bm, LANES), lambda i: (i % nsb, 0)),
                  pl.BlockSpec((bm, LANES), lambda i: (i % nsb, 0))],
        out_specs=pl.BlockSpec((H, bm, LANES), lambda i: (0, i, 0)),
        out_shape=jax.ShapeDtypeStruct((H, T, LANES), BF16),
        compiler_params=_params(1),
        name="proj_norm_rope",
    )(x, w, gain, cos, sin)


def _mla_proj_kernel(x_ref, wc_ref, gq_ref, gkv_ref, wq_ref, wkv_ref, cos_ref, sin_ref,
                     q_ref, k_ref, v_ref, *, scale):
    c = jnp.dot(x_ref[...], wc_ref[...], preferred_element_type=F32)
    cos = cos_ref[...]
    sin = sin_ref[...]

    def rms(t, g):
        ms = jnp.mean(t * t, axis=-1, keepdims=True)
        return t * lax.rsqrt(ms + RMS_EPS) * g

    cq = rms(c[:, :D_Q_RANK], gq_ref[...]).astype(BF16)
    ckv = rms(c[:, D_Q_RANK:D_Q_RANK + D_KV_RANK], gkv_ref[...]).astype(BF16)
    kr = c[:, D_Q_RANK + D_KV_RANK:]
    kr = (kr * cos + _rot_half32(kr) * sin).astype(k_ref.dtype)
    q = jnp.dot(cq, wq_ref[...], preferred_element_type=F32) * scale
    kv = jnp.dot(ckv, wkv_ref[...], preferred_element_type=F32)
    for h in range(D_HEADS):
        base = h * 2 * LANES
        q_ref[h, :, :LANES] = q[:, base:base + LANES].astype(q_ref.dtype)
        qr = q[:, base + LANES:base + 2 * LANES]
        q_ref[h, :, LANES:] = (qr * cos + _rot_half32(qr) * sin).astype(q_ref.dtype)
        k_ref[h, :, :LANES] = kv[:, base:base + LANES].astype(k_ref.dtype)
        k_ref[h, :, LANES:] = kr
        v_ref[h] = kv[:, base + LANES:base + 2 * LANES].astype(v_ref.dtype)


def mla_proj(x, wc, gq, gkv, wq, wkv, cos, sin, *, seq, scale, bm=512):
    T, D = x.shape
    nsb = seq // bm
    return pl.pallas_call(
        functools.partial(_mla_proj_kernel, scale=scale),
        grid=(T // bm,),
        in_specs=[pl.BlockSpec((bm, D), lambda i: (i, 0)),
                  _resident(wc.shape), _resident(gq.shape), _resident(gkv.shape), _resident(wq.shape),
                  _resident(wkv.shape),
                  pl.BlockSpec((bm, LANES), lambda i: (i % nsb, 0)),
                  pl.BlockSpec((bm, LANES), lambda i: (i % nsb, 0))],
        out_specs=[pl.BlockSpec((D_HEADS, bm, 2 * LANES), lambda i: (0, i, 0)),
                   pl.BlockSpec((D_HEADS, bm, 2 * LANES), lambda i: (0, i, 0)),
                   pl.BlockSpec((D_HEADS, bm, LANES), lambda i: (0, i, 0))],
        out_shape=[jax.ShapeDtypeStruct((D_HEADS, T, 2 * LANES), BF16),
                   jax.ShapeDtypeStruct((D_HEADS, T, 2 * LANES), BF16),
                   jax.ShapeDtypeStruct((D_HEADS, T, LANES), BF16)],
        compiler_params=_params(1),
        name="mla_proj",
    )(x, wc, gq, gkv, wq, wkv, cos, sin)


DENSE_TQ = 512
ATTN_TQ = 256


def _attn_pipeline(q_ref, k_ref, v_ref, o_ref, vext_ref, s_buf, m_buf, p_buf, *, heads, group, win, win_start, bias):
    tq = s_buf[0].shape[0]
    S = q_ref.shape[1]
    nblk = S // tq
    total = heads * nblk
    vext_ref[:, :, :LANES] = v_ref[...]
    vext_ref[:, :, LANES:] = jnp.ones(v_ref.shape, vext_ref.dtype)

    def locate(n):
        h, blk = (0, n) if heads == 1 else (n // nblk, n % nblk)
        rows = pl.ds(blk * tq, tq) if isinstance(blk, int) else pl.ds(pl.multiple_of(blk * tq, tq), tq)
        keys = slice(None) if win == S else pl.ds(win_start(blk), win)
        return h, blk, rows, keys

    def scores(n, par):
        h, blk, rows, keys = locate(n)
        s = lax.dot_general(q_ref[h, rows, :], k_ref[h // group, keys, :], (((1,), (1,)), ((), ())),
                            preferred_element_type=F32)
        if bias is not None:
            s = s + bias(h, blk)
        s_buf[par][...] = s
        m_buf[par][...] = jnp.max(s, axis=-1, keepdims=True)

    def exps(par):
        p_buf[par][...] = jnp.exp2(s_buf[par][...] - m_buf[par][...]).astype(BF16)

    def wsum(n, par):
        h, blk, rows, keys = locate(n)
        o_ext = jnp.dot(p_buf[par][...], vext_ref[h // group, keys, :], preferred_element_type=F32)
        o_ref[h, rows, :] = (o_ext[:, :LANES] / o_ext[:, LANES:]).astype(o_ref.dtype)

    scores(0, 0)
    exps(0)
    scores(1, 1)

    def step(j, carry):
        n = 2 * j + 1
        exps(1)
        wsum(n - 1, 0)
        scores(n + 1, 0)
        exps(0)
        wsum(n, 1)
        scores(n + 2, 1)
        return carry

    lax.fori_loop(0, total // 2 - 1, step, 0)
    exps(1)
    wsum(total - 2, 0)
    wsum(total - 1, 1)


def _attn_scratch(kv_heads, seq, win, tq):
    return [pltpu.VMEM((kv_heads, seq, 2 * LANES), BF16),
            pltpu.VMEM((tq, win), F32), pltpu.VMEM((tq, win), F32),
            pltpu.VMEM((tq, 1), F32), pltpu.VMEM((tq, 1), F32),
            pltpu.VMEM((tq, win), BF16), pltpu.VMEM((tq, win), BF16)]


def _dense_attn_kernel(q_ref, k_ref, v_ref, o_ref, vext_ref, s0, s1, m0, m1, p0, p1, *, heads, group):
    _attn_pipeline(q_ref, k_ref, v_ref, o_ref, vext_ref, (s0, s1), (m0, m1), (p0, p1), heads=heads, group=group,
                   win=q_ref.shape[1], win_start=None, bias=None)


def dense_attention(q, k, v, *, n_heads, heads, k0, v0, batch, seq, group):
    T, dq = q.shape[1:]
    kvh = heads // group
    assert n_heads % heads == 0 and heads % group == 0 and k0 % kvh == 0 and v0 % kvh == 0
    return pl.pallas_call(
        functools.partial(_dense_attn_kernel, heads=heads, group=group),
        grid=(batch, n_heads // heads),
        in_specs=[pl.BlockSpec((heads, seq, dq), lambda b, g: (g, b, 0)),
                  pl.BlockSpec((kvh, seq, dq), lambda b, g: (k0 // kvh + g, b, 0)),
                  pl.BlockSpec((kvh, seq, LANES), lambda b, g: (v0 // kvh + g, b, 0))],
        out_specs=pl.BlockSpec((heads, seq, LANES), lambda b, g: (g, b, 0)),
        out_shape=jax.ShapeDtypeStruct((n_heads, T, LANES), BF16),
        scratch_shapes=_attn_scratch(kvh, seq, seq, DENSE_TQ),
        compiler_params=_params(2),
        name="dense_attention",
    )(q, k, v)


DIL_REACH = max(w // 2 for w, _ in B_BRANCHES)
DIL_WIN = 2 * DIL_REACH + ATTN_TQ
DIL_TABLE_TILES = (DIL_WIN + 2 * DIL_REACH) // LANES


def _dilated_attn_kernel(q_ref, k_ref, v_ref, bias_ref, o_ref, vext_ref, s0, s1, m0, m1, p0, p1, *, heads):
    S = q_ref.shape[1]

    def win_start(blk):
        return pl.multiple_of(jnp.clip(blk * ATTN_TQ - DIL_REACH, 0, S - DIL_WIN), ATTN_TQ)

    def bias(h, blk):
        tile0 = (win_start(blk) - blk * ATTN_TQ + 2 * DIL_REACH) // LANES
        return jnp.concatenate([bias_ref[h, tile0 + t] for t in range(DIL_WIN // LANES)], axis=1)

    _attn_pipeline(q_ref, k_ref, v_ref, o_ref, vext_ref, (s0, s1), (m0, m1), (p0, p1), heads=heads, group=1,
                   win=DIL_WIN, win_start=win_start, bias=bias)


def dilated_bias_table(n_heads):
    col = (np.arange(DIL_TABLE_TILES)[:, None, None] * LANES + np.arange(LANES)[None, None, :])
    d = np.abs(col - np.arange(ATTN_TQ)[None, :, None] - 2 * DIL_REACH)
    mult = np.zeros(d.shape, np.float64)
    for window, dil in B_BRANCHES:
        mult += ((d % dil == 0) & (d <= window // 2)).astype(np.float64)
    with np.errstate(divide="ignore"):
        logm = np.log(mult)
    slopes = jnp.asarray(2.0 ** (-8.0 * np.arange(1, n_heads + 1) / n_heads), dtype=F32)
    bias = jnp.asarray(logm, dtype=F32)[None] - slopes[:, None, None, None] * jnp.asarray(d, dtype=F32)[None]
    return bias * LOG2E


def dilated_attention(qkv, bias, *, n_heads, heads, q0, k0, v0, batch, seq):
    T = qkv.shape[1]
    assert n_heads % heads == 0 and q0 % heads == 0 and k0 % heads == 0 and v0 % heads == 0
    qkv_spec = lambda h0: pl.BlockSpec((heads, seq, LANES), lambda b, g: (h0 // heads + g, b, 0))
    return pl.pallas_call(
        functools.partial(_dilated_attn_kernel, heads=heads),
        grid=(batch, n_heads // heads),
        in_specs=[qkv_spec(q0), qkv_spec(k0), qkv_spec(v0),
                  pl.BlockSpec((heads, DIL_TABLE_TILES, ATTN_TQ, LANES), lambda b, g: (g, 0, 0, 0))],
        out_specs=pl.BlockSpec((heads, seq, LANES), lambda b, g: (g, b, 0)),
        out_shape=jax.ShapeDtypeStruct((n_heads, T, LANES), BF16),
        scratch_shapes=_attn_scratch(heads, seq, DIL_WIN, ATTN_TQ),
        compiler_params=_params(2),
        name="dilated_attention",
    )(qkv, qkv, qkv, bias)


NA_QROWS = ATTN_TQ // GRID_W
NA_KROWS = NA_QROWS + NA_ROWS
NA_WIN = NA_KROWS * GRID_W


def _na_window_row(blk, rows):
    lo = blk * NA_QROWS - NA_ROWS // 2
    return jnp.clip(lo, 0, rows - NA_KROWS) if not isinstance(blk, int) else min(max(lo, 0), rows - NA_KROWS)


def _na_attn_kernel(q_ref, k_ref, v_ref, bias_ref, o_ref, vext_ref, s0, s1, m0, m1, p0, p1, *, heads, rows):
    nblk = rows // NA_QROWS

    def win_start(blk):
        return pl.multiple_of(_na_window_row(blk, rows) * GRID_W, GRID_W)

    def bias(h, blk):
        cls = (blk > 0).astype(jnp.int32) + (blk == nblk - 1).astype(jnp.int32) if not isinstance(blk, int) \
            else int(blk > 0) + int(blk == nblk - 1)
        return bias_ref[h, cls]

    _attn_pipeline(q_ref, k_ref, v_ref, o_ref, vext_ref, (s0, s1), (m0, m1), (p0, p1), heads=heads, group=1,
                   win=NA_WIN, win_start=win_start, bias=bias)


def na_bias_table(rpb, rows):
    H = rpb.shape[0]
    col = np.arange(GRID_W)
    c0 = np.clip(col - NA_COLS // 2, 0, GRID_W - NA_COLS)
    col_ok = (col[None, :] >= c0[:, None]) & (col[None, :] < c0[:, None] + NA_COLS)
    dcol = np.clip(col[None, :] - col[:, None] + NA_COLS - 1, 0, 2 * NA_COLS - 2)
    pick = ((dcol[:, :, None] == np.arange(2 * NA_COLS - 1)) & col_ok[:, :, None]).astype(np.float32)
    t = jnp.einsum("hrd,qkd->hrqk", rpb.astype(F32), jnp.asarray(pick), precision=lax.Precision.HIGHEST)
    t = jnp.where(jnp.asarray(col_ok)[None, None], t * LOG2E, -jnp.inf)
    nblk = rows // NA_QROWS

    def geometry(blk):
        r = blk * NA_QROWS + np.arange(NA_QROWS)
        kr = _na_window_row(blk, rows) + np.arange(NA_KROWS)
        r0 = np.clip(r - NA_ROWS // 2, 0, rows - NA_ROWS)
        valid = (kr[None, :] >= r0[:, None]) & (kr[None, :] < r0[:, None] + NA_ROWS)
        drow = np.clip(kr[None, :] - r[:, None] + NA_ROWS - 1, 0, 2 * NA_ROWS - 2)
        return valid, np.where(valid, drow, 0)

    inner = geometry(1)
    assert all(np.array_equal(a, b) for blk in range(1, nblk - 1) for a, b in zip(geometry(blk), inner))
    tables = []
    for blk in (0, 1, nblk - 1):
        valid, drow = geometry(blk)
        slab = jnp.where(jnp.asarray(valid)[None, :, :, None, None], t[:, drow], -jnp.inf)
        tables.append(slab.transpose(0, 1, 3, 2, 4).reshape(H, ATTN_TQ, NA_WIN))
    return jnp.stack(tables, axis=1)


def na_attention(qkv, bias, *, n_heads, heads, q0, k0, v0, batch, seq):
    T = qkv.shape[1]
    rows = seq // GRID_W
    assert n_heads % heads == 0 and q0 % heads == 0 and k0 % heads == 0 and v0 % heads == 0
    assert rows % NA_QROWS == 0 and rows // NA_QROWS >= 3
    qkv_spec = lambda h0: pl.BlockSpec((heads, seq, LANES), lambda b, g: (h0 // heads + g, b, 0))
    return pl.pallas_call(
        functools.partial(_na_attn_kernel, heads=heads, rows=rows),
        grid=(batch, n_heads // heads),
        in_specs=[qkv_spec(q0), qkv_spec(k0), qkv_spec(v0),
                  pl.BlockSpec((heads, 3, ATTN_TQ, NA_WIN), lambda b, g: (g, 0, 0, 0))],
        out_specs=pl.BlockSpec((heads, seq, LANES), lambda b, g: (g, b, 0)),
        out_shape=jax.ShapeDtypeStruct((n_heads, T, LANES), BF16),
        scratch_shapes=_attn_scratch(heads, seq, NA_WIN, ATTN_TQ),
        compiler_params=_params(2),
        name="na_attention",
    )(qkv, qkv, qkv, bias)


def _layer_norm(z, g, b):
    mu = jnp.mean(z, axis=-1, keepdims=True)
    zc = z - mu
    var = jnp.mean(zc * zc, axis=-1, keepdims=True)
    return zc * lax.rsqrt(var + LN_EPS) * g + b


def _top2_of4(v0, v1, v2, v3):
    a = jnp.maximum(v0, v1)
    b = jnp.minimum(v0, v1)
    c = jnp.maximum(v2, v3)
    d = jnp.minimum(v2, v3)
    return jnp.maximum(a, c), jnp.maximum(jnp.minimum(a, c), jnp.maximum(b, d))


def _route(logits_t, rb):
    G = N_GROUPS
    scores = jax.nn.sigmoid(logits_t)
    sel = scores + rb
    sj = [sel[j * G:(j + 1) * G] for j in range(EXPERTS_PER_GROUP)]
    cj = [scores[j * G:(j + 1) * G] for j in range(EXPERTS_PER_GROUP)]
    t1, t2 = _top2_of4(*sj)
    grp = t1 + t2
    gi = lax.broadcasted_iota(jnp.int32, grp.shape, 0).astype(F32)
    gbest = jnp.min(jnp.where(grp == jnp.max(grp, axis=0, keepdims=True), gi, float(G)), axis=0, keepdims=True)
    pick = gi == gbest
    v = [jnp.sum(jnp.where(pick, s, 0.0), axis=0, keepdims=True) for s in sj]
    c = [jnp.sum(jnp.where(pick, s, 0.0), axis=0, keepdims=True) for s in cj]
    neg = jnp.float32(-jnp.inf)
    m1 = jnp.maximum(jnp.maximum(v[0], v[1]), jnp.maximum(v[2], v[3]))
    j1 = jnp.where(v[0] == m1, 0, jnp.where(v[1] == m1, 1, jnp.where(v[2] == m1, 2, 3)))
    w = [jnp.where(j1 == j, neg, v[j]) for j in range(4)]
    m2 = jnp.maximum(jnp.maximum(w[0], w[1]), jnp.maximum(w[2], w[3]))
    j2 = jnp.where(w[0] == m2, 0, jnp.where(w[1] == m2, 1, jnp.where(w[2] == m2, 2, 3)))
    g1 = jnp.where(j1 == 0, c[0], jnp.where(j1 == 1, c[1], jnp.where(j1 == 2, c[2], c[3])))
    g2 = jnp.where(j2 == 0, c[0], jnp.where(j2 == 1, c[1], jnp.where(j2 == 2, c[2], c[3])))
    tot = g1 + g2
    e0 = gbest.astype(jnp.int32) * EXPERTS_PER_GROUP
    return (e0 + j1, e0 + j2), (g1 / tot, g2 / tot)


OUT_SUBTILES = 2


def _out_ln_kernel(a_ref, b_ref, w_ref, x_ref, g_ref, beta_ref, x1_ref, x1p_ref):
    sub = x_ref.shape[0] // OUT_SUBTILES
    half = x_ref.shape[1] // 2
    for t in range(OUT_SUBTILES):
        rows = slice(t * sub, (t + 1) * sub)
        heads = [a_ref[h, rows, :] for h in range(a_ref.shape[0])] + [b_ref[h, rows, :] for h in range(b_ref.shape[0])]
        mix = jnp.dot(jnp.concatenate(heads, axis=1), w_ref[...], preferred_element_type=F32)
        x1 = _layer_norm(ALPHA * x_ref[rows, :] + mix, g_ref[...], beta_ref[...])
        x1_ref[rows, :] = x1
        x1p_ref[rows, :] = _pack_bf16_pair(x1[:, :half], x1[:, half:])


def out_ln(a, b, w, x, g, beta, *, bm=512):
    T, D = x.shape
    return pl.pallas_call(
        _out_ln_kernel,
        grid=(T // bm,),
        in_specs=[pl.BlockSpec((a.shape[0], bm, LANES), lambda i: (0, i, 0)),
                  pl.BlockSpec((b.shape[0], bm, LANES), lambda i: (0, i, 0)),
                  _resident(w.shape),
                  pl.BlockSpec((bm, D), lambda i: (i, 0)),
                  _resident(g.shape), _resident(beta.shape)],
        out_specs=[pl.BlockSpec((bm, D), lambda i: (i, 0)), pl.BlockSpec((bm, D // 2), lambda i: (i, 0))],
        out_shape=[jax.ShapeDtypeStruct((T, D), F32), jax.ShapeDtypeStruct((T, D // 2), jnp.uint32)],
        compiler_params=_params(1),
        name="out_ln",
    )(a, b, w, x, g, beta)


def _route_kernel(x_ref, rwh_ref, rwl_ref, rb_ref, eid_ref, gate_ref):
    x = x_ref[...]
    hi = x.astype(BF16)
    lo = (x - hi.astype(F32)).astype(BF16)
    nt = (((1,), (1,)), ((), ()))
    logits_t = (lax.dot_general(rwh_ref[...], hi, nt, preferred_element_type=F32)
                + lax.dot_general(rwl_ref[...], hi, nt, preferred_element_type=F32)
                + lax.dot_general(rwh_ref[...], lo, nt, preferred_element_type=F32))
    eid, gate = _route(logits_t, rb_ref[...])
    for k in range(TOP_K):
        eid_ref[k:k + 1, :] = eid[k]
        gate_ref[k:k + 1, :] = gate[k]


def route(x1, rw_hi, rw_lo, rb, *, bm=1024):
    T, D = x1.shape
    return pl.pallas_call(
        _route_kernel,
        grid=(T // bm,),
        in_specs=[pl.BlockSpec((bm, D), lambda i: (i, 0)),
                  _resident(rw_hi.shape), _resident(rw_lo.shape), _resident(rb.shape)],
        out_specs=[pl.BlockSpec((TOP_K, bm), lambda i: (0, i)),
                   pl.BlockSpec((TOP_K, bm), lambda i: (0, i))],
        out_shape=[jax.ShapeDtypeStruct((TOP_K, T), jnp.int32),
                   jax.ShapeDtypeStruct((TOP_K, T), F32)],
        compiler_params=_params(1),
        name="route",
    )(x1, rw_hi, rw_lo, rb)


GATHER_DEPTH = 3


def _row_copy(src_hbm, row, dst_vmem, j, sem):
    return pltpu.make_async_copy(src_hbm.at[pl.ds(row, 1), :], dst_vmem.at[pl.ds(j, 1), :], sem)


def _gather_ring_step(start_gather, wait_gather, compute, active=None, idle=None):
    i = pl.program_id(0)
    n_steps = pl.num_programs(0)
    ahead = GATHER_DEPTH - 1
    slot = i % GATHER_DEPTH

    @pl.when(i == 0)
    def _():
        for a in range(ahead):
            start_gather(a, a)

    wait_gather(slot)
    more = i + ahead < n_steps
    cases = [(more, compute)] if active is None else [(more & active, compute), (more & ~active, idle)]
    for cond, work in cases:
        @pl.when(cond)
        def _(work=work):
            start_gather(i + ahead, (i + ahead) % GATHER_DEPTH)
            work(slot)
    cases = [(~more, compute)] if active is None else [(~more & active, compute), (~more & ~active, idle)]
    for cond, work in cases:
        @pl.when(cond)
        def _(work=work):
            work(slot)


def _pack_bf16_pair(lo, hi):
    lo_bits = pltpu.bitcast(lo.astype(BF16).astype(F32), jnp.uint32)
    hi_bits = pltpu.bitcast(hi.astype(BF16).astype(F32), jnp.uint32)
    return lax.shift_right_logical(lo_bits, jnp.uint32(16)) | (hi_bits & jnp.uint32(0xFFFF0000))


def _unpack_bf16_pair(word):
    lo = pltpu.bitcast(lax.shift_left(word, jnp.uint32(16)), F32)
    hi = pltpu.bitcast(word & jnp.uint32(0xFFFF0000), F32)
    return lo, hi


def _moe_ffn_kernel(blk_lo_ref, blk_hi_ref, nblk_ref, tok_ref, x_hbm, wg_ref, wu_ref, wd_ref, y_ref, xbuf, sem):
    i = pl.program_id(0)
    members = (blk_lo_ref[i] % EXPERTS_PER_GROUP, blk_hi_ref[i] % EXPERTS_PER_GROUP)

    def start_gather(blk, s):
        base = blk * MOE_BLOCK
        for j in range(MOE_BLOCK):
            _row_copy(x_hbm, tok_ref[base + j], xbuf.at[s], j, sem.at[s]).start()

    def wait_gather(s):
        pltpu.make_async_copy(x_hbm.at[pl.ds(0, MOE_BLOCK), :], xbuf.at[s], sem.at[s]).wait()

    def ffn(s):
        x_left, x_right = _unpack_bf16_pair(xbuf[s])
        xb = jnp.concatenate([x_left.astype(BF16), x_right.astype(BF16)], axis=1)
        ys = []
        for e in members:
            hg = jnp.dot(xb, wg_ref[e], preferred_element_type=F32)
            hu = jnp.dot(xb, wu_ref[e], preferred_element_type=F32)
            hb = (jax.nn.silu(hg) * hu).astype(BF16)
            ys.append(jnp.dot(hb, wd_ref[e], preferred_element_type=F32))
        y_ref[...] = _pack_bf16_pair(ys[0], ys[1])

    def no_tokens(s):
        y_ref[...] = jnp.zeros(y_ref.shape, y_ref.dtype)

    _gather_ring_step(start_gather, wait_gather, ffn, active=pl.program_id(0) < nblk_ref[0], idle=no_tokens)


def moe_ffn(x1p, w_gate, w_up, w_down, blk_lo, blk_hi, nblk, row_tok):
    D = 2 * x1p.shape[1]
    n_blocks = blk_lo.shape[0]
    assert n_blocks >= GATHER_DEPTH
    n_rows = n_blocks * MOE_BLOCK
    group_w = lambda shape: pl.BlockSpec((EXPERTS_PER_GROUP,) + shape,
                                         lambda i, lo, hi, nb, tk: (lo[i] // EXPERTS_PER_GROUP, 0, 0),
                                         pipeline_mode=pl.Buffered(1))
    grid_spec = pltpu.PrefetchScalarGridSpec(
        num_scalar_prefetch=4,
        grid=(n_blocks,),
        in_specs=[pl.BlockSpec(memory_space=pl.ANY), group_w((D, D_EXPERT)), group_w((D, D_EXPERT)),
                  group_w((D_EXPERT, D))],
        out_specs=pl.BlockSpec((MOE_BLOCK, D), lambda i, lo, hi, nb, tk: (i, 0)),
        scratch_shapes=[pltpu.VMEM((GATHER_DEPTH, MOE_BLOCK, D // 2), jnp.uint32),
                        pltpu.SemaphoreType.DMA((GATHER_DEPTH,))],
    )
    return pl.pallas_call(
        _moe_ffn_kernel,
        grid_spec=grid_spec,
        out_shape=jax.ShapeDtypeStruct((n_rows, D), jnp.uint32),
        compiler_params=_params(1),
        name="moe_ffn",
    )(blk_lo, blk_hi, nblk, row_tok, x1p, w_gate, w_up, w_down)


def _combine_ln_kernel(dest_ref, y_hbm, x1_ref, gate_ref, g_ref, beta_ref, x2_ref, x2b_ref, ybuf, sem, *, bm):
    def start_gather(tile, s):
        base = tile * bm
        for j in range(bm):
            _row_copy(y_hbm, dest_ref[base + j], ybuf.at[s], j, sem.at[s]).start()

    def wait_gather(s):
        pltpu.make_async_copy(y_hbm.at[pl.ds(0, bm), :], ybuf.at[s], sem.at[s]).wait()

    def finish(s):
        gate = gate_ref[...]
        y_lo, y_hi = _unpack_bf16_pair(ybuf[s])
        moe = y_lo * gate[:, 0:1] + y_hi * gate[:, 1:2]
        x2 = _layer_norm(ALPHA * x1_ref[...] + moe, g_ref[...], beta_ref[...])
        x2_ref[...] = x2
        x2b_ref[...] = x2.astype(BF16)

    _gather_ring_step(start_gather, wait_gather, finish)


def combine_ln(y_rows, dest, x1, gate_t, g, beta, *, bm=256):
    T, D = x1.shape
    assert T // bm >= GATHER_DEPTH
    grid_spec = pltpu.PrefetchScalarGridSpec(
        num_scalar_prefetch=1,
        grid=(T // bm,),
        in_specs=[pl.BlockSpec(memory_space=pl.ANY),
                  pl.BlockSpec((bm, D), lambda i, d: (i, 0)),
                  pl.BlockSpec((bm, TOP_K), lambda i, d: (i, 0)),
                  pl.BlockSpec((1, D), lambda i, d: (0, 0)),
                  pl.BlockSpec((1, D), lambda i, d: (0, 0))],
        out_specs=[pl.BlockSpec((bm, D), lambda i, d: (i, 0)),
                   pl.BlockSpec((bm, D), lambda i, d: (i, 0))],
        scratch_shapes=[pltpu.VMEM((GATHER_DEPTH, bm, D), jnp.uint32), pltpu.SemaphoreType.DMA((GATHER_DEPTH,))],
    )
    return pl.pallas_call(
        functools.partial(_combine_ln_kernel, bm=bm),
        grid_spec=grid_spec,
        out_shape=[jax.ShapeDtypeStruct((T, D), F32), jax.ShapeDtypeStruct((T, D), BF16)],
        compiler_params=_params(1),
        name="combine_ln",
    )(dest, y_rows, x1, gate_t, g, beta)


_PAIR_LO = np.array([a for a in range(EXPERTS_PER_GROUP) for b in range(a + 1, EXPERTS_PER_GROUP)], np.int32)
_PAIR_HI = np.array([b for a in range(EXPERTS_PER_GROUP) for b in range(a + 1, EXPERTS_PER_GROUP)], np.int32)
N_PAIRS = len(_PAIR_LO)
N_CLASSES = N_GROUPS * N_PAIRS


def moe_plan(eid, gate):
    T = eid.shape[1]
    n_blocks = (T + N_CLASSES * (MOE_BLOCK - 1) + MOE_BLOCK - 1) // MOE_BLOCK
    first_lower = eid[0] < eid[1]
    e_lo = jnp.minimum(eid[0], eid[1])
    e_hi = jnp.maximum(eid[0], eid[1])
    gate_t = jnp.stack([jnp.where(first_lower, gate[0], gate[1]), jnp.where(first_lower, gate[1], gate[0])], axis=1)
    a = e_lo % EXPERTS_PER_GROUP
    b = e_hi % EXPERTS_PER_GROUP
    pair = a * (2 * EXPERTS_PER_GROUP - 1 - a) // 2 + (b - a - 1)
    cls = (e_lo // EXPERTS_PER_GROUP) * N_PAIRS + pair
    onehot = (cls[:, None] == jnp.arange(N_CLASSES, dtype=jnp.int32)[None, :]).astype(jnp.int32)
    csum = jnp.cumsum(onehot, axis=0)
    rank = jnp.sum(csum * onehot, axis=1) - 1
    counts = csum[-1]
    pcounts = (counts + MOE_BLOCK - 1) // MOE_BLOCK * MOE_BLOCK
    pends = jnp.cumsum(pcounts)
    pstarts = pends - pcounts
    dest = (pstarts[cls] + rank).astype(jnp.int32)
    row_tok = jnp.zeros((n_blocks * MOE_BLOCK,), jnp.int32).at[dest].set(jnp.arange(T, dtype=jnp.int32))
    blk_start = jnp.arange(n_blocks, dtype=jnp.int32) * MOE_BLOCK
    blk_cls = jnp.minimum(jnp.sum((pends[None, :] <= blk_start[:, None]).astype(jnp.int32), axis=1), N_CLASSES - 1)
    blk_base = (blk_cls // N_PAIRS) * EXPERTS_PER_GROUP
    blk_lo = (blk_base + jnp.asarray(_PAIR_LO)[blk_cls % N_PAIRS]).astype(jnp.int32)
    blk_hi = (blk_base + jnp.asarray(_PAIR_HI)[blk_cls % N_PAIRS]).astype(jnp.int32)
    nblk = (pends[-1:] // MOE_BLOCK).astype(jnp.int32)
    return dest, row_tok, blk_lo, blk_hi, nblk, gate_t


def moe_layer(x1, x1p, eid, gate, w_gate, w_up, w_down, ln_g, ln_b):
    dest, row_tok, blk_lo, blk_hi, nblk, gate_t = moe_plan(eid, gate)
    y_rows = moe_ffn(x1p, w_gate.astype(BF16), w_up.astype(BF16), w_down.astype(BF16), blk_lo, blk_hi, nblk, row_tok)
    return combine_ln(y_rows, dest, x1, gate_t, ln_g.reshape(1, -1), ln_b.reshape(1, -1))


def _rope_cos_sin(pos, dim):
    inv_freq = ROPE_THETA ** (-jnp.arange(0, dim, 2, dtype=F32) / dim)
    ang = pos.astype(F32)[:, None] * inv_freq[None, :]
    cos = jnp.cos(ang)
    sin = jnp.sin(ang)
    return jnp.concatenate([cos, cos], axis=-1), jnp.concatenate([-sin, sin], axis=-1)


def axial_rope_tables(seq):
    pos = jnp.arange(seq)
    half = HEAD_DIM // 2
    cr, sr = _rope_cos_sin(pos // GRID_W, half)
    cc, sc = _rope_cos_sin(pos % GRID_W, half)
    return jnp.concatenate([cr, cc], axis=-1), jnp.concatenate([sr, sc], axis=-1)


def mla_rope_tables(seq):
    c, s = _rope_cos_sin(jnp.arange(seq), D_ROPE)
    z = jnp.zeros((seq, LANES - D_ROPE), F32)
    return jnp.concatenate([c, z], axis=-1), jnp.concatenate([s, z], axis=-1)


def router_layout(router_w, router_b):
    perm = np.array([g * EXPERTS_PER_GROUP + j for j in range(EXPERTS_PER_GROUP) for g in range(N_GROUPS)])
    rw = router_w.astype(F32).T[perm]
    hi = rw.astype(BF16)
    lo = (rw - hi.astype(F32)).astype(BF16)
    return hi, lo, router_b.astype(F32)[perm].reshape(N_EXPERTS, 1)


def mla_up_layout(w_q_up, w_kv_up, w_in_tail):
    rq = w_q_up.shape[0]
    wq = w_q_up.reshape(rq, D_HEADS, D_NOPE + D_ROPE)
    wq = jnp.pad(wq, ((0, 0), (0, 0), (0, 2 * LANES - (D_NOPE + D_ROPE)))).reshape(rq, D_HEADS * 2 * LANES)
    wc = jnp.pad(w_in_tail, ((0, 0), (0, LANES - D_ROPE)))
    return wq.astype(BF16), w_kv_up.astype(BF16), wc.astype(BF16)


def kernel(x, router_w, router_b, l0_w_in, l0_a_q_norm, l0_a_k_norm, l0_w_out, l0_ln1_g, l0_ln1_b, l0_w_gate, l0_w_up, l0_w_down, l0_ln2_g, l0_ln2_b, l1_w_in, l1_c_rpb, l1_d_q_norm, l1_d_w_q_up, l1_d_kv_norm, l1_d_w_kv_up, l1_w_out, l1_ln1_g, l1_ln1_b, l1_w_gate, l1_w_up, l1_w_down, l1_ln2_g, l1_ln2_b):
    batch, seq, d = x.shape
    T = batch * seq
    xf = x.reshape(T, d)
    rw_hi, rw_lo, rb = router_layout(router_w, router_b)
    row = lambda v: v.astype(F32).reshape(1, -1)

    a_q_dim = A_HEADS * HEAD_DIM
    a_qk_dim = a_q_dim + A_KV_HEADS * HEAD_DIM
    w0 = l0_w_in.astype(BF16)
    qk_scale = LOG2E * HEAD_DIM ** -0.5
    gain = jnp.concatenate([jnp.tile(l0_a_q_norm.astype(F32), A_HEADS) * qk_scale,
                            jnp.tile(l0_a_k_norm.astype(F32), A_KV_HEADS)]).reshape(1, -1)
    cos_a, sin_a = axial_rope_tables(seq)
    qk_a = proj_norm_rope(xf, w0[:, :a_qk_dim], gain, cos_a, sin_a, seq=seq)
    nb = A_KV_HEADS
    q_scale = jnp.full((B_HEADS * HEAD_DIM,), qk_scale, F32)
    scale0 = jnp.concatenate([jnp.ones((nb * HEAD_DIM,), F32), q_scale, jnp.ones((2 * B_HEADS * HEAD_DIM,), F32)])
    rest0 = proj_heads(xf, w0[:, a_qk_dim:], scale0.reshape(1, -1), heads_per_tile=13)
    group = A_HEADS // A_KV_HEADS
    oa = dense_attention(qk_a, qk_a, rest0, n_heads=A_HEADS, heads=group, k0=A_HEADS, v0=0, batch=batch, seq=seq,
                         group=group)
    ob = dilated_attention(rest0, dilated_bias_table(B_HEADS), n_heads=B_HEADS, heads=1, q0=nb, k0=nb + B_HEADS,
                           v0=nb + 2 * B_HEADS, batch=batch, seq=seq)
    x1, x1p = out_ln(oa, ob, l0_w_out.astype(BF16), xf, row(l0_ln1_g), row(l0_ln1_b))
    eid, gate = route(x1, rw_hi, rw_lo, rb)
    x2, x2b = moe_layer(x1, x1p, eid, gate, l0_w_gate, l0_w_up, l0_w_down, l0_ln2_g, l0_ln2_b)

    c_dim = C_HEADS * HEAD_DIM
    w1 = l1_w_in
    scale1 = jnp.concatenate([jnp.full((c_dim,), qk_scale, F32), jnp.ones((2 * c_dim,), F32)])
    qkv_c = proj_heads(x2b, w1[:, :3 * c_dim].astype(BF16), scale1.reshape(1, -1), heads_per_tile=12)
    oc = na_attention(qkv_c, na_bias_table(l1_c_rpb, seq // GRID_W), n_heads=C_HEADS, heads=2, q0=0, k0=C_HEADS,
                      v0=2 * C_HEADS, batch=batch, seq=seq)
    wq, wkv, wc = mla_up_layout(l1_d_w_q_up, l1_d_w_kv_up, w1[:, 3 * c_dim:])
    cos_d, sin_d = mla_rope_tables(seq)
    qd, kd, vd = mla_proj(x2b, wc, row(l1_d_q_norm), row(l1_d_kv_norm), wq, wkv, cos_d, sin_d, seq=seq,
                          scale=LOG2E * (D_NOPE + D_ROPE) ** -0.5)
    od = dense_attention(qd, kd, vd, n_heads=D_HEADS, heads=1, k0=0, v0=0, batch=batch, seq=seq, group=1)
    x3, x3p = out_ln(oc, od, l1_w_out.astype(BF16), x2, row(l1_ln1_g), row(l1_ln1_b))
    eid, gate = route(x3, rw_hi, rw_lo, rb)
    x4, _ = moe_layer(x3, x3p, eid, gate, l1_w_gate, l1_w_up, l1_w_down, l1_ln2_g, l1_ln2_b)
    return x4.reshape(batch, seq, d)
```

```python
import functools
import math

import jax
import jax.numpy as jnp
import numpy as np
from jax import lax
from jax.experimental import pallas as pl
from jax.experimental.pallas import tpu as pltpu

D_MODEL = 2048
DEPTH = 2
GRID_W = 64
HEAD_DIM = 128
ROPE_THETA = 10000.0
RMS_EPS = 1e-6
LN_EPS = 1e-5

A_HEADS = 8
A_KV_HEADS = 2
B_HEADS = 8
B_BRANCHES = ((128, 1), (512, 4), (2048, 16))
C_HEADS = 8
NA_ROWS = 8
NA_COLS = 16
D_HEADS = 8
D_Q_RANK = 512
D_KV_RANK = 256
D_NOPE = 128
D_ROPE = 64
D_V = 128

N_EXPERTS = 32
N_GROUPS = 8
EXPERTS_PER_GROUP = N_EXPERTS // N_GROUPS
TOP_K = 2
D_EXPERT = 512
MOE_BLOCK = 256

ALPHA = (2.0 * DEPTH) ** 0.25
LOG2E = math.log2(math.e)

LANES = 128
VMEM_LIMIT = 56 * 1024 * 1024

BF16 = jnp.bfloat16
F32 = jnp.float32


def _params(n_axes, **flags):
    return pltpu.CompilerParams(dimension_semantics=("arbitrary",) * n_axes, vmem_limit_bytes=VMEM_LIMIT,
                                flags=flags or None)


def _rot_half32(y):
    lane = lax.broadcasted_iota(jnp.int32, y.shape, y.ndim - 1)
    up = pltpu.roll(y, 96, axis=y.ndim - 1)
    dn = pltpu.roll(y, 32, axis=y.ndim - 1)
    return jnp.where((lane % 64) < 32, up, dn)


def _proj_heads_kernel(x_ref, w_ref, c_ref, o_ref, *, heads_per_tile):
    acc = jnp.dot(x_ref[...].astype(BF16), w_ref[...], preferred_element_type=F32) * c_ref[...]
    for h in range(heads_per_tile):
        o_ref[h] = acc[:, h * LANES:(h + 1) * LANES].astype(o_ref.dtype)


def proj_heads(x, w, col_scale, *, heads_per_tile, bm=512):
    T, D = x.shape
    H = w.shape[1] // LANES
    bn = heads_per_tile * LANES
    grid = (H // heads_per_tile, T // bm)
    return pl.pallas_call(
        functools.partial(_proj_heads_kernel, heads_per_tile=heads_per_tile),
        grid=grid,
        in_specs=[pl.BlockSpec((bm, D), lambda j, i: (i, 0)),
                  pl.BlockSpec((D, bn), lambda j, i: (0, j)),
                  pl.BlockSpec((1, bn), lambda j, i: (0, j))],
        out_specs=pl.BlockSpec((heads_per_tile, bm, LANES), lambda j, i: (j, i, 0)),
        out_shape=jax.ShapeDtypeStruct((H, T, LANES), BF16),
        compiler_params=_params(2),
        name="proj_heads",
    )(x, w, col_scale)


def _resident(shape):
    return pl.BlockSpec(shape, lambda *_: (0,) * len(shape), pipeline_mode=pl.Buffered(1))


PROJ_SUBTILES = 2


def _proj_norm_rope_kernel(x_ref, w_ref, g_ref, cos_ref, sin_ref, o_ref):
    sub = x_ref.shape[0] // PROJ_SUBTILES
    for t in range(PROJ_SUBTILES):
        rows = slice(t * sub, (t + 1) * sub)
        acc = jnp.dot(x_ref[rows, :].astype(BF16), w_ref[...], preferred_element_type=F32)
        cos = cos_ref[rows, :]
        sin = sin_ref[rows, :]
        for h in range(o_ref.shape[0]):
            a = acc[:, h * LANES:(h + 1) * LANES]
            ms = jnp.mean(a * a, axis=-1, keepdims=True)
            y = a * lax.rsqrt(ms + RMS_EPS) * g_ref[:, h * LANES:(h + 1) * LANES]
            o_ref[h, rows, :] = (y * cos + _rot_half32(y) * sin).astype(o_ref.dtype)


def proj_norm_rope(x, w, gain, cos, sin, *, seq, bm=512):
    T, D = x.shape
    H = w.shape[1] // LANES
    nsb = seq // bm
    return pl.pallas_call(
        _proj_norm_rope_kernel,
        grid=(T // bm,),
        in_specs=[pl.BlockSpec((bm, D), lambda i: (i, 0)),
                  _resident(w.shape), _resident(gain.shape),
                  pl.BlockSpec((bm, LANES), lambda i: (i % nsb, 0)),
                  pl.BlockSpec((bm, LANES), lambda i: (i % nsb, 0))],
        out_specs=pl.BlockSpec((H, bm, LANES), lambda i: (0, i, 0)),
        out_shape=jax.ShapeDtypeStruct((H, T, LANES), BF16),
        compiler_params=_params(1),
        name="proj_norm_rope",
    )(x, w, gain, cos, sin)


def _mla_proj_kernel(x_ref, wc_ref, gq_ref, gkv_ref, wq_ref, wkv_ref, cos_ref, sin_ref,
                     q_ref, k_ref, v_ref, *, scale):
    c = jnp.dot(x_ref[...], wc_ref[...], preferred_element_type=F32)
    cos = cos_ref[...]
    sin = sin_ref[...]

    def rms(t, g):
        ms = jnp.mean(t * t, axis=-1, keepdims=True)
        return t * lax.rsqrt(ms + RMS_EPS) * g

    cq = rms(c[:, :D_Q_RANK], gq_ref[...]).astype(BF16)
    ckv = rms(c[:, D_Q_RANK:D_Q_RANK + D_KV_RANK], gkv_ref[...]).astype(BF16)
    kr = c[:, D_Q_RANK + D_KV_RANK:]
    kr = (kr * cos + _rot_half32(kr) * sin).astype(k_ref.dtype)
    q = jnp.dot(cq, wq_ref[...], preferred_element_type=F32) * scale
    kv = jnp.dot(ckv, wkv_ref[...], preferred_element_type=F32)
    for h in range(D_HEADS):
        base = h * 2 * LANES
        q_ref[h, :, :LANES] = q[:, base:base + LANES].astype(q_ref.dtype)
        qr = q[:, base + LANES:base + 2 * LANES]
        q_ref[h, :, LANES:] = (qr * cos + _rot_half32(qr) * sin).astype(q_ref.dtype)
        k_ref[h, :, :LANES] = kv[:, base:base + LANES].astype(k_ref.dtype)
        k_ref[h, :, LANES:] = kr
        v_ref[h] = kv[:, base + LANES:base + 2 * LANES].astype(v_ref.dtype)


def mla_proj(x, wc, gq, gkv, wq, wkv, cos, sin, *, seq, scale, bm=512):
    T, D = x.shape
    nsb = seq // bm
    return pl.pallas_call(
        functools.partial(_mla_proj_kernel, scale=scale),
        grid=(T // bm,),
        in_specs=[pl.BlockSpec((bm, D), lambda i: (i, 0)),
                  _resident(wc.shape), _resident(gq.shape), _resident(gkv.shape), _resident(wq.shape),
                  _resident(wkv.shape),
                  pl.BlockSpec((bm, LANES), lambda i: (i % nsb, 0)),
                  pl.BlockSpec((bm, LANES), lambda i: (i % nsb, 0))],
        out_specs=[pl.BlockSpec((D_HEADS, bm, 2 * LANES), lambda i: (0, i, 0)),
                   pl.BlockSpec((D_HEADS, bm, 2 * LANES), lambda i: (0, i, 0)),
                   pl.BlockSpec((D_HEADS, bm, LANES), lambda i: (0, i, 0))],
        out_shape=[jax.ShapeDtypeStruct((D_HEADS, T, 2 * LANES), BF16),
                   jax.ShapeDtypeStruct((D_HEADS, T, 2 * LANES), BF16),
                   jax.ShapeDtypeStruct((D_HEADS, T, LANES), BF16)],
        compiler_params=_params(1),
        name="mla_proj",
    )(x, wc, gq, gkv, wq, wkv, cos, sin)


DENSE_TQ = 512
ATTN_TQ = 256


def _attn_pipeline(q_ref, k_ref, v_ref, o_ref, vext_ref, s_buf, m_buf, p_buf, *, heads, group, win, win_start, bias):
    tq = s_buf[0].shape[0]
    S = q_ref.shape[1]
    nblk = S // tq
    total = heads * nblk
    vext_ref[:, :, :LANES] = v_ref[...]
    vext_ref[:, :, LANES:] = jnp.ones(v_ref.shape, vext_ref.dtype)

    def locate(n):
        h, blk = (0, n) if heads == 1 else (n // nblk, n % nblk)
        rows = pl.ds(blk * tq, tq) if isinstance(blk, int) else pl.ds(pl.multiple_of(blk * tq, tq), tq)
        keys = slice(None) if win == S else pl.ds(win_start(blk), win)
        return h, blk, rows, keys

    def scores(n, par):
        h, blk, rows, keys = locate(n)
        s = lax.dot_general(q_ref[h, rows, :], k_ref[h // group, keys, :], (((1,), (1,)), ((), ())),
                            preferred_element_type=F32)
        if bias is not None:
            s = s + bias(h, blk)
        s_buf[par][...] = s
        m_buf[par][...] = jnp.max(s, axis=-1, keepdims=True)

    def exps(par):
        p_buf[par][...] = jnp.exp2(s_buf[par][...] - m_buf[par][...]).astype(BF16)

    def wsum(n, par):
        h, blk, rows, keys = locate(n)
        o_ext = jnp.dot(p_buf[par][...], vext_ref[h // group, keys, :], preferred_element_type=F32)
        o_ref[h, rows, :] = (o_ext[:, :LANES] / o_ext[:, LANES:]).astype(o_ref.dtype)

    scores(0, 0)
    exps(0)
    scores(1, 1)

    def step(j, carry):
        n = 2 * j + 1
        exps(1)
        wsum(n - 1, 0)
        scores(n + 1, 0)
        exps(0)
        wsum(n, 1)
        scores(n + 2, 1)
        return carry

    lax.fori_loop(0, total // 2 - 1, step, 0)
    exps(1)
    wsum(total - 2, 0)
    wsum(total - 1, 1)


def _attn_scratch(kv_heads, seq, win, tq):
    return [pltpu.VMEM((kv_heads, seq, 2 * LANES), BF16),
            pltpu.VMEM((tq, win), F32), pltpu.VMEM((tq, win), F32),
            pltpu.VMEM((tq, 1), F32), pltpu.VMEM((tq, 1), F32),
            pltpu.VMEM((tq, win), BF16), pltpu.VMEM((tq, win), BF16)]


def _dense_attn_kernel(q_ref, k_ref, v_ref, o_ref, vext_ref, s0, s1, m0, m1, p0, p1, *, heads, group):
    _attn_pipeline(q_ref, k_ref, v_ref, o_ref, vext_ref, (s0, s1), (m0, m1), (p0, p1), heads=heads, group=group,
                   win=q_ref.shape[1], win_start=None, bias=None)


def dense_attention(q, k, v, *, n_heads, heads, k0, v0, batch, seq, group):
    T, dq = q.shape[1:]
    kvh = heads // group
    assert n_heads % heads == 0 and heads % group == 0 and k0 % kvh == 0 and v0 % kvh == 0
    return pl.pallas_call(
        functools.partial(_dense_attn_kernel, heads=heads, group=group),
        grid=(batch, n_heads // heads),
        in_specs=[pl.BlockSpec((heads, seq, dq), lambda b, g: (g, b, 0)),
                  pl.BlockSpec((kvh, seq, dq), lambda b, g: (k0 // kvh + g, b, 0)),
                  pl.BlockSpec((kvh, seq, LANES), lambda b, g: (v0 // kvh + g, b, 0))],
        out_specs=pl.BlockSpec((heads, seq, LANES), lambda b, g: (g, b, 0)),
        out_shape=jax.ShapeDtypeStruct((n_heads, T, LANES), BF16),
        scratch_shapes=_attn_scratch(kvh, seq, seq, DENSE_TQ),
        compiler_params=_params(2),
        name="dense_attention",
    )(q, k, v)


DIL_REACH = max(w // 2 for w, _ in B_BRANCHES)
DIL_WIN = 2 * DIL_REACH + ATTN_TQ
DIL_TABLE_TILES = (DIL_WIN + 2 * DIL_REACH) // LANES


def _dilated_attn_kernel(q_ref, k_ref, v_ref, bias_ref, o_ref, vext_ref, s0, s1, m0, m1, p0, p1, *, heads):
    S = q_ref.shape[1]

    def win_start(blk):
        return pl.multiple_of(jnp.clip(blk * ATTN_TQ - DIL_REACH, 0, S - DIL_WIN), ATTN_TQ)

    def bias(h, blk):
        tile0 = (win_start(blk) - blk * ATTN_TQ + 2 * DIL_REACH) // LANES
        return jnp.concatenate([bias_ref[h, tile0 + t] for t in range(DIL_WIN // LANES)], axis=1)

    _attn_pipeline(q_ref, k_ref, v_ref, o_ref, vext_ref, (s0, s1), (m0, m1), (p0, p1), heads=heads, group=1,
                   win=DIL_WIN, win_start=win_start, bias=bias)


def dilated_bias_table(n_heads):
    col = (np.arange(DIL_TABLE_TILES)[:, None, None] * LANES + np.arange(LANES)[None, None, :])
    d = np.abs(col - np.arange(ATTN_TQ)[None, :, None] - 2 * DIL_REACH)
    mult = np.zeros(d.shape, np.float64)
    for window, dil in B_BRANCHES:
        mult += ((d % dil == 0) & (d <= window // 2)).astype(np.float64)
    with np.errstate(divide="ignore"):
        logm = np.log(mult)
    slopes = jnp.asarray(2.0 ** (-8.0 * np.arange(1, n_heads + 1) / n_heads), dtype=F32)
    bias = jnp.asarray(logm, dtype=F32)[None] - slopes[:, None, None, None] * jnp.asarray(d, dtype=F32)[None]
    return bias * LOG2E


def dilated_attention(qkv, bias, *, n_heads, heads, q0, k0, v0, batch, seq):
    T = qkv.shape[1]
    assert n_heads % heads == 0 and q0 % heads == 0 and k0 % heads == 0 and v0 % heads == 0
    qkv_spec = lambda h0: pl.BlockSpec((heads, seq, LANES), lambda b, g: (h0 // heads + g, b, 0))
    return pl.pallas_call(
        functools.partial(_dilated_attn_kernel, heads=heads),
        grid=(batch, n_heads // heads),
        in_specs=[qkv_spec(q0), qkv_spec(k0), qkv_spec(v0),
                  pl.BlockSpec((heads, DIL_TABLE_TILES, ATTN_TQ, LANES), lambda b, g: (g, 0, 0, 0))],
        out_specs=pl.BlockSpec((heads, seq, LANES), lambda b, g: (g, b, 0)),
        out_shape=jax.ShapeDtypeStruct((n_heads, T, LANES), BF16),
        scratch_shapes=_attn_scratch(heads, seq, DIL_WIN, ATTN_TQ),
        compiler_params=_params(2),
        name="dilated_attention",
    )(qkv, qkv, qkv, bias)


NA_QROWS = ATTN_TQ // GRID_W
NA_KROWS = NA_QROWS + NA_ROWS
NA_WIN = NA_KROWS * GRID_W


def _na_window_row(blk, rows):
    lo = blk * NA_QROWS - NA_ROWS // 2
    return jnp.clip(lo, 0, rows - NA_KROWS) if not isinstance(blk, int) else min(max(lo, 0), rows - NA_KROWS)


def _na_attn_kernel(q_ref, k_ref, v_ref, bias_ref, o_ref, vext_ref, s0, s1, m0, m1, p0, p1, *, heads, rows):
    nblk = rows // NA_QROWS

    def win_start(blk):
        return pl.multiple_of(_na_window_row(blk, rows) * GRID_W, GRID_W)

    def bias(h, blk):
        cls = (blk > 0).astype(jnp.int32) + (blk == nblk - 1).astype(jnp.int32) if not isinstance(blk, int) \
            else int(blk > 0) + int(blk == nblk - 1)
        return bias_ref[h, cls]

    _attn_pipeline(q_ref, k_ref, v_ref, o_ref, vext_ref, (s0, s1), (m0, m1), (p0, p1), heads=heads, group=1,
                   win=NA_WIN, win_start=win_start, bias=bias)


def na_bias_table(rpb, rows):
    H = rpb.shape[0]
    col = np.arange(GRID_W)
    c0 = np.clip(col - NA_COLS // 2, 0, GRID_W - NA_COLS)
    col_ok = (col[None, :] >= c0[:, None]) & (col[None, :] < c0[:, None] + NA_COLS)
    dcol = np.clip(col[None, :] - col[:, None] + NA_COLS - 1, 0, 2 * NA_COLS - 2)
    pick = ((dcol[:, :, None] == np.arange(2 * NA_COLS - 1)) & col_ok[:, :, None]).astype(np.float32)
    t = jnp.einsum("hrd,qkd->hrqk", rpb.astype(F32), jnp.asarray(pick), precision=lax.Precision.HIGHEST)
    t = jnp.where(jnp.asarray(col_ok)[None, None], t * LOG2E, -jnp.inf)
    nblk = rows // NA_QROWS

    def geometry(blk):
        r = blk * NA_QROWS + np.arange(NA_QROWS)
        kr = _na_window_row(blk, rows) + np.arange(NA_KROWS)
        r0 = np.clip(r - NA_ROWS // 2, 0, rows - NA_ROWS)
        valid = (kr[None, :] >= r0[:, None]) & (kr[None, :] < r0[:, None] + NA_ROWS)
        drow = np.clip(kr[None, :] - r[:, None] + NA_ROWS - 1, 0, 2 * NA_ROWS - 2)
        return valid, np.where(valid, drow, 0)

    inner = geometry(1)
    assert all(np.array_equal(a, b) for blk in range(1, nblk - 1) for a, b in zip(geometry(blk), inner))
    tables = []
    for blk in (0, 1, nblk - 1):
        valid, drow = geometry(blk)
        slab = jnp.where(jnp.asarray(valid)[None, :, :, None, None], t[:, drow], -jnp.inf)
        tables.append(slab.transpose(0, 1, 3, 2, 4).reshape(H, ATTN_TQ, NA_WIN))
    return jnp.stack(tables, axis=1)


def na_attention(qkv, bias, *, n_heads, heads, q0, k0, v0, batch, seq):
    T = qkv.shape[1]
    rows = seq // GRID_W
    assert n_heads % heads == 0 and q0 % heads == 0 and k0 % heads == 0 and v0 % heads == 0
    assert rows % NA_QROWS == 0 and rows // NA_QROWS >= 3
    qkv_spec = lambda h0: pl.BlockSpec((heads, seq, LANES), lambda b, g: (h0 // heads + g, b, 0))
    return pl.pallas_call(
        functools.partial(_na_attn_kernel, heads=heads, rows=rows),
        grid=(batch, n_heads // heads),
        in_specs=[qkv_spec(q0), qkv_spec(k0), qkv_spec(v0),
                  pl.BlockSpec((heads, 3, ATTN_TQ, NA_WIN), lambda b, g: (g, 0, 0, 0))],
        out_specs=pl.BlockSpec((heads, seq, LANES), lambda b, g: (g, b, 0)),
        out_shape=jax.ShapeDtypeStruct((n_heads, T, LANES), BF16),
        scratch_shapes=_attn_scratch(heads, seq, NA_WIN, ATTN_TQ),
        compiler_params=_params(2),
        name="na_attention",
    )(qkv, qkv, qkv, bias)


def _layer_norm(z, g, b):
    mu = jnp.mean(z, axis=-1, keepdims=True)
    zc = z - mu
    var = jnp.mean(zc * zc, axis=-1, keepdims=True)
    return zc * lax.rsqrt(var + LN_EPS) * g + b


def _top2_of4(v0, v1, v2, v3):
    a = jnp.maximum(v0, v1)
    b = jnp.minimum(v0, v1)
    c = jnp.maximum(v2, v3)
    d = jnp.minimum(v2, v3)
    return jnp.maximum(a, c), jnp.maximum(jnp.minimum(a, c), jnp.maximum(b, d))


def _route(logits_t, rb):
    G = N_GROUPS
    scores = jax.nn.sigmoid(logits_t)
    sel = scores + rb
    sj = [sel[j * G:(j + 1) * G] for j in range(EXPERTS_PER_GROUP)]
    cj = [scores[j * G:(j + 1) * G] for j in range(EXPERTS_PER_GROUP)]
    t1, t2 = _top2_of4(*sj)
    grp = t1 + t2
    gi = lax.broadcasted_iota(jnp.int32, grp.shape, 0).astype(F32)
    gbest = jnp.min(jnp.where(grp == jnp.max(grp, axis=0, keepdims=True), gi, float(G)), axis=0, keepdims=True)
    pick = gi == gbest
    v = [jnp.sum(jnp.where(pick, s, 0.0), axis=0, keepdims=True) for s in sj]
    c = [jnp.sum(jnp.where(pick, s, 0.0), axis=0, keepdims=True) for s in cj]
    neg = jnp.float32(-jnp.inf)
    m1 = jnp.maximum(jnp.maximum(v[0], v[1]), jnp.maximum(v[2], v[3]))
    j1 = jnp.where(v[0] == m1, 0, jnp.where(v[1] == m1, 1, jnp.where(v[2] == m1, 2, 3)))
    w = [jnp.where(j1 == j, neg, v[j]) for j in range(4)]
    m2 = jnp.maximum(jnp.maximum(w[0], w[1]), jnp.maximum(w[2], w[3]))
    j2 = jnp.where(w[0] == m2, 0, jnp.where(w[1] == m2, 1, jnp.where(w[2] == m2, 2, 3)))
    g1 = jnp.where(j1 == 0, c[0], jnp.where(j1 == 1, c[1], jnp.where(j1 == 2, c[2], c[3])))
    g2 = jnp.where(j2 == 0, c[0], jnp.where(j2 == 1, c[1], jnp.where(j2 == 2, c[2], c[3])))
    tot = g1 + g2
    e0 = gbest.astype(jnp.int32) * EXPERTS_PER_GROUP
    return (e0 + j1, e0 + j2), (g1 / tot, g2 / tot)


OUT_SUBTILES = 2


def _out_ln_kernel(a_ref, b_ref, w_ref, x_ref, g_ref, beta_ref, x1_ref, x1p_ref):
    sub = x_ref.shape[0] // OUT_SUBTILES
    half = x_ref.shape[1] // 2
    for t in range(OUT_SUBTILES):
        rows = slice(t * sub, (t + 1) * sub)
        heads = [a_ref[h, rows, :] for h in range(a_ref.shape[0])] + [b_ref[h, rows, :] for h in range(b_ref.shape[0])]
        mix = jnp.dot(jnp.concatenate(heads, axis=1), w_ref[...], preferred_element_type=F32)
        x1 = _layer_norm(ALPHA * x_ref[rows, :] + mix, g_ref[...], beta_ref[...])
        x1_ref[rows, :] = x1
        x1p_ref[rows, :] = _pack_bf16_pair(x1[:, :half], x1[:, half:])


def out_ln(a, b, w, x, g, beta, *, bm=512):
    T, D = x.shape
    return pl.pallas_call(
        _out_ln_kernel,
        grid=(T // bm,),
        in_specs=[pl.BlockSpec((a.shape[0], bm, LANES), lambda i: (0, i, 0)),
                  pl.BlockSpec((b.shape[0], bm, LANES), lambda i: (0, i, 0)),
                  _resident(w.shape),
                  pl.BlockSpec((bm, D), lambda i: (i, 0)),
                  _resident(g.shape), _resident(beta.shape)],
        out_specs=[pl.BlockSpec((bm, D), lambda i: (i, 0)), pl.BlockSpec((bm, D // 2), lambda i: (i, 0))],
        out_shape=[jax.ShapeDtypeStruct((T, D), F32), jax.ShapeDtypeStruct((T, D // 2), jnp.uint32)],
        compiler_params=_params(1),
        name="out_ln",
    )(a, b, w, x, g, beta)


def _route_kernel(x_ref, rwh_ref, rwl_ref, rb_ref, eid_ref, gate_ref):
    x = x_ref[...]
    hi = x.astype(BF16)
    lo = (x - hi.astype(F32)).astype(BF16)
    nt = (((1,), (1,)), ((), ()))
    logits_t = (lax.dot_general(rwh_ref[...], hi, nt, preferred_element_type=F32)
                + lax.dot_general(rwl_ref[...], hi, nt, preferred_element_type=F32)
                + lax.dot_general(rwh_ref[...], lo, nt, preferred_element_type=F32))
    eid, gate = _route(logits_t, rb_ref[...])
    for k in range(TOP_K):
        eid_ref[k:k + 1, :] = eid[k]
        gate_ref[k:k + 1, :] = gate[k]


def route(x1, rw_hi, rw_lo, rb, *, bm=1024):
    T, D = x1.shape
    return pl.pallas_call(
        _route_kernel,
        grid=(T // bm,),
        in_specs=[pl.BlockSpec((bm, D), lambda i: (i, 0)),
                  _resident(rw_hi.shape), _resident(rw_lo.shape), _resident(rb.shape)],
        out_specs=[pl.BlockSpec((TOP_K, bm), lambda i: (0, i)),
                   pl.BlockSpec((TOP_K, bm), lambda i: (0, i))],
        out_shape=[jax.ShapeDtypeStruct((TOP_K, T), jnp.int32),
                   jax.ShapeDtypeStruct((TOP_K, T), F32)],
        compiler_params=_params(1),
        name="route",
    )(x1, rw_hi, rw_lo, rb)


GATHER_DEPTH = 3


def _row_copy(src_hbm, row, dst_vmem, j, sem):
    return pltpu.make_async_copy(src_hbm.at[pl.ds(row, 1), :], dst_vmem.at[pl.ds(j, 1), :], sem)


def _gather_ring_step(start_gather, wait_gather, compute, active=None, idle=None):
    i = pl.program_id(0)
    n_steps = pl.num_programs(0)
    ahead = GATHER_DEPTH - 1
    slot = i % GATHER_DEPTH

    @pl.when(i == 0)
    def _():
        for a in range(ahead):
            start_gather(a, a)

    wait_gather(slot)
    more = i + ahead < n_steps
    cases = [(more, compute)] if active is None else [(more & active, compute), (more & ~active, idle)]
    for cond, work in cases:
        @pl.when(cond)
        def _(work=work):
            start_gather(i + ahead, (i + ahead) % GATHER_DEPTH)
            work(slot)
    cases = [(~more, compute)] if active is None else [(~more & active, compute), (~more & ~active, idle)]
    for cond, work in cases:
        @pl.when(cond)
        def _(work=work):
            work(slot)


def _pack_bf16_pair(lo, hi):
    lo_bits = pltpu.bitcast(lo.astype(BF16).astype(F32), jnp.uint32)
    hi_bits = pltpu.bitcast(hi.astype(BF16).astype(F32), jnp.uint32)
    return lax.shift_right_logical(lo_bits, jnp.uint32(16)) | (hi_bits & jnp.uint32(0xFFFF0000))


def _unpack_bf16_pair(word):
    lo = pltpu.bitcast(lax.shift_left(word, jnp.uint32(16)), F32)
    hi = pltpu.bitcast(word & jnp.uint32(0xFFFF0000), F32)
    return lo, hi


def _moe_ffn_kernel(blk_lo_ref, blk_hi_ref, nblk_ref, tok_ref, x_hbm,
                    wg_lo, wu_lo, wd_lo, wg_hi, wu_hi, wd_hi, y_ref, xbuf, sem):
    def start_gather(blk, s):
        base = blk * MOE_BLOCK
        for j in range(MOE_BLOCK):
            _row_copy(x_hbm, tok_ref[base + j], xbuf.at[s], j, sem.at[s]).start()

    def wait_gather(s):
        pltpu.make_async_copy(x_hbm.at[pl.ds(0, MOE_BLOCK), :], xbuf.at[s], sem.at[s]).wait()

    def ffn(s):
        x_left, x_right = _unpack_bf16_pair(xbuf[s])
        xb = jnp.concatenate([x_left.astype(BF16), x_right.astype(BF16)], axis=1)
        ys = []
        for wg, wu, wd in ((wg_lo, wu_lo, wd_lo), (wg_hi, wu_hi, wd_hi)):
            hg = jnp.dot(xb, wg[...], preferred_element_type=F32)
            hu = jnp.dot(xb, wu[...], preferred_element_type=F32)
            hb = (jax.nn.silu(hg) * hu).astype(BF16)
            ys.append(jnp.dot(hb, wd[...], preferred_element_type=F32))
        y_ref[...] = _pack_bf16_pair(ys[0], ys[1])

    def no_tokens(s):
        y_ref[...] = jnp.zeros(y_ref.shape, y_ref.dtype)

    _gather_ring_step(start_gather, wait_gather, ffn, active=pl.program_id(0) < nblk_ref[0], idle=no_tokens)


def moe_ffn(x1p, w_gate, w_up, w_down, blk_lo, blk_hi, nblk, row_tok):
    D = 2 * x1p.shape[1]
    n_blocks = blk_lo.shape[0]
    assert n_blocks >= GATHER_DEPTH
    n_rows = n_blocks * MOE_BLOCK
    w_in = lambda which: pl.BlockSpec((None, D, D_EXPERT), lambda i, lo, hi, nb, tk: ((lo, hi)[which][i], 0, 0))
    w_out = lambda which: pl.BlockSpec((None, D_EXPERT, D), lambda i, lo, hi, nb, tk: ((lo, hi)[which][i], 0, 0))
    grid_spec = pltpu.PrefetchScalarGridSpec(
        num_scalar_prefetch=4,
        grid=(n_blocks,),
        in_specs=[pl.BlockSpec(memory_space=pl.ANY), w_in(0), w_in(0), w_out(0), w_in(1), w_in(1), w_out(1)],
        out_specs=pl.BlockSpec((MOE_BLOCK, D), lambda i, lo, hi, nb, tk: (i, 0)),
        scratch_shapes=[pltpu.VMEM((GATHER_DEPTH, MOE_BLOCK, D // 2), jnp.uint32),
                        pltpu.SemaphoreType.DMA((GATHER_DEPTH,))],
    )
    return pl.pallas_call(
        _moe_ffn_kernel,
        grid_spec=grid_spec,
        out_shape=jax.ShapeDtypeStruct((n_rows, D), jnp.uint32),
        compiler_params=_params(1),
        name="moe_ffn",
    )(blk_lo, blk_hi, nblk, row_tok, x1p, w_gate, w_up, w_down, w_gate, w_up, w_down)


def _combine_ln_kernel(dest_ref, y_hbm, x1_ref, gate_ref, g_ref, beta_ref, x2_ref, x2b_ref, ybuf, sem, *, bm):
    def start_gather(tile, s):
        base = tile * bm
        for j in range(bm):
            _row_copy(y_hbm, dest_ref[base + j], ybuf.at[s], j, sem.at[s]).start()

    def wait_gather(s):
        pltpu.make_async_copy(y_hbm.at[pl.ds(0, bm), :], ybuf.at[s], sem.at[s]).wait()

    def finish(s):
        gate = gate_ref[...]
        y_lo, y_hi = _unpack_bf16_pair(ybuf[s])
        moe = y_lo * gate[:, 0:1] + y_hi * gate[:, 1:2]
        x2 = _layer_norm(ALPHA * x1_ref[...] + moe, g_ref[...], beta_ref[...])
        x2_ref[...] = x2
        x2b_ref[...] = x2.astype(BF16)

    _gather_ring_step(start_gather, wait_gather, finish)


def combine_ln(y_rows, dest, x1, gate_t, g, beta, *, bm=256):
    T, D = x1.shape
    assert T // bm >= GATHER_DEPTH
    grid_spec = pltpu.PrefetchScalarGridSpec(
        num_scalar_prefetch=1,
        grid=(T // bm,),
        in_specs=[pl.BlockSpec(memory_space=pl.ANY),
                  pl.BlockSpec((bm, D), lambda i, d: (i, 0)),
                  pl.BlockSpec((bm, TOP_K), lambda i, d: (i, 0)),
                  pl.BlockSpec((1, D), lambda i, d: (0, 0)),
                  pl.BlockSpec((1, D), lambda i, d: (0, 0))],
        out_specs=[pl.BlockSpec((bm, D), lambda i, d: (i, 0)),
                   pl.BlockSpec((bm, D), lambda i, d: (i, 0))],
        scratch_shapes=[pltpu.VMEM((GATHER_DEPTH, bm, D), jnp.uint32), pltpu.SemaphoreType.DMA((GATHER_DEPTH,))],
    )
    return pl.pallas_call(
        functools.partial(_combine_ln_kernel, bm=bm),
        grid_spec=grid_spec,
        out_shape=[jax.ShapeDtypeStruct((T, D), F32), jax.ShapeDtypeStruct((T, D), BF16)],
        compiler_params=_params(1),
        name="combine_ln",
    )(dest, y_rows, x1, gate_t, g, beta)


_PAIR_LO = np.array([a for a in range(EXPERTS_PER_GROUP) for b in range(a + 1, EXPERTS_PER_GROUP)], np.int32)
_PAIR_HI = np.array([b for a in range(EXPERTS_PER_GROUP) for b in range(a + 1, EXPERTS_PER_GROUP)], np.int32)
N_PAIRS = len(_PAIR_LO)
N_CLASSES = N_GROUPS * N_PAIRS


def moe_plan(eid, gate):
    T = eid.shape[1]
    n_blocks = (T + N_CLASSES * (MOE_BLOCK - 1) + MOE_BLOCK - 1) // MOE_BLOCK
    first_lower = eid[0] < eid[1]
    e_lo = jnp.minimum(eid[0], eid[1])
    e_hi = jnp.maximum(eid[0], eid[1])
    gate_t = jnp.stack([jnp.where(first_lower, gate[0], gate[1]), jnp.where(first_lower, gate[1], gate[0])], axis=1)
    a = e_lo % EXPERTS_PER_GROUP
    b = e_hi % EXPERTS_PER_GROUP
    pair = a * (2 * EXPERTS_PER_GROUP - 1 - a) // 2 + (b - a - 1)
    cls = (e_lo // EXPERTS_PER_GROUP) * N_PAIRS + pair
    onehot = (cls[:, None] == jnp.arange(N_CLASSES, dtype=jnp.int32)[None, :]).astype(jnp.int32)
    csum = jnp.cumsum(onehot, axis=0)
    rank = jnp.sum(csum * onehot, axis=1) - 1
    counts = csum[-1]
    pcounts = (counts + MOE_BLOCK - 1) // MOE_BLOCK * MOE_BLOCK
    pends = jnp.cumsum(pcounts)
    pstarts = pends - pcounts
    dest = (pstarts[cls] + rank).astype(jnp.int32)
    pad_tok = jnp.arange(n_blocks * MOE_BLOCK, dtype=jnp.int32) % T
    row_tok = pad_tok.at[dest].set(jnp.arange(T, dtype=jnp.int32))
    blk_start = jnp.arange(n_blocks, dtype=jnp.int32) * MOE_BLOCK
    blk_cls = jnp.minimum(jnp.sum((pends[None, :] <= blk_start[:, None]).astype(jnp.int32), axis=1), N_CLASSES - 1)
    blk_base = (blk_cls // N_PAIRS) * EXPERTS_PER_GROUP
    blk_lo = (blk_base + jnp.asarray(_PAIR_LO)[blk_cls % N_PAIRS]).astype(jnp.int32)
    blk_hi = (blk_base + jnp.asarray(_PAIR_HI)[blk_cls % N_PAIRS]).astype(jnp.int32)
    nblk = (pends[-1:] // MOE_BLOCK).astype(jnp.int32)
    return dest, row_tok, blk_lo, blk_hi, nblk, gate_t


def moe_layer(x1, x1p, eid, gate, w_gate, w_up, w_down, ln_g, ln_b):
    dest, row_tok, blk_lo, blk_hi, nblk, gate_t = moe_plan(eid, gate)
    y_rows = moe_ffn(x1p, w_gate.astype(BF16), w_up.astype(BF16), w_down.astype(BF16), blk_lo, blk_hi, nblk, row_tok)
    return combine_ln(y_rows, dest, x1, gate_t, ln_g.reshape(1, -1), ln_b.reshape(1, -1))


def _rope_cos_sin(pos, dim):
    inv_freq = ROPE_THETA ** (-jnp.arange(0, dim, 2, dtype=F32) / dim)
    ang = pos.astype(F32)[:, None] * inv_freq[None, :]
    cos = jnp.cos(ang)
    sin = jnp.sin(ang)
    return jnp.concatenate([cos, cos], axis=-1), jnp.concatenate([-sin, sin], axis=-1)


def axial_rope_tables(seq):
    pos = jnp.arange(seq)
    half = HEAD_DIM // 2
    cr, sr = _rope_cos_sin(pos // GRID_W, half)
    cc, sc = _rope_cos_sin(pos % GRID_W, half)
    return jnp.concatenate([cr, cc], axis=-1), jnp.concatenate([sr, sc], axis=-1)


def mla_rope_tables(seq):
    c, s = _rope_cos_sin(jnp.arange(seq), D_ROPE)
    z = jnp.zeros((seq, LANES - D_ROPE), F32)
    return jnp.concatenate([c, z], axis=-1), jnp.concatenate([s, z], axis=-1)


def router_layout(router_w, router_b):
    perm = np.array([g * EXPERTS_PER_GROUP + j for j in range(EXPERTS_PER_GROUP) for g in range(N_GROUPS)])
    rw = router_w.astype(F32).T[perm]
    hi = rw.astype(BF16)
    lo = (rw - hi.astype(F32)).astype(BF16)
    return hi, lo, router_b.astype(F32)[perm].reshape(N_EXPERTS, 1)


def mla_up_layout(w_q_up, w_kv_up, w_in_tail):
    rq = w_q_up.shape[0]
    wq = w_q_up.reshape(rq, D_HEADS, D_NOPE + D_ROPE)
    wq = jnp.pad(wq, ((0, 0), (0, 0), (0, 2 * LANES - (D_NOPE + D_ROPE)))).reshape(rq, D_HEADS * 2 * LANES)
    wc = jnp.pad(w_in_tail, ((0, 0), (0, LANES - D_ROPE)))
    return wq.astype(BF16), w_kv_up.astype(BF16), wc.astype(BF16)


def kernel(x, router_w, router_b, l0_w_in, l0_a_q_norm, l0_a_k_norm, l0_w_out, l0_ln1_g, l0_ln1_b, l0_w_gate, l0_w_up, l0_w_down, l0_ln2_g, l0_ln2_b, l1_w_in, l1_c_rpb, l1_d_q_norm, l1_d_w_q_up, l1_d_kv_norm, l1_d_w_kv_up, l1_w_out, l1_ln1_g, l1_ln1_b, l1_w_gate, l1_w_up, l1_w_down, l1_ln2_g, l1_ln2_b):
    batch, seq, d = x.shape
    T = batch * seq
    xf = x.reshape(T, d)
    rw_hi, rw_lo, rb = router_layout(router_w, router_b)
    row = lambda v: v.astype(F32).reshape(1, -1)

    a_q_dim = A_HEADS * HEAD_DIM
    a_qk_dim = a_q_dim + A_KV_HEADS * HEAD_DIM
    w0 = l0_w_in.astype(BF16)
    qk_scale = LOG2E * HEAD_DIM ** -0.5
    gain = jnp.concatenate([jnp.tile(l0_a_q_norm.astype(F32), A_HEADS) * qk_scale,
                            jnp.tile(l0_a_k_norm.astype(F32), A_KV_HEADS)]).reshape(1, -1)
    cos_a, sin_a = axial_rope_tables(seq)
    qk_a = proj_norm_rope(xf, w0[:, :a_qk_dim], gain, cos_a, sin_a, seq=seq)
    nb = A_KV_HEADS
    q_scale = jnp.full((B_HEADS * HEAD_DIM,), qk_scale, F32)
    scale0 = jnp.concatenate([jnp.ones((nb * HEAD_DIM,), F32), q_scale, jnp.ones((2 * B_HEADS * HEAD_DIM,), F32)])
    rest0 = proj_heads(xf, w0[:, a_qk_dim:], scale0.reshape(1, -1), heads_per_tile=13)
    group = A_HEADS // A_KV_HEADS
    oa = dense_attention(qk_a, qk_a, rest0, n_heads=A_HEADS, heads=group, k0=A_HEADS, v0=0, batch=batch, seq=seq,
                         group=group)
    ob = dilated_attention(rest0, dilated_bias_table(B_HEADS), n_heads=B_HEADS, heads=1, q0=nb, k0=nb + B_HEADS,
                           v0=nb + 2 * B_HEADS, batch=batch, seq=seq)
    x1, x1p = out_ln(oa, ob, l0_w_out.astype(BF16), xf, row(l0_ln1_g), row(l0_ln1_b))
    eid, gate = route(x1, rw_hi, rw_lo, rb)
    x2, x2b = moe_layer(x1, x1p, eid, gate, l0_w_gate, l0_w_up, l0_w_down, l0_ln2_g, l0_ln2_b)

    c_dim = C_HEADS * HEAD_DIM
    w1 = l1_w_in
    scale1 = jnp.concatenate([jnp.full((c_dim,), qk_scale, F32), jnp.ones((2 * c_dim,), F32)])
    qkv_c = proj_heads(x2b, w1[:, :3 * c_dim].astype(BF16), scale1.reshape(1, -1), heads_per_tile=12)
    oc = na_attention(qkv_c, na_bias_table(l1_c_rpb, seq // GRID_W), n_heads=C_HEADS, heads=2, q0=0, k0=C_HEADS,
                      v0=2 * C_HEADS, batch=batch, seq=seq)
    wq, wkv, wc = mla_up_layout(l1_d_w_q_up, l1_d_w_kv_up, w1[:, 3 * c_dim:])
    cos_d, sin_d = mla_rope_tables(seq)
    qd, kd, vd = mla_proj(x2b, wc, row(l1_d_q_norm), row(l1_d_kv_norm), wq, wkv, cos_d, sin_d, seq=seq,
                          scale=LOG2E * (D_NOPE + D_ROPE) ** -0.5)
    od = dense_attention(qd, kd, vd, n_heads=D_HEADS, heads=1, k0=0, v0=0, batch=batch, seq=seq, group=1)
    x3, x3p = out_ln(oc, od, l1_w_out.astype(BF16), x2, row(l1_ln1_g), row(l1_ln1_b))
    eid, gate = route(x3, rw_hi, rw_lo, rb)
    x4, _ = moe_layer(x3, x3p, eid, gate, l1_w_gate, l1_w_up, l1_w_down, l1_ln2_g, l1_ln2_b)
    return x4.reshape(batch, seq, d)
```

```python
import functools
import math

import jax
import jax.numpy as jnp
import numpy as np
from jax import lax
from jax.experimental import pallas as pl
from jax.experimental.pallas import tpu as pltpu

D_MODEL = 2048
DEPTH = 2
GRID_W = 64
HEAD_DIM = 128
ROPE_THETA = 10000.0
RMS_EPS = 1e-6
LN_EPS = 1e-5

A_HEADS = 8
A_KV_HEADS = 2
B_HEADS = 8
B_BRANCHES = ((128, 1), (512, 4), (2048, 16))
C_HEADS = 8
NA_ROWS = 8
NA_COLS = 16
D_HEADS = 8
D_Q_RANK = 512
D_KV_RANK = 256
D_NOPE = 128
D_ROPE = 64
D_V = 128

N_EXPERTS = 32
N_GROUPS = 8
EXPERTS_PER_GROUP = N_EXPERTS // N_GROUPS
TOP_K = 2
D_EXPERT = 512
MOE_BLOCK = 256

ALPHA = (2.0 * DEPTH) ** 0.25
LOG2E = math.log2(math.e)

LANES = 128
VMEM_LIMIT = 56 * 1024 * 1024

BF16 = jnp.bfloat16
F32 = jnp.float32


def _params(n_axes, **flags):
    return pltpu.CompilerParams(dimension_semantics=("arbitrary",) * n_axes, vmem_limit_bytes=VMEM_LIMIT,
                                flags=flags or None)


def _rot_half32(y):
    lane = lax.broadcasted_iota(jnp.int32, y.shape, y.ndim - 1)
    up = pltpu.roll(y, 96, axis=y.ndim - 1)
    dn = pltpu.roll(y, 32, axis=y.ndim - 1)
    return jnp.where((lane % 64) < 32, up, dn)


def _proj_heads_kernel(x_ref, w_ref, c_ref, o_ref, *, heads_per_tile):
    acc = jnp.dot(x_ref[...].astype(BF16), w_ref[...], preferred_element_type=F32) * c_ref[...]
    for h in range(heads_per_tile):
        o_ref[h] = acc[:, h * LANES:(h + 1) * LANES].astype(o_ref.dtype)


def proj_heads(x, w, col_scale, *, heads_per_tile, bm=512):
    T, D = x.shape
    H = w.shape[1] // LANES
    bn = heads_per_tile * LANES
    grid = (H // heads_per_tile, T // bm)
    return pl.pallas_call(
        functools.partial(_proj_heads_kernel, heads_per_tile=heads_per_tile),
        grid=grid,
        in_specs=[pl.BlockSpec((bm, D), lambda j, i: (i, 0)),
                  pl.BlockSpec((D, bn), lambda j, i: (0, j)),
                  pl.BlockSpec((1, bn), lambda j, i: (0, j))],
        out_specs=pl.BlockSpec((heads_per_tile, bm, LANES), lambda j, i: (j, i, 0)),
        out_shape=jax.ShapeDtypeStruct((H, T, LANES), BF16),
        compiler_params=_params(2),
        name="proj_heads",
    )(x, w, col_scale)


def _resident(shape):
    return pl.BlockSpec(shape, lambda *_: (0,) * len(shape), pipeline_mode=pl.Buffered(1))


PROJ_SUBTILES = 2


def _proj_norm_rope_kernel(x_ref, w_ref, g_ref, cos_ref, sin_ref, o_ref):
    sub = x_ref.shape[0] // PROJ_SUBTILES
    for t in range(PROJ_SUBTILES):
        rows = slice(t * sub, (t + 1) * sub)
        acc = jnp.dot(x_ref[rows, :].astype(BF16), w_ref[...], preferred_element_type=F32)
        cos = cos_ref[rows, :]
        sin = sin_ref[rows, :]
        for h in range(o_ref.shape[0]):
            a = acc[:, h * LANES:(h + 1) * LANES]
            ms = jnp.mean(a * a, axis=-1, keepdims=True)
            y = a * lax.rsqrt(ms + RMS_EPS) * g_ref[:, h * LANES:(h + 1) * LANES]
            o_ref[h, rows, :] = (y * cos + _rot_half32(y) * sin).astype(o_ref.dtype)


def proj_norm_rope(x, w, gain, cos, sin, *, seq, bm=512):
    T, D = x.shape
    H = w.shape[1] // LANES
    nsb = seq // bm
    return pl.pallas_call(
        _proj_norm_rope_kernel,
        grid=(T // bm,),
        in_specs=[pl.BlockSpec((bm, D), lambda i: (i, 0)),
                  _resident(w.shape), _resident(gain.shape),
                  pl.BlockSpec((bm, LANES), lambda i: (i % nsb, 0)),
                  pl.BlockSpec((bm, LANES), lambda i: (i % nsb, 0))],
        out_specs=pl.BlockSpec((H, bm, LANES), lambda i: (0, i, 0)),
        out_shape=jax.ShapeDtypeStruct((H, T, LANES), BF16),
        compiler_params=_params(1),
        name="proj_norm_rope",
    )(x, w, gain, cos, sin)


def _mla_proj_kernel(x_ref, wc_ref, gq_ref, gkv_ref, wq_ref, wkv_ref, cos_ref, sin_ref,
                     q_ref, k_ref, v_ref, *, scale):
    def rms(t, g):
        ms = jnp.mean(t * t, axis=-1, keepdims=True)
        return t * lax.rsqrt(ms + RMS_EPS) * g

    sub = x_ref.shape[0] // PROJ_SUBTILES
    for t in range(PROJ_SUBTILES):
        rows = slice(t * sub, (t + 1) * sub)
        c = jnp.dot(x_ref[rows, :], wc_ref[...], preferred_element_type=F32)
        cos = cos_ref[rows, :]
        sin = sin_ref[rows, :]
        cq = rms(c[:, :D_Q_RANK], gq_ref[...]).astype(BF16)
        ckv = rms(c[:, D_Q_RANK:D_Q_RANK + D_KV_RANK], gkv_ref[...]).astype(BF16)
        kr = c[:, D_Q_RANK + D_KV_RANK:]
        kr = (kr * cos + _rot_half32(kr) * sin).astype(k_ref.dtype)
        q = jnp.dot(cq, wq_ref[...], preferred_element_type=F32) * scale
        kv = jnp.dot(ckv, wkv_ref[...], preferred_element_type=F32)
        for h in range(D_HEADS):
            base = h * 2 * LANES
            q_ref[h, rows, :LANES] = q[:, base:base + LANES].astype(q_ref.dtype)
            qr = q[:, base + LANES:base + 2 * LANES]
            q_ref[h, rows, LANES:] = (qr * cos + _rot_half32(qr) * sin).astype(q_ref.dtype)
            k_ref[h, rows, :LANES] = kv[:, base:base + LANES].astype(k_ref.dtype)
            k_ref[h, rows, LANES:] = kr
            v_ref[h, rows, :] = kv[:, base + LANES:base + 2 * LANES].astype(v_ref.dtype)


def mla_proj(x, wc, gq, gkv, wq, wkv, cos, sin, *, seq, scale, bm=512):
    T, D = x.shape
    nsb = seq // bm
    return pl.pallas_call(
        functools.partial(_mla_proj_kernel, scale=scale),
        grid=(T // bm,),
        in_specs=[pl.BlockSpec((bm, D), lambda i: (i, 0)),
                  _resident(wc.shape), _resident(gq.shape), _resident(gkv.shape), _resident(wq.shape),
                  _resident(wkv.shape),
                  pl.BlockSpec((bm, LANES), lambda i: (i % nsb, 0)),
                  pl.BlockSpec((bm, LANES), lambda i: (i % nsb, 0))],
        out_specs=[pl.BlockSpec((D_HEADS, bm, 2 * LANES), lambda i: (0, i, 0)),
                   pl.BlockSpec((D_HEADS, bm, 2 * LANES), lambda i: (0, i, 0)),
                   pl.BlockSpec((D_HEADS, bm, LANES), lambda i: (0, i, 0))],
        out_shape=[jax.ShapeDtypeStruct((D_HEADS, T, 2 * LANES), BF16),
                   jax.ShapeDtypeStruct((D_HEADS, T, 2 * LANES), BF16),
                   jax.ShapeDtypeStruct((D_HEADS, T, LANES), BF16)],
        compiler_params=_params(1),
        name="mla_proj",
    )(x, wc, gq, gkv, wq, wkv, cos, sin)


DENSE_TQ = 512
ATTN_TQ = 256


def _attn_pipeline(q_ref, k_ref, v_ref, o_ref, vext_ref, s_buf, m_buf, p_buf, *, heads, group, win, win_start, bias):
    tq = s_buf[0].shape[0]
    S = q_ref.shape[1]
    nblk = S // tq
    total = heads * nblk
    vext_ref[:, :, :LANES] = v_ref[...]
    vext_ref[:, :, LANES:] = jnp.ones(v_ref.shape, vext_ref.dtype)

    def locate(n):
        h, blk = (0, n) if heads == 1 else (n // nblk, n % nblk)
        rows = pl.ds(blk * tq, tq) if isinstance(blk, int) else pl.ds(pl.multiple_of(blk * tq, tq), tq)
        keys = slice(None) if win == S else pl.ds(win_start(blk), win)
        return h, blk, rows, keys

    def scores(n, par):
        h, blk, rows, keys = locate(n)
        s = lax.dot_general(q_ref[h, rows, :], k_ref[h // group, keys, :], (((1,), (1,)), ((), ())),
                            preferred_element_type=F32)
        if bias is not None:
            s = s + bias(h, blk)
        s_buf[par][...] = s
        m_buf[par][...] = jnp.max(s, axis=-1, keepdims=True)

    def exps(par):
        p_buf[par][...] = jnp.exp2(s_buf[par][...] - m_buf[par][...]).astype(BF16)

    def wsum(n, par):
        h, blk, rows, keys = locate(n)
        o_ext = jnp.dot(p_buf[par][...], vext_ref[h // group, keys, :], preferred_element_type=F32)
        o_ref[h, rows, :] = (o_ext[:, :LANES] / o_ext[:, LANES:]).astype(o_ref.dtype)

    scores(0, 0)
    exps(0)
    scores(1, 1)

    def step(j, carry):
        n = 2 * j + 1
        exps(1)
        wsum(n - 1, 0)
        scores(n + 1, 0)
        exps(0)
        wsum(n, 1)
        scores(n + 2, 1)
        return carry

    lax.fori_loop(0, total // 2 - 1, step, 0)
    exps(1)
    wsum(total - 2, 0)
    wsum(total - 1, 1)


def _attn_scratch(kv_heads, seq, win, tq):
    return [pltpu.VMEM((kv_heads, seq, 2 * LANES), BF16),
            pltpu.VMEM((tq, win), F32), pltpu.VMEM((tq, win), F32),
            pltpu.VMEM((tq, 1), F32), pltpu.VMEM((tq, 1), F32),
            pltpu.VMEM((tq, win), BF16), pltpu.VMEM((tq, win), BF16)]


def _dense_attn_kernel(q_ref, k_ref, v_ref, o_ref, vext_ref, s0, s1, m0, m1, p0, p1, *, heads, group):
    _attn_pipeline(q_ref, k_ref, v_ref, o_ref, vext_ref, (s0, s1), (m0, m1), (p0, p1), heads=heads, group=group,
                   win=q_ref.shape[1], win_start=None, bias=None)


def dense_attention(q, k, v, *, n_heads, heads, k0, v0, batch, seq, group):
    T, dq = q.shape[1:]
    kvh = heads // group
    assert n_heads % heads == 0 and heads % group == 0 and k0 % kvh == 0 and v0 % kvh == 0
    return pl.pallas_call(
        functools.partial(_dense_attn_kernel, heads=heads, group=group),
        grid=(batch, n_heads // heads),
        in_specs=[pl.BlockSpec((heads, seq, dq), lambda b, g: (g, b, 0)),
                  pl.BlockSpec((kvh, seq, dq), lambda b, g: (k0 // kvh + g, b, 0)),
                  pl.BlockSpec((kvh, seq, LANES), lambda b, g: (v0 // kvh + g, b, 0))],
        out_specs=pl.BlockSpec((heads, seq, LANES), lambda b, g: (g, b, 0)),
        out_shape=jax.ShapeDtypeStruct((n_heads, T, LANES), BF16),
        scratch_shapes=_attn_scratch(kvh, seq, seq, DENSE_TQ),
        compiler_params=_params(2),
        name="dense_attention",
    )(q, k, v)


DIL_REACH = max(w // 2 for w, _ in B_BRANCHES)
DIL_WIN = 2 * DIL_REACH + ATTN_TQ
DIL_TABLE_TILES = (DIL_WIN + 2 * DIL_REACH) // LANES


def _dilated_attn_kernel(q_ref, k_ref, v_ref, bias_ref, o_ref, vext_ref, s0, s1, m0, m1, p0, p1, *, heads):
    S = q_ref.shape[1]

    def win_start(blk):
        return pl.multiple_of(jnp.clip(blk * ATTN_TQ - DIL_REACH, 0, S - DIL_WIN), ATTN_TQ)

    def bias(h, blk):
        tile0 = (win_start(blk) - blk * ATTN_TQ + 2 * DIL_REACH) // LANES
        return jnp.concatenate([bias_ref[h, tile0 + t] for t in range(DIL_WIN // LANES)], axis=1)

    _attn_pipeline(q_ref, k_ref, v_ref, o_ref, vext_ref, (s0, s1), (m0, m1), (p0, p1), heads=heads, group=1,
                   win=DIL_WIN, win_start=win_start, bias=bias)


def dilated_bias_table(n_heads):
    col = (np.arange(DIL_TABLE_TILES)[:, None, None] * LANES + np.arange(LANES)[None, None, :])
    d = np.abs(col - np.arange(ATTN_TQ)[None, :, None] - 2 * DIL_REACH)
    mult = np.zeros(d.shape, np.float64)
    for window, dil in B_BRANCHES:
        mult += ((d % dil == 0) & (d <= window // 2)).astype(np.float64)
    with np.errstate(divide="ignore"):
        logm = np.log(mult)
    slopes = jnp.asarray(2.0 ** (-8.0 * np.arange(1, n_heads + 1) / n_heads), dtype=F32)
    bias = jnp.asarray(logm, dtype=F32)[None] - slopes[:, None, None, None] * jnp.asarray(d, dtype=F32)[None]
    return bias * LOG2E


def dilated_attention(qkv, bias, *, n_heads, heads, q0, k0, v0, batch, seq):
    T = qkv.shape[1]
    assert n_heads % heads == 0 and q0 % heads == 0 and k0 % heads == 0 and v0 % heads == 0
    qkv_spec = lambda h0: pl.BlockSpec((heads, seq, LANES), lambda b, g: (h0 // heads + g, b, 0))
    return pl.pallas_call(
        functools.partial(_dilated_attn_kernel, heads=heads),
        grid=(batch, n_heads // heads),
        in_specs=[qkv_spec(q0), qkv_spec(k0), qkv_spec(v0),
                  pl.BlockSpec((heads, DIL_TABLE_TILES, ATTN_TQ, LANES), lambda b, g: (g, 0, 0, 0))],
        out_specs=pl.BlockSpec((heads, seq, LANES), lambda b, g: (g, b, 0)),
        out_shape=jax.ShapeDtypeStruct((n_heads, T, LANES), BF16),
        scratch_shapes=_attn_scratch(heads, seq, DIL_WIN, ATTN_TQ),
        compiler_params=_params(2),
        name="dilated_attention",
    )(qkv, qkv, qkv, bias)


NA_QROWS = ATTN_TQ // GRID_W
NA_KROWS = NA_QROWS + NA_ROWS
NA_WIN = NA_KROWS * GRID_W


def _na_window_row(blk, rows):
    lo = blk * NA_QROWS - NA_ROWS // 2
    return jnp.clip(lo, 0, rows - NA_KROWS) if not isinstance(blk, int) else min(max(lo, 0), rows - NA_KROWS)


def _na_attn_kernel(q_ref, k_ref, v_ref, bias_ref, o_ref, vext_ref, s0, s1, m0, m1, p0, p1, *, heads, rows):
    nblk = rows // NA_QROWS

    def win_start(blk):
        return pl.multiple_of(_na_window_row(blk, rows) * GRID_W, GRID_W)

    def bias(h, blk):
        cls = (blk > 0).astype(jnp.int32) + (blk == nblk - 1).astype(jnp.int32) if not isinstance(blk, int) \
            else int(blk > 0) + int(blk == nblk - 1)
        return bias_ref[h, cls]

    _attn_pipeline(q_ref, k_ref, v_ref, o_ref, vext_ref, (s0, s1), (m0, m1), (p0, p1), heads=heads, group=1,
                   win=NA_WIN, win_start=win_start, bias=bias)


def na_bias_table(rpb, rows):
    H = rpb.shape[0]
    col = np.arange(GRID_W)
    c0 = np.clip(col - NA_COLS // 2, 0, GRID_W - NA_COLS)
    col_ok = (col[None, :] >= c0[:, None]) & (col[None, :] < c0[:, None] + NA_COLS)
    dcol = np.clip(col[None, :] - col[:, None] + NA_COLS - 1, 0, 2 * NA_COLS - 2)
    pick = ((dcol[:, :, None] == np.arange(2 * NA_COLS - 1)) & col_ok[:, :, None]).astype(np.float32)
    t = jnp.einsum("hrd,qkd->hrqk", rpb.astype(F32), jnp.asarray(pick), precision=lax.Precision.HIGHEST)
    t = jnp.where(jnp.asarray(col_ok)[None, None], t * LOG2E, -jnp.inf)
    nblk = rows // NA_QROWS

    def geometry(blk):
        r = blk * NA_QROWS + np.arange(NA_QROWS)
        kr = _na_window_row(blk, rows) + np.arange(NA_KROWS)
        r0 = np.clip(r - NA_ROWS // 2, 0, rows - NA_ROWS)
        valid = (kr[None, :] >= r0[:, None]) & (kr[None, :] < r0[:, None] + NA_ROWS)
        drow = np.clip(kr[None, :] - r[:, None] + NA_ROWS - 1, 0, 2 * NA_ROWS - 2)
        return valid, np.where(valid, drow, 0)

    inner = geometry(1)
    assert all(np.array_equal(a, b) for blk in range(1, nblk - 1) for a, b in zip(geometry(blk), inner))
    tables = []
    for blk in (0, 1, nblk - 1):
        valid, drow = geometry(blk)
        slab = jnp.where(jnp.asarray(valid)[None, :, :, None, None], t[:, drow], -jnp.inf)
        tables.append(slab.transpose(0, 1, 3, 2, 4).reshape(H, ATTN_TQ, NA_WIN))
    return jnp.stack(tables, axis=1)


def na_attention(qkv, bias, *, n_heads, heads, q0, k0, v0, batch, seq):
    T = qkv.shape[1]
    rows = seq // GRID_W
    assert n_heads % heads == 0 and q0 % heads == 0 and k0 % heads == 0 and v0 % heads == 0
    assert rows % NA_QROWS == 0 and rows // NA_QROWS >= 3
    qkv_spec = lambda h0: pl.BlockSpec((heads, seq, LANES), lambda b, g: (h0 // heads + g, b, 0))
    return pl.pallas_call(
        functools.partial(_na_attn_kernel, heads=heads, rows=rows),
        grid=(batch, n_heads // heads),
        in_specs=[qkv_spec(q0), qkv_spec(k0), qkv_spec(v0),
                  pl.BlockSpec((heads, 3, ATTN_TQ, NA_WIN), lambda b, g: (g, 0, 0, 0))],
        out_specs=pl.BlockSpec((heads, seq, LANES), lambda b, g: (g, b, 0)),
        out_shape=jax.ShapeDtypeStruct((n_heads, T, LANES), BF16),
        scratch_shapes=_attn_scratch(heads, seq, NA_WIN, ATTN_TQ),
        compiler_params=_params(2),
        name="na_attention",
    )(qkv, qkv, qkv, bias)


def _layer_norm(z, g, b):
    mu = jnp.mean(z, axis=-1, keepdims=True)
    zc = z - mu
    var = jnp.mean(zc * zc, axis=-1, keepdims=True)
    return zc * lax.rsqrt(var + LN_EPS) * g + b


def _top2_of4(v0, v1, v2, v3):
    a = jnp.maximum(v0, v1)
    b = jnp.minimum(v0, v1)
    c = jnp.maximum(v2, v3)
    d = jnp.minimum(v2, v3)
    return jnp.maximum(a, c), jnp.maximum(jnp.minimum(a, c), jnp.maximum(b, d))


def _route(logits_t, rb):
    G = N_GROUPS
    scores = jax.nn.sigmoid(logits_t)
    sel = scores + rb
    sj = [sel[j * G:(j + 1) * G] for j in range(EXPERTS_PER_GROUP)]
    cj = [scores[j * G:(j + 1) * G] for j in range(EXPERTS_PER_GROUP)]
    t1, t2 = _top2_of4(*sj)
    grp = t1 + t2
    gi = lax.broadcasted_iota(jnp.int32, grp.shape, 0).astype(F32)
    gbest = jnp.min(jnp.where(grp == jnp.max(grp, axis=0, keepdims=True), gi, float(G)), axis=0, keepdims=True)
    pick = gi == gbest
    v = [jnp.sum(jnp.where(pick, s, 0.0), axis=0, keepdims=True) for s in sj]
    c = [jnp.sum(jnp.where(pick, s, 0.0), axis=0, keepdims=True) for s in cj]
    neg = jnp.float32(-jnp.inf)
    m1 = jnp.maximum(jnp.maximum(v[0], v[1]), jnp.maximum(v[2], v[3]))
    j1 = jnp.where(v[0] == m1, 0, jnp.where(v[1] == m1, 1, jnp.where(v[2] == m1, 2, 3)))
    w = [jnp.where(j1 == j, neg, v[j]) for j in range(4)]
    m2 = jnp.maximum(jnp.maximum(w[0], w[1]), jnp.maximum(w[2], w[3]))
    j2 = jnp.where(w[0] == m2, 0, jnp.where(w[1] == m2, 1, jnp.where(w[2] == m2, 2, 3)))
    g1 = jnp.where(j1 == 0, c[0], jnp.where(j1 == 1, c[1], jnp.where(j1 == 2, c[2], c[3])))
    g2 = jnp.where(j2 == 0, c[0], jnp.where(j2 == 1, c[1], jnp.where(j2 == 2, c[2], c[3])))
    tot = g1 + g2
    e0 = gbest.astype(jnp.int32) * EXPERTS_PER_GROUP
    return (e0 + j1, e0 + j2), (g1 / tot, g2 / tot)


OUT_SUBTILES = 2


def _out_ln_kernel(a_ref, b_ref, w_ref, x_ref, g_ref, beta_ref, x1_ref, x1p_ref):
    sub = x_ref.shape[0] // OUT_SUBTILES
    half = x_ref.shape[1] // 2
    for t in range(OUT_SUBTILES):
        rows = slice(t * sub, (t + 1) * sub)
        heads = [a_ref[h, rows, :] for h in range(a_ref.shape[0])] + [b_ref[h, rows, :] for h in range(b_ref.shape[0])]
        mix = jnp.dot(jnp.concatenate(heads, axis=1), w_ref[...], preferred_element_type=F32)
        x1 = _layer_norm(ALPHA * x_ref[rows, :] + mix, g_ref[...], beta_ref[...])
        x1_ref[rows, :] = x1
        x1p_ref[rows, :] = _pack_bf16_pair(x1[:, :half], x1[:, half:])


def out_ln(a, b, w, x, g, beta, *, bm=512):
    T, D = x.shape
    return pl.pallas_call(
        _out_ln_kernel,
        grid=(T // bm,),
        in_specs=[pl.BlockSpec((a.shape[0], bm, LANES), lambda i: (0, i, 0)),
                  pl.BlockSpec((b.shape[0], bm, LANES), lambda i: (0, i, 0)),
                  _resident(w.shape),
                  pl.BlockSpec((bm, D), lambda i: (i, 0)),
                  _resident(g.shape), _resident(beta.shape)],
        out_specs=[pl.BlockSpec((bm, D), lambda i: (i, 0)), pl.BlockSpec((bm, D // 2), lambda i: (i, 0))],
        out_shape=[jax.ShapeDtypeStruct((T, D), F32), jax.ShapeDtypeStruct((T, D // 2), jnp.uint32)],
        compiler_params=_params(1),
        name="out_ln",
    )(a, b, w, x, g, beta)


def _route_kernel(x_ref, rwh_ref, rwl_ref, rb_ref, eid_ref, gate_ref):
    x = x_ref[...]
    hi = x.astype(BF16)
    lo = (x - hi.astype(F32)).astype(BF16)
    nt = (((1,), (1,)), ((), ()))
    logits_t = (lax.dot_general(rwh_ref[...], hi, nt, preferred_element_type=F32)
                + lax.dot_general(rwl_ref[...], hi, nt, preferred_element_type=F32)
                + lax.dot_general(rwh_ref[...], lo, nt, preferred_element_type=F32))
    eid, gate = _route(logits_t, rb_ref[...])
    for k in range(TOP_K):
        eid_ref[k:k + 1, :] = eid[k]
        gate_ref[k:k + 1, :] = gate[k]


def route(x1, rw_hi, rw_lo, rb, *, bm=1024):
    T, D = x1.shape
    return pl.pallas_call(
        _route_kernel,
        grid=(T // bm,),
        in_specs=[pl.BlockSpec((bm, D), lambda i: (i, 0)),
                  _resident(rw_hi.shape), _resident(rw_lo.shape), _resident(rb.shape)],
        out_specs=[pl.BlockSpec((TOP_K, bm), lambda i: (0, i)),
                   pl.BlockSpec((TOP_K, bm), lambda i: (0, i))],
        out_shape=[jax.ShapeDtypeStruct((TOP_K, T), jnp.int32),
                   jax.ShapeDtypeStruct((TOP_K, T), F32)],
        compiler_params=_params(1),
        name="route",
    )(x1, rw_hi, rw_lo, rb)


GATHER_DEPTH = 4


def _row_copy(src_hbm, row, dst_vmem, j, sem):
    return pltpu.make_async_copy(src_hbm.at[pl.ds(row, 1), :], dst_vmem.at[pl.ds(j, 1), :], sem)


def _gather_ring_step(start_gather, wait_gather, compute, active=None, idle=None):
    i = pl.program_id(0)
    n_steps = pl.num_programs(0)
    ahead = GATHER_DEPTH - 1
    slot = i % GATHER_DEPTH

    @pl.when(i == 0)
    def _():
        for a in range(ahead):
            start_gather(a, a)

    wait_gather(slot)
    more = i + ahead < n_steps
    cases = [(more, compute)] if active is None else [(more & active, compute), (more & ~active, idle)]
    for cond, work in cases:
        @pl.when(cond)
        def _(work=work):
            start_gather(i + ahead, (i + ahead) % GATHER_DEPTH)
            work(slot)
    cases = [(~more, compute)] if active is None else [(~more & active, compute), (~more & ~active, idle)]
    for cond, work in cases:
        @pl.when(cond)
        def _(work=work):
            work(slot)


def _pack_bf16_pair(lo, hi):
    lo_bits = pltpu.bitcast(lo.astype(BF16).astype(F32), jnp.uint32)
    hi_bits = pltpu.bitcast(hi.astype(BF16).astype(F32), jnp.uint32)
    return lax.shift_right_logical(lo_bits, jnp.uint32(16)) | (hi_bits & jnp.uint32(0xFFFF0000))


def _unpack_bf16_pair(word):
    lo = pltpu.bitcast(lax.shift_left(word, jnp.uint32(16)), F32)
    hi = pltpu.bitcast(word & jnp.uint32(0xFFFF0000), F32)
    return lo, hi


def _moe_ffn_kernel(blk_lo_ref, blk_hi_ref, nblk_ref, tok_ref, x_hbm,
                    wg_lo, wu_lo, wd_lo, wg_hi, wu_hi, wd_hi, y_ref, xbuf, sem):
    def start_gather(blk, s):
        base = blk * MOE_BLOCK
        for j in range(MOE_BLOCK):
            _row_copy(x_hbm, tok_ref[base + j], xbuf.at[s], j, sem.at[s]).start()

    def wait_gather(s):
        pltpu.make_async_copy(x_hbm.at[pl.ds(0, MOE_BLOCK), :], xbuf.at[s], sem.at[s]).wait()

    def ffn(s):
        x_left, x_right = _unpack_bf16_pair(xbuf[s])
        xb = jnp.concatenate([x_left.astype(BF16), x_right.astype(BF16)], axis=1)
        ys = []
        for wg, wu, wd in ((wg_lo, wu_lo, wd_lo), (wg_hi, wu_hi, wd_hi)):
            hg = jnp.dot(xb, wg[...], preferred_element_type=F32)
            hu = jnp.dot(xb, wu[...], preferred_element_type=F32)
            hb = (jax.nn.silu(hg) * hu).astype(BF16)
            ys.append(jnp.dot(hb, wd[...], preferred_element_type=F32))
        y_ref[...] = _pack_bf16_pair(ys[0], ys[1])

    def no_tokens(s):
        y_ref[...] = jnp.zeros(y_ref.shape, y_ref.dtype)

    _gather_ring_step(start_gather, wait_gather, ffn, active=pl.program_id(0) < nblk_ref[0], idle=no_tokens)


def moe_ffn(x1p, w_gate, w_up, w_down, blk_lo, blk_hi, nblk, row_tok):
    D = 2 * x1p.shape[1]
    n_blocks = blk_lo.shape[0]
    assert n_blocks >= GATHER_DEPTH
    n_rows = n_blocks * MOE_BLOCK
    w_in = lambda which: pl.BlockSpec((None, D, D_EXPERT), lambda i, lo, hi, nb, tk: ((lo, hi)[which][i], 0, 0))
    w_out = lambda which: pl.BlockSpec((None, D_EXPERT, D), lambda i, lo, hi, nb, tk: ((lo, hi)[which][i], 0, 0))
    grid_spec = pltpu.PrefetchScalarGridSpec(
        num_scalar_prefetch=4,
        grid=(n_blocks,),
        in_specs=[pl.BlockSpec(memory_space=pl.ANY), w_in(0), w_in(0), w_out(0), w_in(1), w_in(1), w_out(1)],
        out_specs=pl.BlockSpec((MOE_BLOCK, D), lambda i, lo, hi, nb, tk: (i, 0)),
        scratch_shapes=[pltpu.VMEM((GATHER_DEPTH, MOE_BLOCK, D // 2), jnp.uint32),
                        pltpu.SemaphoreType.DMA((GATHER_DEPTH,))],
    )
    return pl.pallas_call(
        _moe_ffn_kernel,
        grid_spec=grid_spec,
        out_shape=jax.ShapeDtypeStruct((n_rows, D), jnp.uint32),
        compiler_params=_params(1),
        name="moe_ffn",
    )(blk_lo, blk_hi, nblk, row_tok, x1p, w_gate, w_up, w_down, w_gate, w_up, w_down)


def _combine_ln_kernel(dest_ref, y_hbm, x1_ref, gate_ref, g_ref, beta_ref, x2_ref, x2b_ref, ybuf, sem, *, bm):
    def start_gather(tile, s):
        base = tile * bm
        for j in range(bm):
            _row_copy(y_hbm, dest_ref[base + j], ybuf.at[s], j, sem.at[s]).start()

    def wait_gather(s):
        pltpu.make_async_copy(y_hbm.at[pl.ds(0, bm), :], ybuf.at[s], sem.at[s]).wait()

    def finish(s):
        gate = gate_ref[...]
        y_lo, y_hi = _unpack_bf16_pair(ybuf[s])
        moe = y_lo * gate[:, 0:1] + y_hi * gate[:, 1:2]
        x2 = _layer_norm(ALPHA * x1_ref[...] + moe, g_ref[...], beta_ref[...])
        x2_ref[...] = x2
        x2b_ref[...] = x2.astype(BF16)

    _gather_ring_step(start_gather, wait_gather, finish)


def combine_ln(y_rows, dest, x1, gate_t, g, beta, *, bm=256):
    T, D = x1.shape
    assert T // bm >= GATHER_DEPTH
    grid_spec = pltpu.PrefetchScalarGridSpec(
        num_scalar_prefetch=1,
        grid=(T // bm,),
        in_specs=[pl.BlockSpec(memory_space=pl.ANY),
                  pl.BlockSpec((bm, D), lambda i, d: (i, 0)),
                  pl.BlockSpec((bm, TOP_K), lambda i, d: (i, 0)),
                  pl.BlockSpec((1, D), lambda i, d: (0, 0)),
                  pl.BlockSpec((1, D), lambda i, d: (0, 0))],
        out_specs=[pl.BlockSpec((bm, D), lambda i, d: (i, 0)),
                   pl.BlockSpec((bm, D), lambda i, d: (i, 0))],
        scratch_shapes=[pltpu.VMEM((GATHER_DEPTH, bm, D), jnp.uint32), pltpu.SemaphoreType.DMA((GATHER_DEPTH,))],
    )
    return pl.pallas_call(
        functools.partial(_combine_ln_kernel, bm=bm),
        grid_spec=grid_spec,
        out_shape=[jax.ShapeDtypeStruct((T, D), F32), jax.ShapeDtypeStruct((T, D), BF16)],
        compiler_params=_params(1),
        name="combine_ln",
    )(dest, y_rows, x1, gate_t, g, beta)


_PAIR_LO = np.array([a for a in range(EXPERTS_PER_GROUP) for b in range(a + 1, EXPERTS_PER_GROUP)], np.int32)
_PAIR_HI = np.array([b for a in range(EXPERTS_PER_GROUP) for b in range(a + 1, EXPERTS_PER_GROUP)], np.int32)
N_PAIRS = len(_PAIR_LO)
N_CLASSES = N_GROUPS * N_PAIRS


def moe_plan(eid, gate):
    T = eid.shape[1]
    n_blocks = (T + N_CLASSES * (MOE_BLOCK - 1) + MOE_BLOCK - 1) // MOE_BLOCK
    first_lower = eid[0] < eid[1]
    e_lo = jnp.minimum(eid[0], eid[1])
    e_hi = jnp.maximum(eid[0], eid[1])
    gate_t = jnp.stack([jnp.where(first_lower, gate[0], gate[1]), jnp.where(first_lower, gate[1], gate[0])], axis=1)
    a = e_lo % EXPERTS_PER_GROUP
    b = e_hi % EXPERTS_PER_GROUP
    pair = a * (2 * EXPERTS_PER_GROUP - 1 - a) // 2 + (b - a - 1)
    cls = (e_lo // EXPERTS_PER_GROUP) * N_PAIRS + pair
    onehot = (cls[:, None] == jnp.arange(N_CLASSES, dtype=jnp.int32)[None, :]).astype(jnp.int32)
    csum = jnp.cumsum(onehot, axis=0)
    rank = jnp.sum(csum * onehot, axis=1) - 1
    counts = csum[-1]
    pcounts = (counts + MOE_BLOCK - 1) // MOE_BLOCK * MOE_BLOCK
    pends = jnp.cumsum(pcounts)
    pstarts = pends - pcounts
    dest = (pstarts[cls] + rank).astype(jnp.int32)
    pad_tok = jnp.arange(n_blocks * MOE_BLOCK, dtype=jnp.int32) % T
    row_tok = pad_tok.at[dest].set(jnp.arange(T, dtype=jnp.int32))
    blk_start = jnp.arange(n_blocks, dtype=jnp.int32) * MOE_BLOCK
    blk_cls = jnp.minimum(jnp.sum((pends[None, :] <= blk_start[:, None]).astype(jnp.int32), axis=1), N_CLASSES - 1)
    blk_base = (blk_cls // N_PAIRS) * EXPERTS_PER_GROUP
    blk_lo = (blk_base + jnp.asarray(_PAIR_LO)[blk_cls % N_PAIRS]).astype(jnp.int32)
    blk_hi = (blk_base + jnp.asarray(_PAIR_HI)[blk_cls % N_PAIRS]).astype(jnp.int32)
    nblk = (pends[-1:] // MOE_BLOCK).astype(jnp.int32)
    return dest, row_tok, blk_lo, blk_hi, nblk, gate_t


def moe_layer(x1, x1p, eid, gate, w_gate, w_up, w_down, ln_g, ln_b):
    dest, row_tok, blk_lo, blk_hi, nblk, gate_t = moe_plan(eid, gate)
    y_rows = moe_ffn(x1p, w_gate.astype(BF16), w_up.astype(BF16), w_down.astype(BF16), blk_lo, blk_hi, nblk, row_tok)
    return combine_ln(y_rows, dest, x1, gate_t, ln_g.reshape(1, -1), ln_b.reshape(1, -1))


def _rope_cos_sin(pos, dim):
    inv_freq = ROPE_THETA ** (-jnp.arange(0, dim, 2, dtype=F32) / dim)
    ang = pos.astype(F32)[:, None] * inv_freq[None, :]
    cos = jnp.cos(ang)
    sin = jnp.sin(ang)
    return jnp.concatenate([cos, cos], axis=-1), jnp.concatenate([-sin, sin], axis=-1)


def axial_rope_tables(seq):
    pos = jnp.arange(seq)
    half = HEAD_DIM // 2
    cr, sr = _rope_cos_sin(pos // GRID_W, half)
    cc, sc = _rope_cos_sin(pos % GRID_W, half)
    return jnp.concatenate([cr, cc], axis=-1), jnp.concatenate([sr, sc], axis=-1)


def mla_rope_tables(seq):
    c, s = _rope_cos_sin(jnp.arange(seq), D_ROPE)
    z = jnp.zeros((seq, LANES - D_ROPE), F32)
    return jnp.concatenate([c, z], axis=-1), jnp.concatenate([s, z], axis=-1)


def router_layout(router_w, router_b):
    perm = np.array([g * EXPERTS_PER_GROUP + j for j in range(EXPERTS_PER_GROUP) for g in range(N_GROUPS)])
    rw = router_w.astype(F32).T[perm]
    hi = rw.astype(BF16)
    lo = (rw - hi.astype(F32)).astype(BF16)
    return hi, lo, router_b.astype(F32)[perm].reshape(N_EXPERTS, 1)


def mla_up_layout(w_q_up, w_kv_up, w_in_tail):
    rq = w_q_up.shape[0]
    wq = w_q_up.reshape(rq, D_HEADS, D_NOPE + D_ROPE)
    wq = jnp.pad(wq, ((0, 0), (0, 0), (0, 2 * LANES - (D_NOPE + D_ROPE)))).reshape(rq, D_HEADS * 2 * LANES)
    wc = jnp.pad(w_in_tail, ((0, 0), (0, LANES - D_ROPE)))
    return wq.astype(BF16), w_kv_up.astype(BF16), wc.astype(BF16)


def kernel(x, router_w, router_b, l0_w_in, l0_a_q_norm, l0_a_k_norm, l0_w_out, l0_ln1_g, l0_ln1_b, l0_w_gate, l0_w_up, l0_w_down, l0_ln2_g, l0_ln2_b, l1_w_in, l1_c_rpb, l1_d_q_norm, l1_d_w_q_up, l1_d_kv_norm, l1_d_w_kv_up, l1_w_out, l1_ln1_g, l1_ln1_b, l1_w_gate, l1_w_up, l1_w_down, l1_ln2_g, l1_ln2_b):
    batch, seq, d = x.shape
    T = batch * seq
    xf = x.reshape(T, d)
    rw_hi, rw_lo, rb = router_layout(router_w, router_b)
    row = lambda v: v.astype(F32).reshape(1, -1)

    a_q_dim = A_HEADS * HEAD_DIM
    a_qk_dim = a_q_dim + A_KV_HEADS * HEAD_DIM
    w0 = l0_w_in.astype(BF16)
    qk_scale = LOG2E * HEAD_DIM ** -0.5
    gain = jnp.concatenate([jnp.tile(l0_a_q_norm.astype(F32), A_HEADS) * qk_scale,
                            jnp.tile(l0_a_k_norm.astype(F32), A_KV_HEADS)]).reshape(1, -1)
    cos_a, sin_a = axial_rope_tables(seq)
    qk_a = proj_norm_rope(xf, w0[:, :a_qk_dim], gain, cos_a, sin_a, seq=seq)
    nb = A_KV_HEADS
    q_scale = jnp.full((B_HEADS * HEAD_DIM,), qk_scale, F32)
    scale0 = jnp.concatenate([jnp.ones((nb * HEAD_DIM,), F32), q_scale, jnp.ones((2 * B_HEADS * HEAD_DIM,), F32)])
    rest0 = proj_heads(xf, w0[:, a_qk_dim:], scale0.reshape(1, -1), heads_per_tile=13)
    group = A_HEADS // A_KV_HEADS
    oa = dense_attention(qk_a, qk_a, rest0, n_heads=A_HEADS, heads=group, k0=A_HEADS, v0=0, batch=batch, seq=seq,
                         group=group)
    ob = dilated_attention(rest0, dilated_bias_table(B_HEADS), n_heads=B_HEADS, heads=1, q0=nb, k0=nb + B_HEADS,
                           v0=nb + 2 * B_HEADS, batch=batch, seq=seq)
    x1, x1p = out_ln(oa, ob, l0_w_out.astype(BF16), xf, row(l0_ln1_g), row(l0_ln1_b))
    eid, gate = route(x1, rw_hi, rw_lo, rb)
    x2, x2b = moe_layer(x1, x1p, eid, gate, l0_w_gate, l0_w_up, l0_w_down, l0_ln2_g, l0_ln2_b)

    c_dim = C_HEADS * HEAD_DIM
    w1 = l1_w_in
    scale1 = jnp.concatenate([jnp.full((c_dim,), qk_scale, F32), jnp.ones((2 * c_dim,), F32)])
    qkv_c = proj_heads(x2b, w1[:, :3 * c_dim].astype(BF16), scale1.reshape(1, -1), heads_per_tile=12)
    oc = na_attention(qkv_c, na_bias_table(l1_c_rpb, seq // GRID_W), n_heads=C_HEADS, heads=2, q0=0, k0=C_HEADS,
                      v0=2 * C_HEADS, batch=batch, seq=seq)
    wq, wkv, wc = mla_up_layout(l1_d_w_q_up, l1_d_w_kv_up, w1[:, 3 * c_dim:])
    cos_d, sin_d = mla_rope_tables(seq)
    qd, kd, vd = mla_proj(x2b, wc, row(l1_d_q_norm), row(l1_d_kv_norm), wq, wkv, cos_d, sin_d, seq=seq,
                          scale=LOG2E * (D_NOPE + D_ROPE) ** -0.5)
    od = dense_attention(qd, kd, vd, n_heads=D_HEADS, heads=1, k0=0, v0=0, batch=batch, seq=seq, group=1)
    x3, x3p = out_ln(oc, od, l1_w_out.astype(BF16), x2, row(l1_ln1_g), row(l1_ln1_b))
    eid, gate = route(x3, rw_hi, rw_lo, rb)
    x4, _ = moe_layer(x3, x3p, eid, gate, l1_w_gate, l1_w_up, l1_w_down, l1_ln2_g, l1_ln2_b)
    return x4.reshape(batch, seq, d)
```

```python
import functools
import math

import jax
import jax.numpy as jnp
import numpy as np
from jax import lax
from jax.experimental import pallas as pl
from jax.experimental.pallas import tpu as pltpu

D_MODEL = 2048
DEPTH = 2
GRID_W = 64
HEAD_DIM = 128
ROPE_THETA = 10000.0
RMS_EPS = 1e-6
LN_EPS = 1e-5

A_HEADS = 8
A_KV_HEADS = 2
B_HEADS = 8
B_BRANCHES = ((128, 1), (512, 4), (2048, 16))
C_HEADS = 8
NA_ROWS = 8
NA_COLS = 16
D_HEADS = 8
D_Q_RANK = 512
D_KV_RANK = 256
D_NOPE = 128
D_ROPE = 64
D_V = 128

N_EXPERTS = 32
N_GROUPS = 8
EXPERTS_PER_GROUP = N_EXPERTS // N_GROUPS
TOP_K = 2
D_EXPERT = 512
MOE_BLOCK = 256

ALPHA = (2.0 * DEPTH) ** 0.25
LOG2E = math.log2(math.e)

LANES = 128
VMEM_LIMIT = 56 * 1024 * 1024

BF16 = jnp.bfloat16
F32 = jnp.float32


def _params(n_axes, **flags):
    return pltpu.CompilerParams(dimension_semantics=("arbitrary",) * n_axes, vmem_limit_bytes=VMEM_LIMIT,
                                flags=flags or None)


def _rot_half32(y):
    lane = lax.broadcasted_iota(jnp.int32, y.shape, y.ndim - 1)
    up = pltpu.roll(y, 96, axis=y.ndim - 1)
    dn = pltpu.roll(y, 32, axis=y.ndim - 1)
    return jnp.where((lane % 64) < 32, up, dn)


def _proj_heads_kernel(x_ref, w_ref, c_ref, o_ref, *, heads_per_tile):
    acc = jnp.dot(x_ref[...].astype(BF16), w_ref[...], preferred_element_type=F32) * c_ref[...]
    for h in range(heads_per_tile):
        o_ref[h] = acc[:, h * LANES:(h + 1) * LANES].astype(o_ref.dtype)


def proj_heads(x, w, col_scale, *, heads_per_tile, bm=512):
    T, D = x.shape
    H = w.shape[1] // LANES
    bn = heads_per_tile * LANES
    grid = (H // heads_per_tile, T // bm)
    return pl.pallas_call(
        functools.partial(_proj_heads_kernel, heads_per_tile=heads_per_tile),
        grid=grid,
        in_specs=[pl.BlockSpec((bm, D), lambda j, i: (i, 0)),
                  pl.BlockSpec((D, bn), lambda j, i: (0, j)),
                  pl.BlockSpec((1, bn), lambda j, i: (0, j))],
        out_specs=pl.BlockSpec((heads_per_tile, bm, LANES), lambda j, i: (j, i, 0)),
        out_shape=jax.ShapeDtypeStruct((H, T, LANES), BF16),
        compiler_params=_params(2),
        name="proj_heads",
    )(x, w, col_scale)


def _resident(shape):
    return pl.BlockSpec(shape, lambda *_: (0,) * len(shape), pipeline_mode=pl.Buffered(1))


PROJ_SUBTILES = 2


def _proj_norm_rope_kernel(x_ref, w_ref, g_ref, cos_ref, sin_ref, o_ref):
    sub = x_ref.shape[0] // PROJ_SUBTILES
    for t in range(PROJ_SUBTILES):
        rows = slice(t * sub, (t + 1) * sub)
        acc = jnp.dot(x_ref[rows, :].astype(BF16), w_ref[...], preferred_element_type=F32)
        cos = cos_ref[rows, :]
        sin = sin_ref[rows, :]
        for h in range(o_ref.shape[0]):
            a = acc[:, h * LANES:(h + 1) * LANES]
            ms = jnp.mean(a * a, axis=-1, keepdims=True)
            y = a * lax.rsqrt(ms + RMS_EPS) * g_ref[:, h * LANES:(h + 1) * LANES]
            o_ref[h, rows, :] = (y * cos + _rot_half32(y) * sin).astype(o_ref.dtype)


def proj_norm_rope(x, w, gain, cos, sin, *, seq, bm=512):
    T, D = x.shape
    H = w.shape[1] // LANES
    nsb = seq // bm
    return pl.pallas_call(
        _proj_norm_rope_kernel,
        grid=(T // bm,),
        in_specs=[pl.BlockSpec((bm, D), lambda i: (i, 0)),
                  _resident(w.shape), _resident(gain.shape),
                  pl.BlockSpec((bm, LANES), lambda i: (i % nsb, 0)),
                  pl.BlockSpec((bm, LANES), lambda i: (i % nsb, 0))],
        out_specs=pl.BlockSpec((H, bm, LANES), lambda i: (0, i, 0)),
        out_shape=jax.ShapeDtypeStruct((H, T, LANES), BF16),
        compiler_params=_params(1),
        name="proj_norm_rope",
    )(x, w, gain, cos, sin)


def _mla_proj_kernel(x_ref, wc_ref, gq_ref, gkv_ref, wq_ref, wkv_ref, cos_ref, sin_ref,
                     q_ref, k_ref, v_ref, *, scale):
    c = jnp.dot(x_ref[...], wc_ref[...], preferred_element_type=F32)
    cos = cos_ref[...]
    sin = sin_ref[...]

    def rms(t, g):
        ms = jnp.mean(t * t, axis=-1, keepdims=True)
        return t * lax.rsqrt(ms + RMS_EPS) * g

    cq = rms(c[:, :D_Q_RANK], gq_ref[...]).astype(BF16)
    ckv = rms(c[:, D_Q_RANK:D_Q_RANK + D_KV_RANK], gkv_ref[...]).astype(BF16)
    kr = c[:, D_Q_RANK + D_KV_RANK:]
    kr = (kr * cos + _rot_half32(kr) * sin).astype(k_ref.dtype)
    q = jnp.dot(cq, wq_ref[...], preferred_element_type=F32) * scale
    kv = jnp.dot(ckv, wkv_ref[...], preferred_element_type=F32)
    for h in range(D_HEADS):
        base = h * 2 * LANES
        q_ref[h, :, :LANES] = q[:, base:base + LANES].astype(q_ref.dtype)
        qr = q[:, base + LANES:base + 2 * LANES]
        q_ref[h, :, LANES:] = (qr * cos + _rot_half32(qr) * sin).astype(q_ref.dtype)
        k_ref[h, :, :LANES] = kv[:, base:base + LANES].astype(k_ref.dtype)
        k_ref[h, :, LANES:] = kr
        v_ref[h] = kv[:, base + LANES:base + 2 * LANES].astype(v_ref.dtype)


def mla_proj(x, wc, gq, gkv, wq, wkv, cos, sin, *, seq, scale, bm=512):
    T, D = x.shape
    nsb = seq // bm
    return pl.pallas_call(
        functools.partial(_mla_proj_kernel, scale=scale),
        grid=(T // bm,),
        in_specs=[pl.BlockSpec((bm, D), lambda i: (i, 0)),
                  _resident(wc.shape), _resident(gq.shape), _resident(gkv.shape), _resident(wq.shape),
                  _resident(wkv.shape),
                  pl.BlockSpec((bm, LANES), lambda i: (i % nsb, 0)),
                  pl.BlockSpec((bm, LANES), lambda i: (i % nsb, 0))],
        out_specs=[pl.BlockSpec((D_HEADS, bm, 2 * LANES), lambda i: (0, i, 0)),
                   pl.BlockSpec((D_HEADS, bm, 2 * LANES), lambda i: (0, i, 0)),
                   pl.BlockSpec((D_HEADS, bm, LANES), lambda i: (0, i, 0))],
        out_shape=[jax.ShapeDtypeStruct((D_HEADS, T, 2 * LANES), BF16),
                   jax.ShapeDtypeStruct((D_HEADS, T, 2 * LANES), BF16),
                   jax.ShapeDtypeStruct((D_HEADS, T, LANES), BF16)],
        compiler_params=_params(1),
        name="mla_proj",
    )(x, wc, gq, gkv, wq, wkv, cos, sin)


DENSE_TQ = 512
ATTN_TQ = 256


def _attn_pipeline(q_ref, k_ref, v_ref, o_ref, vext_ref, s_buf, m_buf, p_buf, *, heads, group, win, win_start, bias):
    tq = s_buf[0].shape[0]
    S = q_ref.shape[1]
    nblk = S // tq
    total = heads * nblk
    vext_ref[:, :, :LANES] = v_ref[...]
    vext_ref[:, :, LANES:] = jnp.ones(v_ref.shape, vext_ref.dtype)

    def locate(n):
        h, blk = (0, n) if heads == 1 else (n // nblk, n % nblk)
        rows = pl.ds(blk * tq, tq) if isinstance(blk, int) else pl.ds(pl.multiple_of(blk * tq, tq), tq)
        keys = slice(None) if win == S else pl.ds(win_start(blk), win)
        return h, blk, rows, keys

    def scores(n, par):
        h, blk, rows, keys = locate(n)
        s = lax.dot_general(q_ref[h, rows, :], k_ref[h // group, keys, :], (((1,), (1,)), ((), ())),
                            preferred_element_type=F32)
        if bias is not None:
            s = s + bias(h, blk)
        s_buf[par][...] = s
        m_buf[par][...] = jnp.max(s, axis=-1, keepdims=True)

    def exps(par):
        p_buf[par][...] = jnp.exp2(s_buf[par][...] - m_buf[par][...]).astype(BF16)

    def wsum(n, par):
        h, blk, rows, keys = locate(n)
        o_ext = jnp.dot(p_buf[par][...], vext_ref[h // group, keys, :], preferred_element_type=F32)
        o_ref[h, rows, :] = (o_ext[:, :LANES] / o_ext[:, LANES:]).astype(o_ref.dtype)

    scores(0, 0)
    exps(0)
    scores(1, 1)

    def step(j, carry):
        n = 2 * j + 1
        exps(1)
        wsum(n - 1, 0)
        scores(n + 1, 0)
        exps(0)
        wsum(n, 1)
        scores(n + 2, 1)
        return carry

    lax.fori_loop(0, total // 2 - 1, step, 0)
    exps(1)
    wsum(total - 2, 0)
    wsum(total - 1, 1)


def _attn_scratch(kv_heads, seq, win, tq):
    return [pltpu.VMEM((kv_heads, seq, 2 * LANES), BF16),
            pltpu.VMEM((tq, win), F32), pltpu.VMEM((tq, win), F32),
            pltpu.VMEM((tq, 1), F32), pltpu.VMEM((tq, 1), F32),
            pltpu.VMEM((tq, win), BF16), pltpu.VMEM((tq, win), BF16)]


def _dense_attn_kernel(q_ref, k_ref, v_ref, o_ref, vext_ref, s0, s1, m0, m1, p0, p1, *, heads, group):
    _attn_pipeline(q_ref, k_ref, v_ref, o_ref, vext_ref, (s0, s1), (m0, m1), (p0, p1), heads=heads, group=group,
                   win=q_ref.shape[1], win_start=None, bias=None)


def dense_attention(q, k, v, *, n_heads, heads, k0, v0, batch, seq, group):
    T, dq = q.shape[1:]
    kvh = heads // group
    assert n_heads % heads == 0 and heads % group == 0 and k0 % kvh == 0 and v0 % kvh == 0
    return pl.pallas_call(
        functools.partial(_dense_attn_kernel, heads=heads, group=group),
        grid=(batch, n_heads // heads),
        in_specs=[pl.BlockSpec((heads, seq, dq), lambda b, g: (g, b, 0)),
                  pl.BlockSpec((kvh, seq, dq), lambda b, g: (k0 // kvh + g, b, 0)),
                  pl.BlockSpec((kvh, seq, LANES), lambda b, g: (v0 // kvh + g, b, 0))],
        out_specs=pl.BlockSpec((heads, seq, LANES), lambda b, g: (g, b, 0)),
        out_shape=jax.ShapeDtypeStruct((n_heads, T, LANES), BF16),
        scratch_shapes=_attn_scratch(kvh, seq, seq, DENSE_TQ),
        compiler_params=_params(2),
        name="dense_attention",
    )(q, k, v)


DIL_REACH = max(w // 2 for w, _ in B_BRANCHES)
DIL_WIN = 2 * DIL_REACH + ATTN_TQ
DIL_TABLE_TILES = (DIL_WIN + 2 * DIL_REACH) // LANES


def _dilated_attn_kernel(q_ref, k_ref, v_ref, bias_ref, o_ref, vext_ref, s0, s1, m0, m1, p0, p1, *, heads):
    S = q_ref.shape[1]

    def win_start(blk):
        return pl.multiple_of(jnp.clip(blk * ATTN_TQ - DIL_REACH, 0, S - DIL_WIN), ATTN_TQ)

    def bias(h, blk):
        tile0 = (win_start(blk) - blk * ATTN_TQ + 2 * DIL_REACH) // LANES
        return jnp.concatenate([bias_ref[h, tile0 + t] for t in range(DIL_WIN // LANES)], axis=1)

    _attn_pipeline(q_ref, k_ref, v_ref, o_ref, vext_ref, (s0, s1), (m0, m1), (p0, p1), heads=heads, group=1,
                   win=DIL_WIN, win_start=win_start, bias=bias)


def dilated_bias_table(n_heads):
    col = (np.arange(DIL_TABLE_TILES)[:, None, None] * LANES + np.arange(LANES)[None, None, :])
    d = np.abs(col - np.arange(ATTN_TQ)[None, :, None] - 2 * DIL_REACH)
    mult = np.zeros(d.shape, np.float64)
    for window, dil in B_BRANCHES:
        mult += ((d % dil == 0) & (d <= window // 2)).astype(np.float64)
    with np.errstate(divide="ignore"):
        logm = np.log(mult)
    slopes = jnp.asarray(2.0 ** (-8.0 * np.arange(1, n_heads + 1) / n_heads), dtype=F32)
    bias = jnp.asarray(logm, dtype=F32)[None] - slopes[:, None, None, None] * jnp.asarray(d, dtype=F32)[None]
    return bias * LOG2E


def dilated_attention(qkv, bias, *, n_heads, heads, q0, k0, v0, batch, seq):
    T = qkv.shape[1]
    assert n_heads % heads == 0 and q0 % heads == 0 and k0 % heads == 0 and v0 % heads == 0
    qkv_spec = lambda h0: pl.BlockSpec((heads, seq, LANES), lambda b, g: (h0 // heads + g, b, 0))
    return pl.pallas_call(
        functools.partial(_dilated_attn_kernel, heads=heads),
        grid=(batch, n_heads // heads),
        in_specs=[qkv_spec(q0), qkv_spec(k0), qkv_spec(v0),
                  pl.BlockSpec((heads, DIL_TABLE_TILES, ATTN_TQ, LANES), lambda b, g: (g, 0, 0, 0))],
        out_specs=pl.BlockSpec((heads, seq, LANES), lambda b, g: (g, b, 0)),
        out_shape=jax.ShapeDtypeStruct((n_heads, T, LANES), BF16),
        scratch_shapes=_attn_scratch(heads, seq, DIL_WIN, ATTN_TQ),
        compiler_params=_params(2),
        name="dilated_attention",
    )(qkv, qkv, qkv, bias)


NA_QROWS = ATTN_TQ // GRID_W
NA_KROWS = NA_QROWS + NA_ROWS
NA_WIN = NA_KROWS * GRID_W


def _na_window_row(blk, rows):
    lo = blk * NA_QROWS - NA_ROWS // 2
    return jnp.clip(lo, 0, rows - NA_KROWS) if not isinstance(blk, int) else min(max(lo, 0), rows - NA_KROWS)


def _na_attn_kernel(q_ref, k_ref, v_ref, bias_ref, o_ref, vext_ref, s0, s1, m0, m1, p0, p1, *, heads, rows):
    nblk = rows // NA_QROWS

    def win_start(blk):
        return pl.multiple_of(_na_window_row(blk, rows) * GRID_W, GRID_W)

    def bias(h, blk):
        cls = (blk > 0).astype(jnp.int32) + (blk == nblk - 1).astype(jnp.int32) if not isinstance(blk, int) \
            else int(blk > 0) + int(blk == nblk - 1)
        return bias_ref[h, cls]

    _attn_pipeline(q_ref, k_ref, v_ref, o_ref, vext_ref, (s0, s1), (m0, m1), (p0, p1), heads=heads, group=1,
                   win=NA_WIN, win_start=win_start, bias=bias)


def na_bias_table(rpb, rows):
    H = rpb.shape[0]
    col = np.arange(GRID_W)
    c0 = np.clip(col - NA_COLS // 2, 0, GRID_W - NA_COLS)
    col_ok = (col[None, :] >= c0[:, None]) & (col[None, :] < c0[:, None] + NA_COLS)
    dcol = np.clip(col[None, :] - col[:, None] + NA_COLS - 1, 0, 2 * NA_COLS - 2)
    pick = ((dcol[:, :, None] == np.arange(2 * NA_COLS - 1)) & col_ok[:, :, None]).astype(np.float32)
    t = jnp.einsum("hrd,qkd->hrqk", rpb.astype(F32), jnp.asarray(pick), precision=lax.Precision.HIGHEST)
    t = jnp.where(jnp.asarray(col_ok)[None, None], t * LOG2E, -jnp.inf)
    nblk = rows // NA_QROWS

    def geometry(blk):
        r = blk * NA_QROWS + np.arange(NA_QROWS)
        kr = _na_window_row(blk, rows) + np.arange(NA_KROWS)
        r0 = np.clip(r - NA_ROWS // 2, 0, rows - NA_ROWS)
        valid = (kr[None, :] >= r0[:, None]) & (kr[None, :] < r0[:, None] + NA_ROWS)
        drow = np.clip(kr[None, :] - r[:, None] + NA_ROWS - 1, 0, 2 * NA_ROWS - 2)
        return valid, np.where(valid, drow, 0)

    inner = geometry(1)
    assert all(np.array_equal(a, b) for blk in range(1, nblk - 1) for a, b in zip(geometry(blk), inner))
    tables = []
    for blk in (0, 1, nblk - 1):
        valid, drow = geometry(blk)
        slab = jnp.where(jnp.asarray(valid)[None, :, :, None, None], t[:, drow], -jnp.inf)
        tables.append(slab.transpose(0, 1, 3, 2, 4).reshape(H, ATTN_TQ, NA_WIN))
    return jnp.stack(tables, axis=1)


def na_attention(qkv, bias, *, n_heads, heads, q0, k0, v0, batch, seq):
    T = qkv.shape[1]
    rows = seq // GRID_W
    assert n_heads % heads == 0 and q0 % heads == 0 and k0 % heads == 0 and v0 % heads == 0
    assert rows % NA_QROWS == 0 and rows // NA_QROWS >= 3
    qkv_spec = lambda h0: pl.BlockSpec((heads, seq, LANES), lambda b, g: (h0 // heads + g, b, 0))
    return pl.pallas_call(
        functools.partial(_na_attn_kernel, heads=heads, rows=rows),
        grid=(batch, n_heads // heads),
        in_specs=[qkv_spec(q0), qkv_spec(k0), qkv_spec(v0),
                  pl.BlockSpec((heads, 3, ATTN_TQ, NA_WIN), lambda b, g: (g, 0, 0, 0))],
        out_specs=pl.BlockSpec((heads, seq, LANES), lambda b, g: (g, b, 0)),
        out_shape=jax.ShapeDtypeStruct((n_heads, T, LANES), BF16),
        scratch_shapes=_attn_scratch(heads, seq, NA_WIN, ATTN_TQ),
        compiler_params=_params(2),
        name="na_attention",
    )(qkv, qkv, qkv, bias)


def _layer_norm(z, g, b):
    mu = jnp.mean(z, axis=-1, keepdims=True)
    zc = z - mu
    var = jnp.mean(zc * zc, axis=-1, keepdims=True)
    return zc * lax.rsqrt(var + LN_EPS) * g + b


def _top2_of4(v0, v1, v2, v3):
    a = jnp.maximum(v0, v1)
    b = jnp.minimum(v0, v1)
    c = jnp.maximum(v2, v3)
    d = jnp.minimum(v2, v3)
    return jnp.maximum(a, c), jnp.maximum(jnp.minimum(a, c), jnp.maximum(b, d))


def _route(logits_t, rb):
    G = N_GROUPS
    scores = jax.nn.sigmoid(logits_t)
    sel = scores + rb
    sj = [sel[j * G:(j + 1) * G] for j in range(EXPERTS_PER_GROUP)]
    cj = [scores[j * G:(j + 1) * G] for j in range(EXPERTS_PER_GROUP)]
    t1, t2 = _top2_of4(*sj)
    grp = t1 + t2
    gi = lax.broadcasted_iota(jnp.int32, grp.shape, 0).astype(F32)
    gbest = jnp.min(jnp.where(grp == jnp.max(grp, axis=0, keepdims=True), gi, float(G)), axis=0, keepdims=True)
    pick = gi == gbest
    v = [jnp.sum(jnp.where(pick, s, 0.0), axis=0, keepdims=True) for s in sj]
    c = [jnp.sum(jnp.where(pick, s, 0.0), axis=0, keepdims=True) for s in cj]
    neg = jnp.float32(-jnp.inf)
    m1 = jnp.maximum(jnp.maximum(v[0], v[1]), jnp.maximum(v[2], v[3]))
    j1 = jnp.where(v[0] == m1, 0, jnp.where(v[1] == m1, 1, jnp.where(v[2] == m1, 2, 3)))
    w = [jnp.where(j1 == j, neg, v[j]) for j in range(4)]
    m2 = jnp.maximum(jnp.maximum(w[0], w[1]), jnp.maximum(w[2], w[3]))
    j2 = jnp.where(w[0] == m2, 0, jnp.where(w[1] == m2, 1, jnp.where(w[2] == m2, 2, 3)))
    g1 = jnp.where(j1 == 0, c[0], jnp.where(j1 == 1, c[1], jnp.where(j1 == 2, c[2], c[3])))
    g2 = jnp.where(j2 == 0, c[0], jnp.where(j2 == 1, c[1], jnp.where(j2 == 2, c[2], c[3])))
    tot = g1 + g2
    e0 = gbest.astype(jnp.int32) * EXPERTS_PER_GROUP
    return (e0 + j1, e0 + j2), (g1 / tot, g2 / tot)


OUT_SUBTILES = 2


def _out_ln_kernel(a_ref, b_ref, w_ref, x_ref, g_ref, beta_ref, x1_ref, x1p_ref):
    sub = x_ref.shape[0] // OUT_SUBTILES
    half = x_ref.shape[1] // 2
    for t in range(OUT_SUBTILES):
        rows = slice(t * sub, (t + 1) * sub)
        heads = [a_ref[h, rows, :] for h in range(a_ref.shape[0])] + [b_ref[h, rows, :] for h in range(b_ref.shape[0])]
        mix = jnp.dot(jnp.concatenate(heads, axis=1), w_ref[...], preferred_element_type=F32)
        x1 = _layer_norm(ALPHA * x_ref[rows, :] + mix, g_ref[...], beta_ref[...])
        x1_ref[rows, :] = x1
        x1p_ref[rows, :] = _pack_bf16_pair(x1[:, :half], x1[:, half:])


def out_ln(a, b, w, x, g, beta, *, bm=512):
    T, D = x.shape
    return pl.pallas_call(
        _out_ln_kernel,
        grid=(T // bm,),
        in_specs=[pl.BlockSpec((a.shape[0], bm, LANES), lambda i: (0, i, 0)),
                  pl.BlockSpec((b.shape[0], bm, LANES), lambda i: (0, i, 0)),
                  _resident(w.shape),
                  pl.BlockSpec((bm, D), lambda i: (i, 0)),
                  _resident(g.shape), _resident(beta.shape)],
        out_specs=[pl.BlockSpec((bm, D), lambda i: (i, 0)), pl.BlockSpec((bm, D // 2), lambda i: (i, 0))],
        out_shape=[jax.ShapeDtypeStruct((T, D), F32), jax.ShapeDtypeStruct((T, D // 2), jnp.uint32)],
        compiler_params=_params(1),
        name="out_ln",
    )(a, b, w, x, g, beta)


def _route_kernel(x_ref, rwh_ref, rwl_ref, rb_ref, eid_ref, gate_ref):
    x = x_ref[...]
    hi = x.astype(BF16)
    lo = (x - hi.astype(F32)).astype(BF16)
    nt = (((1,), (1,)), ((), ()))
    logits_t = (lax.dot_general(rwh_ref[...], hi, nt, preferred_element_type=F32)
                + lax.dot_general(rwl_ref[...], hi, nt, preferred_element_type=F32)
                + lax.dot_general(rwh_ref[...], lo, nt, preferred_element_type=F32))
    eid, gate = _route(logits_t, rb_ref[...])
    for k in range(TOP_K):
        eid_ref[k:k + 1, :] = eid[k]
        gate_ref[k:k + 1, :] = gate[k]


def route(x1, rw_hi, rw_lo, rb, *, bm=1024):
    T, D = x1.shape
    return pl.pallas_call(
        _route_kernel,
        grid=(T // bm,),
        in_specs=[pl.BlockSpec((bm, D), lambda i: (i, 0)),
                  _resident(rw_hi.shape), _resident(rw_lo.shape), _resident(rb.shape)],
        out_specs=[pl.BlockSpec((TOP_K, bm), lambda i: (0, i)),
                   pl.BlockSpec((TOP_K, bm), lambda i: (0, i))],
        out_shape=[jax.ShapeDtypeStruct((TOP_K, T), jnp.int32),
                   jax.ShapeDtypeStruct((TOP_K, T), F32)],
        compiler_params=_params(1),
        name="route",
    )(x1, rw_hi, rw_lo, rb)


GATHER_DEPTH = 3


def _row_copy(src_hbm, row, dst_vmem, j, sem):
    return pltpu.make_async_copy(src_hbm.at[pl.ds(row, 1), :], dst_vmem.at[pl.ds(j, 1), :], sem)


def _gather_ring_step(start_gather, wait_gather, compute, active=None, idle=None):
    i = pl.program_id(0)
    n_steps = pl.num_programs(0)
    ahead = GATHER_DEPTH - 1
    slot = i % GATHER_DEPTH

    @pl.when(i == 0)
    def _():
        for a in range(ahead):
            start_gather(a, a)

    wait_gather(slot)
    more = i + ahead < n_steps
    cases = [(more, compute)] if active is None else [(more & active, compute), (more & ~active, idle)]
    for cond, work in cases:
        @pl.when(cond)
        def _(work=work):
            start_gather(i + ahead, (i + ahead) % GATHER_DEPTH)
            work(slot)
    cases = [(~more, compute)] if active is None else [(~more & active, compute), (~more & ~active, idle)]
    for cond, work in cases:
        @pl.when(cond)
        def _(work=work):
            work(slot)


def _pack_bf16_pair(lo, hi):
    lo_bits = pltpu.bitcast(lo.astype(BF16).astype(F32), jnp.uint32)
    hi_bits = pltpu.bitcast(hi.astype(BF16).astype(F32), jnp.uint32)
    return lax.shift_right_logical(lo_bits, jnp.uint32(16)) | (hi_bits & jnp.uint32(0xFFFF0000))


def _unpack_bf16_pair(word):
    lo = pltpu.bitcast(lax.shift_left(word, jnp.uint32(16)), F32)
    hi = pltpu.bitcast(word & jnp.uint32(0xFFFF0000), F32)
    return lo, hi


def _moe_ffn_kernel(blk_lo_ref, blk_hi_ref, nblk_ref, tok_ref, x_hbm,
                    wg_lo, wu_lo, wd_lo, wg_hi, wu_hi, wd_hi, y_ref, xbuf, sem):
    def start_gather(blk, s):
        base = blk * MOE_BLOCK
        for j in range(MOE_BLOCK):
            _row_copy(x_hbm, tok_ref[base + j], xbuf.at[s], j, sem.at[s]).start()

    def wait_gather(s):
        pltpu.make_async_copy(x_hbm.at[pl.ds(0, MOE_BLOCK), :], xbuf.at[s], sem.at[s]).wait()

    def ffn(s):
        x_left, x_right = _unpack_bf16_pair(xbuf[s])
        xb = jnp.concatenate([x_left.astype(BF16), x_right.astype(BF16)], axis=1)
        ys = []
        for wg, wu, wd in ((wg_lo, wu_lo, wd_lo), (wg_hi, wu_hi, wd_hi)):
            hg = jnp.dot(xb, wg[...], preferred_element_type=F32)
            hu = jnp.dot(xb, wu[...], preferred_element_type=F32)
            hb = (jax.nn.silu(hg) * hu).astype(BF16)
            ys.append(jnp.dot(hb, wd[...], preferred_element_type=F32))
        y_ref[...] = _pack_bf16_pair(ys[0], ys[1])

    def no_tokens(s):
        y_ref[...] = jnp.zeros(y_ref.shape, y_ref.dtype)

    _gather_ring_step(start_gather, wait_gather, ffn, active=pl.program_id(0) < nblk_ref[0], idle=no_tokens)


def moe_ffn(x1p, w_gate, w_up, w_down, blk_lo, blk_hi, nblk, row_tok):
    D = 2 * x1p.shape[1]
    n_blocks = blk_lo.shape[0]
    assert n_blocks >= GATHER_DEPTH
    n_rows = n_blocks * MOE_BLOCK
    w_in = lambda which: pl.BlockSpec((None, D, D_EXPERT), lambda i, lo, hi, nb, tk: ((lo, hi)[which][i], 0, 0))
    w_out = lambda which: pl.BlockSpec((None, D_EXPERT, D), lambda i, lo, hi, nb, tk: ((lo, hi)[which][i], 0, 0))
    grid_spec = pltpu.PrefetchScalarGridSpec(
        num_scalar_prefetch=4,
        grid=(n_blocks,),
        in_specs=[pl.BlockSpec(memory_space=pl.ANY), w_in(0), w_in(0), w_out(0), w_in(1), w_in(1), w_out(1)],
        out_specs=pl.BlockSpec((MOE_BLOCK, D), lambda i, lo, hi, nb, tk: (i, 0)),
        scratch_shapes=[pltpu.VMEM((GATHER_DEPTH, MOE_BLOCK, D // 2), jnp.uint32),
                        pltpu.SemaphoreType.DMA((GATHER_DEPTH,))],
    )
    return pl.pallas_call(
        _moe_ffn_kernel,
        grid_spec=grid_spec,
        out_shape=jax.ShapeDtypeStruct((n_rows, D), jnp.uint32),
        compiler_params=_params(1),
        name="moe_ffn",
    )(blk_lo, blk_hi, nblk, row_tok, x1p, w_gate, w_up, w_down, w_gate, w_up, w_down)


def _combine_ln_kernel(dest_ref, y_hbm, x1_ref, gate_ref, g_ref, beta_ref, x2_ref, x2b_ref, ybuf, sem, *, bm):
    def start_gather(tile, s):
        base = tile * bm
        for j in range(bm):
            _row_copy(y_hbm, dest_ref[base + j], ybuf.at[s], j, sem.at[s]).start(priority=j % 2)

    def wait_gather(s):
        pltpu.make_async_copy(y_hbm.at[pl.ds(0, bm), :], ybuf.at[s], sem.at[s]).wait()

    def finish(s):
        gate = gate_ref[...]
        y_lo, y_hi = _unpack_bf16_pair(ybuf[s])
        moe = y_lo * gate[:, 0:1] + y_hi * gate[:, 1:2]
        x2 = _layer_norm(ALPHA * x1_ref[...] + moe, g_ref[...], beta_ref[...])
        x2_ref[...] = x2
        x2b_ref[...] = x2.astype(BF16)

    _gather_ring_step(start_gather, wait_gather, finish)


def combine_ln(y_rows, dest, x1, gate_t, g, beta, *, bm=256):
    T, D = x1.shape
    assert T // bm >= GATHER_DEPTH
    grid_spec = pltpu.PrefetchScalarGridSpec(
        num_scalar_prefetch=1,
        grid=(T // bm,),
        in_specs=[pl.BlockSpec(memory_space=pl.ANY),
                  pl.BlockSpec((bm, D), lambda i, d: (i, 0)),
                  pl.BlockSpec((bm, TOP_K), lambda i, d: (i, 0)),
                  pl.BlockSpec((1, D), lambda i, d: (0, 0)),
                  pl.BlockSpec((1, D), lambda i, d: (0, 0))],
        out_specs=[pl.BlockSpec((bm, D), lambda i, d: (i, 0)),
                   pl.BlockSpec((bm, D), lambda i, d: (i, 0))],
        scratch_shapes=[pltpu.VMEM((GATHER_DEPTH, bm, D), jnp.uint32), pltpu.SemaphoreType.DMA((GATHER_DEPTH,))],
    )
    return pl.pallas_call(
        functools.partial(_combine_ln_kernel, bm=bm),
        grid_spec=grid_spec,
        out_shape=[jax.ShapeDtypeStruct((T, D), F32), jax.ShapeDtypeStruct((T, D), BF16)],
        compiler_params=_params(1),
        name="combine_ln",
    )(dest, y_rows, x1, gate_t, g, beta)


_PAIR_LO = np.array([a for a in range(EXPERTS_PER_GROUP) for b in range(a + 1, EXPERTS_PER_GROUP)], np.int32)
_PAIR_HI = np.array([b for a in range(EXPERTS_PER_GROUP) for b in range(a + 1, EXPERTS_PER_GROUP)], np.int32)
N_PAIRS = len(_PAIR_LO)
N_CLASSES = N_GROUPS * N_PAIRS


def moe_plan(eid, gate):
    T = eid.shape[1]
    n_blocks = (T + N_CLASSES * (MOE_BLOCK - 1) + MOE_BLOCK - 1) // MOE_BLOCK
    first_lower = eid[0] < eid[1]
    e_lo = jnp.minimum(eid[0], eid[1])
    e_hi = jnp.maximum(eid[0], eid[1])
    gate_t = jnp.stack([jnp.where(first_lower, gate[0], gate[1]), jnp.where(first_lower, gate[1], gate[0])], axis=1)
    a = e_lo % EXPERTS_PER_GROUP
    b = e_hi % EXPERTS_PER_GROUP
    pair = a * (2 * EXPERTS_PER_GROUP - 1 - a) // 2 + (b - a - 1)
    cls = (e_lo // EXPERTS_PER_GROUP) * N_PAIRS + pair
    onehot = (cls[:, None] == jnp.arange(N_CLASSES, dtype=jnp.int32)[None, :]).astype(jnp.int32)
    csum = jnp.cumsum(onehot, axis=0)
    rank = jnp.sum(csum * onehot, axis=1) - 1
    counts = csum[-1]
    pcounts = (counts + MOE_BLOCK - 1) // MOE_BLOCK * MOE_BLOCK
    pends = jnp.cumsum(pcounts)
    pstarts = pends - pcounts
    dest = (pstarts[cls] + rank).astype(jnp.int32)
    pad_tok = jnp.arange(n_blocks * MOE_BLOCK, dtype=jnp.int32) % T
    row_tok = pad_tok.at[dest].set(jnp.arange(T, dtype=jnp.int32))
    blk_start = jnp.arange(n_blocks, dtype=jnp.int32) * MOE_BLOCK
    blk_cls = jnp.minimum(jnp.sum((pends[None, :] <= blk_start[:, None]).astype(jnp.int32), axis=1), N_CLASSES - 1)
    blk_base = (blk_cls // N_PAIRS) * EXPERTS_PER_GROUP
    blk_lo = (blk_base + jnp.asarray(_PAIR_LO)[blk_cls % N_PAIRS]).astype(jnp.int32)
    blk_hi = (blk_base + jnp.asarray(_PAIR_HI)[blk_cls % N_PAIRS]).astype(jnp.int32)
    nblk = (pends[-1:] // MOE_BLOCK).astype(jnp.int32)
    return dest, row_tok, blk_lo, blk_hi, nblk, gate_t


def moe_layer(x1, x1p, eid, gate, w_gate, w_up, w_down, ln_g, ln_b):
    dest, row_tok, blk_lo, blk_hi, nblk, gate_t = moe_plan(eid, gate)
    y_rows = moe_ffn(x1p, w_gate.astype(BF16), w_up.astype(BF16), w_down.astype(BF16), blk_lo, blk_hi, nblk, row_tok)
    return combine_ln(y_rows, dest, x1, gate_t, ln_g.reshape(1, -1), ln_b.reshape(1, -1))


def _rope_cos_sin(pos, dim):
    inv_freq = ROPE_THETA ** (-jnp.arange(0, dim, 2, dtype=F32) / dim)
    ang = pos.astype(F32)[:, None] * inv_freq[None, :]
    cos = jnp.cos(ang)
    sin = jnp.sin(ang)
    return jnp.concatenate([cos, cos], axis=-1), jnp.concatenate([-sin, sin], axis=-1)


def axial_rope_tables(seq):
    pos = jnp.arange(seq)
    half = HEAD_DIM // 2
    cr, sr = _rope_cos_sin(pos // GRID_W, half)
    cc, sc = _rope_cos_sin(pos % GRID_W, half)
    return jnp.concatenate([cr, cc], axis=-1), jnp.concatenate([sr, sc], axis=-1)


def mla_rope_tables(seq):
    c, s = _rope_cos_sin(jnp.arange(seq), D_ROPE)
    z = jnp.zeros((seq, LANES - D_ROPE), F32)
    return jnp.concatenate([c, z], axis=-1), jnp.concatenate([s, z], axis=-1)


def router_layout(router_w, router_b):
    perm = np.array([g * EXPERTS_PER_GROUP + j for j in range(EXPERTS_PER_GROUP) for g in range(N_GROUPS)])
    rw = router_w.astype(F32).T[perm]
    hi = rw.astype(BF16)
    lo = (rw - hi.astype(F32)).astype(BF16)
    return hi, lo, router_b.astype(F32)[perm].reshape(N_EXPERTS, 1)


def mla_up_layout(w_q_up, w_kv_up, w_in_tail):
    rq = w_q_up.shape[0]
    wq = w_q_up.reshape(rq, D_HEADS, D_NOPE + D_ROPE)
    wq = jnp.pad(wq, ((0, 0), (0, 0), (0, 2 * LANES - (D_NOPE + D_ROPE)))).reshape(rq, D_HEADS * 2 * LANES)
    wc = jnp.pad(w_in_tail, ((0, 0), (0, LANES - D_ROPE)))
    return wq.astype(BF16), w_kv_up.astype(BF16), wc.astype(BF16)


def kernel(x, router_w, router_b, l0_w_in, l0_a_q_norm, l0_a_k_norm, l0_w_out, l0_ln1_g, l0_ln1_b, l0_w_gate, l0_w_up, l0_w_down, l0_ln2_g, l0_ln2_b, l1_w_in, l1_c_rpb, l1_d_q_norm, l1_d_w_q_up, l1_d_kv_norm, l1_d_w_kv_up, l1_w_out, l1_ln1_g, l1_ln1_b, l1_w_gate, l1_w_up, l1_w_down, l1_ln2_g, l1_ln2_b):
    batch, seq, d = x.shape
    T = batch * seq
    xf = x.reshape(T, d)
    rw_hi, rw_lo, rb = router_layout(router_w, router_b)
    row = lambda v: v.astype(F32).reshape(1, -1)

    a_q_dim = A_HEADS * HEAD_DIM
    a_qk_dim = a_q_dim + A_KV_HEADS * HEAD_DIM
    w0 = l0_w_in.astype(BF16)
    qk_scale = LOG2E * HEAD_DIM ** -0.5
    gain = jnp.concatenate([jnp.tile(l0_a_q_norm.astype(F32), A_HEADS) * qk_scale,
                            jnp.tile(l0_a_k_norm.astype(F32), A_KV_HEADS)]).reshape(1, -1)
    cos_a, sin_a = axial_rope_tables(seq)
    qk_a = proj_norm_rope(xf, w0[:, :a_qk_dim], gain, cos_a, sin_a, seq=seq)
    nb = A_KV_HEADS
    q_scale = jnp.full((B_HEADS * HEAD_DIM,), qk_scale, F32)
    scale0 = jnp.concatenate([jnp.ones((nb * HEAD_DIM,), F32), q_scale, jnp.ones((2 * B_HEADS * HEAD_DIM,), F32)])
    rest0 = proj_heads(xf, w0[:, a_qk_dim:], scale0.reshape(1, -1), heads_per_tile=13)
    group = A_HEADS // A_KV_HEADS
    oa = dense_attention(qk_a, qk_a, rest0, n_heads=A_HEADS, heads=group, k0=A_HEADS, v0=0, batch=batch, seq=seq,
                         group=group)
    ob = dilated_attention(rest0, dilated_bias_table(B_HEADS), n_heads=B_HEADS, heads=1, q0=nb, k0=nb + B_HEADS,
                           v0=nb + 2 * B_HEADS, batch=batch, seq=seq)
    x1, x1p = out_ln(oa, ob, l0_w_out.astype(BF16), xf, row(l0_ln1_g), row(l0_ln1_b))
    eid, gate = route(x1, rw_hi, rw_lo, rb)
    x2, x2b = moe_layer(x1, x1p, eid, gate, l0_w_gate, l0_w_up, l0_w_down, l0_ln2_g, l0_ln2_b)

    c_dim = C_HEADS * HEAD_DIM
    w1 = l1_w_in
    scale1 = jnp.concatenate([jnp.full((c_dim,), qk_scale, F32), jnp.ones((2 * c_dim,), F32)])
    qkv_c = proj_heads(x2b, w1[:, :3 * c_dim].astype(BF16), scale1.reshape(1, -1), heads_per_tile=12)
    oc = na_attention(qkv_c, na_bias_table(l1_c_rpb, seq // GRID_W), n_heads=C_HEADS, heads=2, q0=0, k0=C_HEADS,
                      v0=2 * C_HEADS, batch=batch, seq=seq)
    wq, wkv, wc = mla_up_layout(l1_d_w_q_up, l1_d_w_kv_up, w1[:, 3 * c_dim:])
    cos_d, sin_d = mla_rope_tables(seq)
    qd, kd, vd = mla_proj(x2b, wc, row(l1_d_q_norm), row(l1_d_kv_norm), wq, wkv, cos_d, sin_d, seq=seq,
                          scale=LOG2E * (D_NOPE + D_ROPE) ** -0.5)
    od = dense_attention(qd, kd, vd, n_heads=D_HEADS, heads=1, k0=0, v0=0, batch=batch, seq=seq, group=1)
    x3, x3p = out_ln(oc, od, l1_w_out.astype(BF16), x2, row(l1_ln1_g), row(l1_ln1_b))
    eid, gate = route(x3, rw_hi, rw_lo, rb)
    x4, _ = moe_layer(x3, x3p, eid, gate, l1_w_gate, l1_w_up, l1_w_down, l1_ln2_g, l1_ln2_b)
    return x4.reshape(batch, seq, d)
```

```python
import functools
import math

import jax
import jax.numpy as jnp
import numpy as np
from jax import lax
from jax.experimental import pallas as pl
from jax.experimental.pallas import tpu as pltpu

D_MODEL = 2048
DEPTH = 2
GRID_W = 64
HEAD_DIM = 128
ROPE_THETA = 10000.0
RMS_EPS = 1e-6
LN_EPS = 1e-5

A_HEADS = 8
A_KV_HEADS = 2
B_HEADS = 8
B_BRANCHES = ((128, 1), (512, 4), (2048, 16))
C_HEADS = 8
NA_ROWS = 8
NA_COLS = 16
D_HEADS = 8
D_Q_RANK = 512
D_KV_RANK = 256
D_NOPE = 128
D_ROPE = 64
D_V = 128

N_EXPERTS = 32
N_GROUPS = 8
EXPERTS_PER_GROUP = N_EXPERTS // N_GROUPS
TOP_K = 2
D_EXPERT = 512
MOE_BLOCK = 256

ALPHA = (2.0 * DEPTH) ** 0.25
LOG2E = math.log2(math.e)

LANES = 128
VMEM_LIMIT = 56 * 1024 * 1024

BF16 = jnp.bfloat16
F32 = jnp.float32


def _params(n_axes, **flags):
    return pltpu.CompilerParams(dimension_semantics=("arbitrary",) * n_axes, vmem_limit_bytes=VMEM_LIMIT,
                                flags=flags or None)


def _rot_half32(y):
    lane = lax.broadcasted_iota(jnp.int32, y.shape, y.ndim - 1)
    up = pltpu.roll(y, 96, axis=y.ndim - 1)
    dn = pltpu.roll(y, 32, axis=y.ndim - 1)
    return jnp.where((lane % 64) < 32, up, dn)


def _proj_heads_kernel(x_ref, w_ref, c_ref, o_ref, *, heads_per_tile):
    acc = jnp.dot(x_ref[...].astype(BF16), w_ref[...], preferred_element_type=F32) * c_ref[...]
    for h in range(heads_per_tile):
        o_ref[h] = acc[:, h * LANES:(h + 1) * LANES].astype(o_ref.dtype)


def proj_heads(x, w, col_scale, *, heads_per_tile, bm=512):
    T, D = x.shape
    H = w.shape[1] // LANES
    bn = heads_per_tile * LANES
    grid = (H // heads_per_tile, T // bm)
    return pl.pallas_call(
        functools.partial(_proj_heads_kernel, heads_per_tile=heads_per_tile),
        grid=grid,
        in_specs=[pl.BlockSpec((bm, D), lambda j, i: (i, 0)),
                  pl.BlockSpec((D, bn), lambda j, i: (0, j)),
                  pl.BlockSpec((1, bn), lambda j, i: (0, j))],
        out_specs=pl.BlockSpec((heads_per_tile, bm, LANES), lambda j, i: (j, i, 0)),
        out_shape=jax.ShapeDtypeStruct((H, T, LANES), BF16),
        compiler_params=_params(2),
        name="proj_heads",
    )(x, w, col_scale)


def _resident(shape):
    return pl.BlockSpec(shape, lambda *_: (0,) * len(shape), pipeline_mode=pl.Buffered(1))


PROJ_SUBTILES = 2


def _proj_norm_rope_kernel(x_ref, w_ref, g_ref, cos_ref, sin_ref, o_ref):
    sub = x_ref.shape[0] // PROJ_SUBTILES
    for t in range(PROJ_SUBTILES):
        rows = slice(t * sub, (t + 1) * sub)
        acc = jnp.dot(x_ref[rows, :].astype(BF16), w_ref[...], preferred_element_type=F32)
        cos = cos_ref[rows, :]
        sin = sin_ref[rows, :]
        for h in range(o_ref.shape[0]):
            a = acc[:, h * LANES:(h + 1) * LANES]
            ms = jnp.mean(a * a, axis=-1, keepdims=True)
            y = a * lax.rsqrt(ms + RMS_EPS) * g_ref[:, h * LANES:(h + 1) * LANES]
            o_ref[h, rows, :] = (y * cos + _rot_half32(y) * sin).astype(o_ref.dtype)


def proj_norm_rope(x, w, gain, cos, sin, *, seq, bm=512):
    T, D = x.shape
    H = w.shape[1] // LANES
    nsb = seq // bm
    return pl.pallas_call(
        _proj_norm_rope_kernel,
        grid=(T // bm,),
        in_specs=[pl.BlockSpec((bm, D), lambda i: (i, 0)),
                  _resident(w.shape), _resident(gain.shape),
                  pl.BlockSpec((bm, LANES), lambda i: (i % nsb, 0)),
                  pl.BlockSpec((bm, LANES), lambda i: (i % nsb, 0))],
        out_specs=pl.BlockSpec((H, bm, LANES), lambda i: (0, i, 0)),
        out_shape=jax.ShapeDtypeStruct((H, T, LANES), BF16),
        compiler_params=_params(1),
        name="proj_norm_rope",
    )(x, w, gain, cos, sin)


def _mla_proj_kernel(x_ref, wc_ref, gq_ref, gkv_ref, wq_ref, wkv_ref, cos_ref, sin_ref,
                     q_ref, k_ref, v_ref, *, scale):
    c = jnp.dot(x_ref[...], wc_ref[...], preferred_element_type=F32)
    cos = cos_ref[...]
    sin = sin_ref[...]

    def rms(t, g):
        ms = jnp.mean(t * t, axis=-1, keepdims=True)
        return t * lax.rsqrt(ms + RMS_EPS) * g

    cq = rms(c[:, :D_Q_RANK], gq_ref[...]).astype(BF16)
    ckv = rms(c[:, D_Q_RANK:D_Q_RANK + D_KV_RANK], gkv_ref[...]).astype(BF16)
    kr = c[:, D_Q_RANK + D_KV_RANK:]
    kr = (kr * cos + _rot_half32(kr) * sin).astype(k_ref.dtype)
    q = jnp.dot(cq, wq_ref[...], preferred_element_type=F32) * scale
    kv = jnp.dot(ckv, wkv_ref[...], preferred_element_type=F32)
    for h in range(D_HEADS):
        base = h * 2 * LANES
        q_ref[h, :, :LANES] = q[:, base:base + LANES].astype(q_ref.dtype)
        qr = q[:, base + LANES:base + 2 * LANES]
        q_ref[h, :, LANES:] = (qr * cos + _rot_half32(qr) * sin).astype(q_ref.dtype)
        k_ref[h, :, :LANES] = kv[:, base:base + LANES].astype(k_ref.dtype)
        k_ref[h, :, LANES:] = kr
        v_ref[h] = kv[:, base + LANES:base + 2 * LANES].astype(v_ref.dtype)


def mla_proj(x, wc, gq, gkv, wq, wkv, cos, sin, *, seq, scale, bm=512):
    T, D = x.shape
    nsb = seq // bm
    return pl.pallas_call(
        functools.partial(_mla_proj_kernel, scale=scale),
        grid=(T // bm,),
        in_specs=[pl.BlockSpec((bm, D), lambda i: (i, 0)),
                  _resident(wc.shape), _resident(gq.shape), _resident(gkv.shape), _resident(wq.shape),
                  _resident(wkv.shape),
                  pl.BlockSpec((bm, LANES), lambda i: (i % nsb, 0)),
                  pl.BlockSpec((bm, LANES), lambda i: (i % nsb, 0))],
        out_specs=[pl.BlockSpec((D_HEADS, bm, 2 * LANES), lambda i: (0, i, 0)),
                   pl.BlockSpec((D_HEADS, bm, 2 * LANES), lambda i: (0, i, 0)),
                   pl.BlockSpec((D_HEADS, bm, LANES), lambda i: (0, i, 0))],
        out_shape=[jax.ShapeDtypeStruct((D_HEADS, T, 2 * LANES), BF16),
                   jax.ShapeDtypeStruct((D_HEADS, T, 2 * LANES), BF16),
                   jax.ShapeDtypeStruct((D_HEADS, T, LANES), BF16)],
        compiler_params=_params(1),
        name="mla_proj",
    )(x, wc, gq, gkv, wq, wkv, cos, sin)


DENSE_TQ = 512
ATTN_TQ = 256


def _attn_pipeline(q_ref, k_ref, v_ref, o_ref, vext_ref, s_buf, m_buf, p_buf, *, heads, group, win, win_start, bias):
    tq = s_buf[0].shape[0]
    S = q_ref.shape[1]
    nblk = S // tq
    total = heads * nblk
    vext_ref[:, :, :LANES] = v_ref[...]
    vext_ref[:, :, LANES:] = jnp.ones(v_ref.shape, vext_ref.dtype)

    def locate(n):
        h, blk = (0, n) if heads == 1 else (n // nblk, n % nblk)
        rows = pl.ds(blk * tq, tq) if isinstance(blk, int) else pl.ds(pl.multiple_of(blk * tq, tq), tq)
        keys = slice(None) if win == S else pl.ds(win_start(blk), win)
        return h, blk, rows, keys

    def scores(n, par):
        h, blk, rows, keys = locate(n)
        s = lax.dot_general(q_ref[h, rows, :], k_ref[h // group, keys, :], (((1,), (1,)), ((), ())),
                            preferred_element_type=F32)
        if bias is not None:
            s = s + bias(h, blk)
        s_buf[par][...] = s
        m_buf[par][...] = jnp.max(s, axis=-1, keepdims=True)

    def exps(par):
        p_buf[par][...] = jnp.exp2(s_buf[par][...] - m_buf[par][...]).astype(BF16)

    def wsum(n, par):
        h, blk, rows, keys = locate(n)
        o_ext = jnp.dot(p_buf[par][...], vext_ref[h // group, keys, :], preferred_element_type=F32)
        o_ref[h, rows, :] = (o_ext[:, :LANES] / o_ext[:, LANES:]).astype(o_ref.dtype)

    scores(0, 0)
    exps(0)
    scores(1, 1)

    def step(j, carry):
        n = 2 * j + 1
        exps(1)
        wsum(n - 1, 0)
        scores(n + 1, 0)
        exps(0)
        wsum(n, 1)
        scores(n + 2, 1)
        return carry

    lax.fori_loop(0, total // 2 - 1, step, 0)
    exps(1)
    wsum(total - 2, 0)
    wsum(total - 1, 1)


def _attn_scratch(kv_heads, seq, win, tq):
    return [pltpu.VMEM((kv_heads, seq, 2 * LANES), BF16),
            pltpu.VMEM((tq, win), F32), pltpu.VMEM((tq, win), F32),
            pltpu.VMEM((tq, 1), F32), pltpu.VMEM((tq, 1), F32),
            pltpu.VMEM((tq, win), BF16), pltpu.VMEM((tq, win), BF16)]


def _dense_attn_kernel(q_ref, k_ref, v_ref, o_ref, vext_ref, s0, s1, m0, m1, p0, p1, *, heads, group):
    _attn_pipeline(q_ref, k_ref, v_ref, o_ref, vext_ref, (s0, s1), (m0, m1), (p0, p1), heads=heads, group=group,
                   win=q_ref.shape[1], win_start=None, bias=None)


def dense_attention(q, k, v, *, n_heads, heads, k0, v0, batch, seq, group):
    T, dq = q.shape[1:]
    kvh = heads // group
    assert n_heads % heads == 0 and heads % group == 0 and k0 % kvh == 0 and v0 % kvh == 0
    return pl.pallas_call(
        functools.partial(_dense_attn_kernel, heads=heads, group=group),
        grid=(batch, n_heads // heads),
        in_specs=[pl.BlockSpec((heads, seq, dq), lambda b, g: (g, b, 0)),
                  pl.BlockSpec((kvh, seq, dq), lambda b, g: (k0 // kvh + g, b, 0)),
                  pl.BlockSpec((kvh, seq, LANES), lambda b, g: (v0 // kvh + g, b, 0))],
        out_specs=pl.BlockSpec((heads, seq, LANES), lambda b, g: (g, b, 0)),
        out_shape=jax.ShapeDtypeStruct((n_heads, T, LANES), BF16),
        scratch_shapes=_attn_scratch(kvh, seq, seq, DENSE_TQ),
        compiler_params=_params(2),
        name="dense_attention",
    )(q, k, v)


DIL_REACH = max(w // 2 for w, _ in B_BRANCHES)
DIL_WIN = 2 * DIL_REACH + ATTN_TQ
DIL_TABLE_TILES = (DIL_WIN + 2 * DIL_REACH) // LANES


def _dilated_attn_kernel(q_ref, k_ref, v_ref, bias_ref, o_ref, vext_ref, s0, s1, m0, m1, p0, p1, *, heads):
    S = q_ref.shape[1]

    def win_start(blk):
        return pl.multiple_of(jnp.clip(blk * ATTN_TQ - DIL_REACH, 0, S - DIL_WIN), ATTN_TQ)

    def bias(h, blk):
        tile0 = (win_start(blk) - blk * ATTN_TQ + 2 * DIL_REACH) // LANES
        return jnp.concatenate([bias_ref[h, tile0 + t] for t in range(DIL_WIN // LANES)], axis=1)

    _attn_pipeline(q_ref, k_ref, v_ref, o_ref, vext_ref, (s0, s1), (m0, m1), (p0, p1), heads=heads, group=1,
                   win=DIL_WIN, win_start=win_start, bias=bias)


def dilated_bias_table(n_heads):
    col = (np.arange(DIL_TABLE_TILES)[:, None, None] * LANES + np.arange(LANES)[None, None, :])
    d = np.abs(col - np.arange(ATTN_TQ)[None, :, None] - 2 * DIL_REACH)
    mult = np.zeros(d.shape, np.float64)
    for window, dil in B_BRANCHES:
        mult += ((d % dil == 0) & (d <= window // 2)).astype(np.float64)
    with np.errstate(divide="ignore"):
        logm = np.log(mult)
    slopes = jnp.asarray(2.0 ** (-8.0 * np.arange(1, n_heads + 1) / n_heads), dtype=F32)
    bias = jnp.asarray(logm, dtype=F32)[None] - slopes[:, None, None, None] * jnp.asarray(d, dtype=F32)[None]
    return bias * LOG2E


def dilated_attention(qkv, bias, *, n_heads, heads, q0, k0, v0, batch, seq):
    T = qkv.shape[1]
    assert n_heads % heads == 0 and q0 % heads == 0 and k0 % heads == 0 and v0 % heads == 0
    qkv_spec = lambda h0: pl.BlockSpec((heads, seq, LANES), lambda b, g: (h0 // heads + g, b, 0))
    return pl.pallas_call(
        functools.partial(_dilated_attn_kernel, heads=heads),
        grid=(batch, n_heads // heads),
        in_specs=[qkv_spec(q0), qkv_spec(k0), qkv_spec(v0),
                  pl.BlockSpec((heads, DIL_TABLE_TILES, ATTN_TQ, LANES), lambda b, g: (g, 0, 0, 0))],
        out_specs=pl.BlockSpec((heads, seq, LANES), lambda b, g: (g, b, 0)),
        out_shape=jax.ShapeDtypeStruct((n_heads, T, LANES), BF16),
        scratch_shapes=_attn_scratch(heads, seq, DIL_WIN, ATTN_TQ),
        compiler_params=_params(2),
        name="dilated_attention",
    )(qkv, qkv, qkv, bias)


NA_QROWS = ATTN_TQ // GRID_W
NA_KROWS = NA_QROWS + NA_ROWS
NA_WIN = NA_KROWS * GRID_W


def _na_window_row(blk, rows):
    lo = blk * NA_QROWS - NA_ROWS // 2
    return jnp.clip(lo, 0, rows - NA_KROWS) if not isinstance(blk, int) else min(max(lo, 0), rows - NA_KROWS)


def _na_attn_kernel(q_ref, k_ref, v_ref, bias_ref, o_ref, vext_ref, s0, s1, m0, m1, p0, p1, *, heads, rows):
    nblk = rows // NA_QROWS

    def win_start(blk):
        return pl.multiple_of(_na_window_row(blk, rows) * GRID_W, GRID_W)

    def bias(h, blk):
        cls = (blk > 0).astype(jnp.int32) + (blk == nblk - 1).astype(jnp.int32) if not isinstance(blk, int) \
            else int(blk > 0) + int(blk == nblk - 1)
        return bias_ref[h, cls]

    _attn_pipeline(q_ref, k_ref, v_ref, o_ref, vext_ref, (s0, s1), (m0, m1), (p0, p1), heads=heads, group=1,
                   win=NA_WIN, win_start=win_start, bias=bias)


def na_bias_table(rpb, rows):
    H = rpb.shape[0]
    col = np.arange(GRID_W)
    c0 = np.clip(col - NA_COLS // 2, 0, GRID_W - NA_COLS)
    col_ok = (col[None, :] >= c0[:, None]) & (col[None, :] < c0[:, None] + NA_COLS)
    dcol = np.clip(col[None, :] - col[:, None] + NA_COLS - 1, 0, 2 * NA_COLS - 2)
    pick = ((dcol[:, :, None] == np.arange(2 * NA_COLS - 1)) & col_ok[:, :, None]).astype(np.float32)
    t = jnp.einsum("hrd,qkd->hrqk", rpb.astype(F32), jnp.asarray(pick), precision=lax.Precision.HIGHEST)
    t = jnp.where(jnp.asarray(col_ok)[None, None], t * LOG2E, -jnp.inf)
    nblk = rows // NA_QROWS

    def geometry(blk):
        r = blk * NA_QROWS + np.arange(NA_QROWS)
        kr = _na_window_row(blk, rows) + np.arange(NA_KROWS)
        r0 = np.clip(r - NA_ROWS // 2, 0, rows - NA_ROWS)
        valid = (kr[None, :] >= r0[:, None]) & (kr[None, :] < r0[:, None] + NA_ROWS)
        drow = np.clip(kr[None, :] - r[:, None] + NA_ROWS - 1, 0, 2 * NA_ROWS - 2)
        return valid, np.where(valid, drow, 0)

    inner = geometry(1)
    assert all(np.array_equal(a, b) for blk in range(1, nblk - 1) for a, b in zip(geometry(blk), inner))
    tables = []
    for blk in (0, 1, nblk - 1):
        valid, drow = geometry(blk)
        slab = jnp.where(jnp.asarray(valid)[None, :, :, None, None], t[:, drow], -jnp.inf)
        tables.append(slab.transpose(0, 1, 3, 2, 4).reshape(H, ATTN_TQ, NA_WIN))
    return jnp.stack(tables, axis=1)


def na_attention(qkv, bias, *, n_heads, heads, q0, k0, v0, batch, seq):
    T = qkv.shape[1]
    rows = seq // GRID_W
    assert n_heads % heads == 0 and q0 % heads == 0 and k0 % heads == 0 and v0 % heads == 0
    assert rows % NA_QROWS == 0 and rows // NA_QROWS >= 3
    qkv_spec = lambda h0: pl.BlockSpec((heads, seq, LANES), lambda b, g: (h0 // heads + g, b, 0))
    return pl.pallas_call(
        functools.partial(_na_attn_kernel, heads=heads, rows=rows),
        grid=(batch, n_heads // heads),
        in_specs=[qkv_spec(q0), qkv_spec(k0), qkv_spec(v0),
                  pl.BlockSpec((heads, 3, ATTN_TQ, NA_WIN), lambda b, g: (g, 0, 0, 0))],
        out_specs=pl.BlockSpec((heads, seq, LANES), lambda b, g: (g, b, 0)),
        out_shape=jax.ShapeDtypeStruct((n_heads, T, LANES), BF16),
        scratch_shapes=_attn_scratch(heads, seq, NA_WIN, ATTN_TQ),
        compiler_params=_params(2),
        name="na_attention",
    )(qkv, qkv, qkv, bias)


def _layer_norm(z, g, b):
    mu = jnp.mean(z, axis=-1, keepdims=True)
    zc = z - mu
    var = jnp.mean(zc * zc, axis=-1, keepdims=True)
    return zc * lax.rsqrt(var + LN_EPS) * g + b


def _top2_of4(v0, v1, v2, v3):
    a = jnp.maximum(v0, v1)
    b = jnp.minimum(v0, v1)
    c = jnp.maximum(v2, v3)
    d = jnp.minimum(v2, v3)
    return jnp.maximum(a, c), jnp.maximum(jnp.minimum(a, c), jnp.maximum(b, d))


def _route(logits_t, rb):
    G = N_GROUPS
    scores = jax.nn.sigmoid(logits_t)
    sel = scores + rb
    sj = [sel[j * G:(j + 1) * G] for j in range(EXPERTS_PER_GROUP)]
    cj = [scores[j * G:(j + 1) * G] for j in range(EXPERTS_PER_GROUP)]
    t1, t2 = _top2_of4(*sj)
    grp = t1 + t2
    gi = lax.broadcasted_iota(jnp.int32, grp.shape, 0).astype(F32)
    gbest = jnp.min(jnp.where(grp == jnp.max(grp, axis=0, keepdims=True), gi, float(G)), axis=0, keepdims=True)
    pick = gi == gbest
    v = [jnp.sum(jnp.where(pick, s, 0.0), axis=0, keepdims=True) for s in sj]
    c = [jnp.sum(jnp.where(pick, s, 0.0), axis=0, keepdims=True) for s in cj]
    neg = jnp.float32(-jnp.inf)
    m1 = jnp.maximum(jnp.maximum(v[0], v[1]), jnp.maximum(v[2], v[3]))
    j1 = jnp.where(v[0] == m1, 0, jnp.where(v[1] == m1, 1, jnp.where(v[2] == m1, 2, 3)))
    w = [jnp.where(j1 == j, neg, v[j]) for j in range(4)]
    m2 = jnp.maximum(jnp.maximum(w[0], w[1]), jnp.maximum(w[2], w[3]))
    j2 = jnp.where(w[0] == m2, 0, jnp.where(w[1] == m2, 1, jnp.where(w[2] == m2, 2, 3)))
    g1 = jnp.where(j1 == 0, c[0], jnp.where(j1 == 1, c[1], jnp.where(j1 == 2, c[2], c[3])))
    g2 = jnp.where(j2 == 0, c[0], jnp.where(j2 == 1, c[1], jnp.where(j2 == 2, c[2], c[3])))
    tot = g1 + g2
    e0 = gbest.astype(jnp.int32) * EXPERTS_PER_GROUP
    return (e0 + j1, e0 + j2), (g1 / tot, g2 / tot)


OUT_SUBTILES = 2


def _out_ln_kernel(a_ref, b_ref, w_ref, x_ref, g_ref, beta_ref, x1_ref):
    sub = x_ref.shape[0] // OUT_SUBTILES
    for t in range(OUT_SUBTILES):
        rows = slice(t * sub, (t + 1) * sub)
        heads = [a_ref[h, rows, :] for h in range(a_ref.shape[0])] + [b_ref[h, rows, :] for h in range(b_ref.shape[0])]
        mix = jnp.dot(jnp.concatenate(heads, axis=1), w_ref[...], preferred_element_type=F32)
        x1_ref[rows, :] = _layer_norm(ALPHA * x_ref[rows, :] + mix, g_ref[...], beta_ref[...])


def out_ln(a, b, w, x, g, beta, *, bm=512):
    T, D = x.shape
    return pl.pallas_call(
        _out_ln_kernel,
        grid=(T // bm,),
        in_specs=[pl.BlockSpec((a.shape[0], bm, LANES), lambda i: (0, i, 0)),
                  pl.BlockSpec((b.shape[0], bm, LANES), lambda i: (0, i, 0)),
                  _resident(w.shape),
                  pl.BlockSpec((bm, D), lambda i: (i, 0)),
                  _resident(g.shape), _resident(beta.shape)],
        out_specs=pl.BlockSpec((bm, D), lambda i: (i, 0)),
        out_shape=jax.ShapeDtypeStruct((T, D), F32),
        compiler_params=_params(1),
        name="out_ln",
    )(a, b, w, x, g, beta)


def _route_kernel(x_ref, rwh_ref, rwl_ref, rb_ref, eid_ref, gate_ref):
    x = x_ref[...]
    hi = x.astype(BF16)
    lo = (x - hi.astype(F32)).astype(BF16)
    nt = (((1,), (1,)), ((), ()))
    logits_t = (lax.dot_general(rwh_ref[...], hi, nt, preferred_element_type=F32)
                + lax.dot_general(rwl_ref[...], hi, nt, preferred_element_type=F32)
                + lax.dot_general(rwh_ref[...], lo, nt, preferred_element_type=F32))
    eid, gate = _route(logits_t, rb_ref[...])
    for k in range(TOP_K):
        eid_ref[k:k + 1, :] = eid[k]
        gate_ref[k:k + 1, :] = gate[k]


def route(x1, rw_hi, rw_lo, rb, *, bm=1024):
    T, D = x1.shape
    return pl.pallas_call(
        _route_kernel,
        grid=(T // bm,),
        in_specs=[pl.BlockSpec((bm, D), lambda i: (i, 0)),
                  _resident(rw_hi.shape), _resident(rw_lo.shape), _resident(rb.shape)],
        out_specs=[pl.BlockSpec((TOP_K, bm), lambda i: (0, i)),
                   pl.BlockSpec((TOP_K, bm), lambda i: (0, i))],
        out_shape=[jax.ShapeDtypeStruct((TOP_K, T), jnp.int32),
                   jax.ShapeDtypeStruct((TOP_K, T), F32)],
        compiler_params=_params(1),
        name="route",
    )(x1, rw_hi, rw_lo, rb)


GATHER_DEPTH = 3


def _row_copy(src_hbm, row, dst_vmem, j, sem):
    return pltpu.make_async_copy(src_hbm.at[pl.ds(row, 1), :], dst_vmem.at[pl.ds(j, 1), :], sem)


def _gather_ring_step(start_gather, wait_gather, compute, active=None, idle=None):
    i = pl.program_id(0)
    n_steps = pl.num_programs(0)
    ahead = GATHER_DEPTH - 1
    slot = i % GATHER_DEPTH

    @pl.when(i == 0)
    def _():
        for a in range(ahead):
            start_gather(a, a)

    wait_gather(slot)
    more = i + ahead < n_steps
    cases = [(more, compute)] if active is None else [(more & active, compute), (more & ~active, idle)]
    for cond, work in cases:
        @pl.when(cond)
        def _(work=work):
            start_gather(i + ahead, (i + ahead) % GATHER_DEPTH)
            work(slot)
    cases = [(~more, compute)] if active is None else [(~more & active, compute), (~more & ~active, idle)]
    for cond, work in cases:
        @pl.when(cond)
        def _(work=work):
            work(slot)


def _moe_ffn_kernel(blk_lo_ref, blk_hi_ref, nblk_ref, tok_ref, x_hbm,
                    wg_lo, wu_lo, wd_lo, wg_hi, wu_hi, wd_hi, y_ref, xbuf, sem):
    def start_gather(blk, s):
        base = blk * MOE_BLOCK
        for j in range(MOE_BLOCK):
            _row_copy(x_hbm, tok_ref[base + j], xbuf.at[s], j, sem.at[s]).start()

    def wait_gather(s):
        pltpu.make_async_copy(x_hbm.at[pl.ds(0, MOE_BLOCK), :], xbuf.at[s], sem.at[s]).wait()

    def ffn(s):
        D = xbuf.shape[-1]
        xb = xbuf[s].astype(BF16)
        for half, (wg, wu, wd) in enumerate(((wg_lo, wu_lo, wd_lo), (wg_hi, wu_hi, wd_hi))):
            hg = jnp.dot(xb, wg[...], preferred_element_type=F32)
            hu = jnp.dot(xb, wu[...], preferred_element_type=F32)
            hb = (jax.nn.silu(hg) * hu).astype(BF16)
            y_ref[:, half * D:(half + 1) * D] = jnp.dot(hb, wd[...], preferred_element_type=F32)

    def no_tokens(s):
        y_ref[...] = jnp.zeros(y_ref.shape, y_ref.dtype)

    _gather_ring_step(start_gather, wait_gather, ffn, active=pl.program_id(0) < nblk_ref[0], idle=no_tokens)


def moe_ffn(x1, w_gate, w_up, w_down, blk_lo, blk_hi, nblk, row_tok):
    D = x1.shape[1]
    n_blocks = blk_lo.shape[0]
    assert n_blocks >= GATHER_DEPTH
    n_rows = n_blocks * MOE_BLOCK
    w_in = lambda which: pl.BlockSpec((None, D, D_EXPERT), lambda i, lo, hi, nb, tk: ((lo, hi)[which][i], 0, 0))
    w_out = lambda which: pl.BlockSpec((None, D_EXPERT, D), lambda i, lo, hi, nb, tk: ((lo, hi)[which][i], 0, 0))
    grid_spec = pltpu.PrefetchScalarGridSpec(
        num_scalar_prefetch=4,
        grid=(n_blocks,),
        in_specs=[pl.BlockSpec(memory_space=pl.ANY), w_in(0), w_in(0), w_out(0), w_in(1), w_in(1), w_out(1)],
        out_specs=pl.BlockSpec((MOE_BLOCK, 2 * D), lambda i, lo, hi, nb, tk: (i, 0)),
        scratch_shapes=[pltpu.VMEM((GATHER_DEPTH, MOE_BLOCK, D), F32), pltpu.SemaphoreType.DMA((GATHER_DEPTH,))],
    )
    return pl.pallas_call(
        _moe_ffn_kernel,
        grid_spec=grid_spec,
        out_shape=jax.ShapeDtypeStruct((n_rows, 2 * D), F32),
        compiler_params=_params(1),
        name="moe_ffn",
    )(blk_lo, blk_hi, nblk, row_tok, x1, w_gate, w_up, w_down, w_gate, w_up, w_down)


def _combine_ln_kernel(dest_ref, y_hbm, x1_ref, gate_ref, g_ref, beta_ref, x2_ref, x2b_ref, ybuf, sem, *, bm):
    def start_gather(tile, s):
        base = tile * bm
        for j in range(bm):
            _row_copy(y_hbm, dest_ref[base + j], ybuf.at[s], j, sem.at[s]).start()

    def wait_gather(s):
        pltpu.make_async_copy(y_hbm.at[pl.ds(0, bm), :], ybuf.at[s], sem.at[s]).wait()

    def finish(s):
        D = x1_ref.shape[-1]
        gate = gate_ref[...]
        moe = ybuf[s, :, :D] * gate[:, 0:1] + ybuf[s, :, D:] * gate[:, 1:2]
        x2 = _layer_norm(ALPHA * x1_ref[...] + moe, g_ref[...], beta_ref[...])
        x2_ref[...] = x2
        x2b_ref[...] = x2.astype(BF16)

    _gather_ring_step(start_gather, wait_gather, finish)


def combine_ln(y_rows, dest, x1, gate_t, g, beta, *, bm=256):
    T, D = x1.shape
    assert T // bm >= GATHER_DEPTH
    grid_spec = pltpu.PrefetchScalarGridSpec(
        num_scalar_prefetch=1,
        grid=(T // bm,),
        in_specs=[pl.BlockSpec(memory_space=pl.ANY),
                  pl.BlockSpec((bm, D), lambda i, d: (i, 0)),
                  pl.BlockSpec((bm, TOP_K), lambda i, d: (i, 0)),
                  pl.BlockSpec((1, D), lambda i, d: (0, 0)),
                  pl.BlockSpec((1, D), lambda i, d: (0, 0))],
        out_specs=[pl.BlockSpec((bm, D), lambda i, d: (i, 0)),
                   pl.BlockSpec((bm, D), lambda i, d: (i, 0))],
        scratch_shapes=[pltpu.VMEM((GATHER_DEPTH, bm, TOP_K * D), F32), pltpu.SemaphoreType.DMA((GATHER_DEPTH,))],
    )
    return pl.pallas_call(
        functools.partial(_combine_ln_kernel, bm=bm),
        grid_spec=grid_spec,
        out_shape=[jax.ShapeDtypeStruct((T, D), F32), jax.ShapeDtypeStruct((T, D), BF16)],
        compiler_params=_params(1),
        name="combine_ln",
    )(dest, y_rows, x1, gate_t, g, beta)


_PAIR_LO = np.array([a for a in range(EXPERTS_PER_GROUP) for b in range(a + 1, EXPERTS_PER_GROUP)], np.int32)
_PAIR_HI = np.array([b for a in range(EXPERTS_PER_GROUP) for b in range(a + 1, EXPERTS_PER_GROUP)], np.int32)
N_PAIRS = len(_PAIR_LO)
N_CLASSES = N_GROUPS * N_PAIRS


def moe_plan(eid, gate):
    T = eid.shape[1]
    n_blocks = (T + N_CLASSES * (MOE_BLOCK - 1) + MOE_BLOCK - 1) // MOE_BLOCK
    first_lower = eid[0] < eid[1]
    e_lo = jnp.minimum(eid[0], eid[1])
    e_hi = jnp.maximum(eid[0], eid[1])
    gate_t = jnp.stack([jnp.where(first_lower, gate[0], gate[1]), jnp.where(first_lower, gate[1], gate[0])], axis=1)
    a = e_lo % EXPERTS_PER_GROUP
    b = e_hi % EXPERTS_PER_GROUP
    pair = a * (2 * EXPERTS_PER_GROUP - 1 - a) // 2 + (b - a - 1)
    cls = (e_lo // EXPERTS_PER_GROUP) * N_PAIRS + pair
    onehot = (cls[:, None] == jnp.arange(N_CLASSES, dtype=jnp.int32)[None, :]).astype(jnp.int32)
    csum = jnp.cumsum(onehot, axis=0)
    rank = jnp.sum(csum * onehot, axis=1) - 1
    counts = csum[-1]
    pcounts = (counts + MOE_BLOCK - 1) // MOE_BLOCK * MOE_BLOCK
    pends = jnp.cumsum(pcounts)
    pstarts = pends - pcounts
    dest = (pstarts[cls] + rank).astype(jnp.int32)
    pad_tok = jnp.arange(n_blocks * MOE_BLOCK, dtype=jnp.int32) % T
    row_tok = pad_tok.at[dest].set(jnp.arange(T, dtype=jnp.int32))
    blk_start = jnp.arange(n_blocks, dtype=jnp.int32) * MOE_BLOCK
    blk_cls = jnp.minimum(jnp.sum((pends[None, :] <= blk_start[:, None]).astype(jnp.int32), axis=1), N_CLASSES - 1)
    blk_base = (blk_cls // N_PAIRS) * EXPERTS_PER_GROUP
    blk_lo = (blk_base + jnp.asarray(_PAIR_LO)[blk_cls % N_PAIRS]).astype(jnp.int32)
    blk_hi = (blk_base + jnp.asarray(_PAIR_HI)[blk_cls % N_PAIRS]).astype(jnp.int32)
    nblk = (pends[-1:] // MOE_BLOCK).astype(jnp.int32)
    return dest, row_tok, blk_lo, blk_hi, nblk, gate_t


def moe_layer(x1, eid, gate, w_gate, w_up, w_down, ln_g, ln_b):
    dest, row_tok, blk_lo, blk_hi, nblk, gate_t = moe_plan(eid, gate)
    y_rows = moe_ffn(x1, w_gate.astype(BF16), w_up.astype(BF16), w_down.astype(BF16), blk_lo, blk_hi, nblk, row_tok)
    return combine_ln(y_rows, dest, x1, gate_t, ln_g.reshape(1, -1), ln_b.reshape(1, -1))


def _rope_cos_sin(pos, dim):
    inv_freq = ROPE_THETA ** (-jnp.arange(0, dim, 2, dtype=F32) / dim)
    ang = pos.astype(F32)[:, None] * inv_freq[None, :]
    cos = jnp.cos(ang)
    sin = jnp.sin(ang)
    return jnp.concatenate([cos, cos], axis=-1), jnp.concatenate([-sin, sin], axis=-1)


def axial_rope_tables(seq):
    pos = jnp.arange(seq)
    half = HEAD_DIM // 2
    cr, sr = _rope_cos_sin(pos // GRID_W, half)
    cc, sc = _rope_cos_sin(pos % GRID_W, half)
    return jnp.concatenate([cr, cc], axis=-1), jnp.concatenate([sr, sc], axis=-1)


def mla_rope_tables(seq):
    c, s = _rope_cos_sin(jnp.arange(seq), D_ROPE)
    z = jnp.zeros((seq, LANES - D_ROPE), F32)
    return jnp.concatenate([c, z], axis=-1), jnp.concatenate([s, z], axis=-1)


def router_layout(router_w, router_b):
    perm = np.array([g * EXPERTS_PER_GROUP + j for j in range(EXPERTS_PER_GROUP) for g in range(N_GROUPS)])
    rw = router_w.astype(F32).T[perm]
    hi = rw.astype(BF16)
    lo = (rw - hi.astype(F32)).astype(BF16)
    return hi, lo, router_b.astype(F32)[perm].reshape(N_EXPERTS, 1)


def mla_up_layout(w_q_up, w_kv_up, w_in_tail):
    rq = w_q_up.shape[0]
    wq = w_q_up.reshape(rq, D_HEADS, D_NOPE + D_ROPE)
    wq = jnp.pad(wq, ((0, 0), (0, 0), (0, 2 * LANES - (D_NOPE + D_ROPE)))).reshape(rq, D_HEADS * 2 * LANES)
    wc = jnp.pad(w_in_tail, ((0, 0), (0, LANES - D_ROPE)))
    return wq.astype(BF16), w_kv_up.astype(BF16), wc.astype(BF16)


def kernel(x, router_w, router_b, l0_w_in, l0_a_q_norm, l0_a_k_norm, l0_w_out, l0_ln1_g, l0_ln1_b, l0_w_gate, l0_w_up, l0_w_down, l0_ln2_g, l0_ln2_b, l1_w_in, l1_c_rpb, l1_d_q_norm, l1_d_w_q_up, l1_d_kv_norm, l1_d_w_kv_up, l1_w_out, l1_ln1_g, l1_ln1_b, l1_w_gate, l1_w_up, l1_w_down, l1_ln2_g, l1_ln2_b):
    batch, seq, d = x.shape
    T = batch * seq
    xf = x.reshape(T, d)
    rw_hi, rw_lo, rb = router_layout(router_w, router_b)
    row = lambda v: v.astype(F32).reshape(1, -1)

    a_q_dim = A_HEADS * HEAD_DIM
    a_qk_dim = a_q_dim + A_KV_HEADS * HEAD_DIM
    w0 = l0_w_in.astype(BF16)
    qk_scale = LOG2E * HEAD_DIM ** -0.5
    gain = jnp.concatenate([jnp.tile(l0_a_q_norm.astype(F32), A_HEADS) * qk_scale,
                            jnp.tile(l0_a_k_norm.astype(F32), A_KV_HEADS)]).reshape(1, -1)
    cos_a, sin_a = axial_rope_tables(seq)
    qk_a = proj_norm_rope(xf, w0[:, :a_qk_dim], gain, cos_a, sin_a, seq=seq)
    nb = A_KV_HEADS
    q_scale = jnp.full((B_HEADS * HEAD_DIM,), qk_scale, F32)
    scale0 = jnp.concatenate([jnp.ones((nb * HEAD_DIM,), F32), q_scale, jnp.ones((2 * B_HEADS * HEAD_DIM,), F32)])
    rest0 = proj_heads(xf, w0[:, a_qk_dim:], scale0.reshape(1, -1), heads_per_tile=13)
    group = A_HEADS // A_KV_HEADS
    oa = dense_attention(qk_a, qk_a, rest0, n_heads=A_HEADS, heads=group, k0=A_HEADS, v0=0, batch=batch, seq=seq,
                         group=group)
    ob = dilated_attention(rest0, dilated_bias_table(B_HEADS), n_heads=B_HEADS, heads=1, q0=nb, k0=nb + B_HEADS,
                           v0=nb + 2 * B_HEADS, batch=batch, seq=seq)
    x1 = out_ln(oa, ob, l0_w_out.astype(BF16), xf, row(l0_ln1_g), row(l0_ln1_b))
    eid, gate = route(x1, rw_hi, rw_lo, rb)
    x2, x2b = moe_layer(x1, eid, gate, l0_w_gate, l0_w_up, l0_w_down, l0_ln2_g, l0_ln2_b)

    c_dim = C_HEADS * HEAD_DIM
    w1 = l1_w_in
    scale1 = jnp.concatenate([jnp.full((c_dim,), qk_scale, F32), jnp.ones((2 * c_dim,), F32)])
    qkv_c = proj_heads(x2b, w1[:, :3 * c_dim].astype(BF16), scale1.reshape(1, -1), heads_per_tile=12)
    oc = na_attention(qkv_c, na_bias_table(l1_c_rpb, seq // GRID_W), n_heads=C_HEADS, heads=2, q0=0, k0=C_HEADS,
                      v0=2 * C_HEADS, batch=batch, seq=seq)
    wq, wkv, wc = mla_up_layout(l1_d_w_q_up, l1_d_w_kv_up, w1[:, 3 * c_dim:])
    cos_d, sin_d = mla_rope_tables(seq)
    qd, kd, vd = mla_proj(x2b, wc, row(l1_d_q_norm), row(l1_d_kv_norm), wq, wkv, cos_d, sin_d, seq=seq,
                          scale=LOG2E * (D_NOPE + D_ROPE) ** -0.5)
    od = dense_attention(qd, kd, vd, n_heads=D_HEADS, heads=1, k0=0, v0=0, batch=batch, seq=seq, group=1)
    x3 = out_ln(oc, od, l1_w_out.astype(BF16), x2, row(l1_ln1_g), row(l1_ln1_b))
    eid, gate = route(x3, rw_hi, rw_lo, rb)
    x4, _ = moe_layer(x3, eid, gate, l1_w_gate, l1_w_up, l1_w_down, l1_ln2_g, l1_ln2_b)
    return x4.reshape(batch, seq, d)
```

```python
import functools
import math

import jax
import jax.numpy as jnp
import numpy as np
from jax import lax
from jax.experimental import pallas as pl
from jax.experimental.pallas import tpu as pltpu

D_MODEL = 2048
DEPTH = 2
GRID_W = 64
HEAD_DIM = 128
ROPE_THETA = 10000.0
RMS_EPS = 1e-6
LN_EPS = 1e-5

A_HEADS = 8
A_KV_HEADS = 2
B_HEADS = 8
B_BRANCHES = ((128, 1), (512, 4), (2048, 16))
C_HEADS = 8
NA_ROWS = 8
NA_COLS = 16
D_HEADS = 8
D_Q_RANK = 512
D_KV_RANK = 256
D_NOPE = 128
D_ROPE = 64
D_V = 128

N_EXPERTS = 32
N_GROUPS = 8
EXPERTS_PER_GROUP = N_EXPERTS // N_GROUPS
TOP_K = 2
D_EXPERT = 512
MOE_BLOCK = 256

ALPHA = (2.0 * DEPTH) ** 0.25
LOG2E = math.log2(math.e)

LANES = 128
VMEM_LIMIT = 56 * 1024 * 1024

BF16 = jnp.bfloat16
F32 = jnp.float32


def _params(n_axes, **flags):
    return pltpu.CompilerParams(dimension_semantics=("arbitrary",) * n_axes, vmem_limit_bytes=VMEM_LIMIT,
                                flags=flags or None)


def _rot_half32(y):
    lane = lax.broadcasted_iota(jnp.int32, y.shape, y.ndim - 1)
    up = pltpu.roll(y, 96, axis=y.ndim - 1)
    dn = pltpu.roll(y, 32, axis=y.ndim - 1)
    return jnp.where((lane % 64) < 32, up, dn)


def _proj_heads_kernel(x_ref, w_ref, c_ref, o_ref, *, heads_per_tile):
    acc = jnp.dot(x_ref[...].astype(BF16), w_ref[...], preferred_element_type=F32) * c_ref[...]
    for h in range(heads_per_tile):
        o_ref[h] = acc[:, h * LANES:(h + 1) * LANES].astype(o_ref.dtype)


def proj_heads(x, w, col_scale, *, heads_per_tile, bm=512):
    T, D = x.shape
    H = w.shape[1] // LANES
    bn = heads_per_tile * LANES
    grid = (H // heads_per_tile, T // bm)
    return pl.pallas_call(
        functools.partial(_proj_heads_kernel, heads_per_tile=heads_per_tile),
        grid=grid,
        in_specs=[pl.BlockSpec((bm, D), lambda j, i: (i, 0)),
                  pl.BlockSpec((D, bn), lambda j, i: (0, j)),
                  pl.BlockSpec((1, bn), lambda j, i: (0, j))],
        out_specs=pl.BlockSpec((heads_per_tile, bm, LANES), lambda j, i: (j, i, 0)),
        out_shape=jax.ShapeDtypeStruct((H, T, LANES), BF16),
        compiler_params=_params(2),
        name="proj_heads",
    )(x, w, col_scale)


def _resident(shape):
    return pl.BlockSpec(shape, lambda *_: (0,) * len(shape), pipeline_mode=pl.Buffered(1))


PROJ_SUBTILES = 2


def _proj_norm_rope_kernel(x_ref, w_ref, g_ref, cos_ref, sin_ref, o_ref):
    sub = x_ref.shape[0] // PROJ_SUBTILES
    for t in range(PROJ_SUBTILES):
        rows = slice(t * sub, (t + 1) * sub)
        acc = jnp.dot(x_ref[rows, :].astype(BF16), w_ref[...], preferred_element_type=F32)
        cos = cos_ref[rows, :]
        sin = sin_ref[rows, :]
        for h in range(o_ref.shape[0]):
            a = acc[:, h * LANES:(h + 1) * LANES]
            ms = jnp.mean(a * a, axis=-1, keepdims=True)
            y = a * lax.rsqrt(ms + RMS_EPS) * g_ref[:, h * LANES:(h + 1) * LANES]
            o_ref[h, rows, :] = (y * cos + _rot_half32(y) * sin).astype(o_ref.dtype)


def proj_norm_rope(x, w, gain, cos, sin, *, seq, bm=512):
    T, D = x.shape
    H = w.shape[1] // LANES
    nsb = seq // bm
    return pl.pallas_call(
        _proj_norm_rope_kernel,
        grid=(T // bm,),
        in_specs=[pl.BlockSpec((bm, D), lambda i: (i, 0)),
                  _resident(w.shape), _resident(gain.shape),
                  pl.BlockSpec((bm, LANES), lambda i: (i % nsb, 0)),
                  pl.BlockSpec((bm, LANES), lambda i: (i % nsb, 0))],
        out_specs=pl.BlockSpec((H, bm, LANES), lambda i: (0, i, 0)),
        out_shape=jax.ShapeDtypeStruct((H, T, LANES), BF16),
        compiler_params=_params(1),
        name="proj_norm_rope",
    )(x, w, gain, cos, sin)


def _mla_proj_kernel(x_ref, wc_ref, gq_ref, gkv_ref, wq_ref, wkv_ref, cos_ref, sin_ref,
                     q_ref, k_ref, v_ref, *, scale):
    c = jnp.dot(x_ref[...], wc_ref[...], preferred_element_type=F32)
    cos = cos_ref[...]
    sin = sin_ref[...]

    def rms(t, g):
        ms = jnp.mean(t * t, axis=-1, keepdims=True)
        return t * lax.rsqrt(ms + RMS_EPS) * g

    cq = rms(c[:, :D_Q_RANK], gq_ref[...]).astype(BF16)
    ckv = rms(c[:, D_Q_RANK:D_Q_RANK + D_KV_RANK], gkv_ref[...]).astype(BF16)
    kr = c[:, D_Q_RANK + D_KV_RANK:]
    kr = (kr * cos + _rot_half32(kr) * sin).astype(k_ref.dtype)
    q = jnp.dot(cq, wq_ref[...], preferred_element_type=F32) * scale
    kv = jnp.dot(ckv, wkv_ref[...], preferred_element_type=F32)
    for h in range(D_HEADS):
        base = h * 2 * LANES
        q_ref[h, :, :LANES] = q[:, base:base + LANES].astype(q_ref.dtype)
        qr = q[:, base + LANES:base + 2 * LANES]
        q_ref[h, :, LANES:] = (qr * cos + _rot_half32(qr) * sin).astype(q_ref.dtype)
        k_ref[h, :, :LANES] = kv[:, base:base + LANES].astype(k_ref.dtype)
        k_ref[h, :, LANES:] = kr
        v_ref[h] = kv[:, base + LANES:base + 2 * LANES].astype(v_ref.dtype)


def mla_proj(x, wc, gq, gkv, wq, wkv, cos, sin, *, seq, scale, bm=512):
    T, D = x.shape
    nsb = seq // bm
    return pl.pallas_call(
        functools.partial(_mla_proj_kernel, scale=scale),
        grid=(T // bm,),
        in_specs=[pl.BlockSpec((bm, D), lambda i: (i, 0)),
                  _resident(wc.shape), _resident(gq.shape), _resident(gkv.shape), _resident(wq.shape),
                  _resident(wkv.shape),
                  pl.BlockSpec((bm, LANES), lambda i: (i % nsb, 0)),
                  pl.BlockSpec((bm, LANES), lambda i: (i % nsb, 0))],
        out_specs=[pl.BlockSpec((D_HEADS, bm, 2 * LANES), lambda i: (0, i, 0)),
                   pl.BlockSpec((D_HEADS, bm, 2 * LANES), lambda i: (0, i, 0)),
                   pl.BlockSpec((D_HEADS, bm, LANES), lambda i: (0, i, 0))],
        out_shape=[jax.ShapeDtypeStruct((D_HEADS, T, 2 * LANES), BF16),
                   jax.ShapeDtypeStruct((D_HEADS, T, 2 * LANES), BF16),
                   jax.ShapeDtypeStruct((D_HEADS, T, LANES), BF16)],
        compiler_params=_params(1),
        name="mla_proj",
    )(x, wc, gq, gkv, wq, wkv, cos, sin)


DENSE_TQ = 512
ATTN_TQ = 256


def _attn_pipeline(q_ref, k_ref, v_ref, o_ref, vext_ref, s_buf, m_buf, p_buf, *, heads, group, win, win_start, bias):
    tq = s_buf[0].shape[0]
    S = q_ref.shape[1]
    nblk = S // tq
    total = heads * nblk
    vext_ref[:, :, :LANES] = v_ref[...]
    vext_ref[:, :, LANES:] = jnp.ones(v_ref.shape, vext_ref.dtype)

    def locate(n):
        h, blk = (0, n) if heads == 1 else (n // nblk, n % nblk)
        rows = pl.ds(blk * tq, tq) if isinstance(blk, int) else pl.ds(pl.multiple_of(blk * tq, tq), tq)
        keys = slice(None) if win == S else pl.ds(win_start(blk), win)
        return h, blk, rows, keys

    def scores(n, par):
        h, blk, rows, keys = locate(n)
        s = lax.dot_general(q_ref[h, rows, :], k_ref[h // group, keys, :], (((1,), (1,)), ((), ())),
                            preferred_element_type=F32)
        if bias is not None:
            s = s + bias(h, blk)
        s_buf[par][...] = s
        m_buf[par][...] = jnp.max(s, axis=-1, keepdims=True)

    def exps(par):
        p_buf[par][...] = jnp.exp2(s_buf[par][...] - m_buf[par][...]).astype(BF16)

    def wsum(n, par):
        h, blk, rows, keys = locate(n)
        o_ext = jnp.dot(p_buf[par][...], vext_ref[h // group, keys, :], preferred_element_type=F32)
        o_ref[h, rows, :] = (o_ext[:, :LANES] / o_ext[:, LANES:]).astype(o_ref.dtype)

    scores(0, 0)
    exps(0)
    scores(1, 1)

    def step(j, carry):
        n = 2 * j + 1
        exps(1)
        wsum(n - 1, 0)
        scores(n + 1, 0)
        exps(0)
        wsum(n, 1)
        scores(n + 2, 1)
        return carry

    lax.fori_loop(0, total // 2 - 1, step, 0)
    exps(1)
    wsum(total - 2, 0)
    wsum(total - 1, 1)


def _attn_scratch(kv_heads, seq, win, tq):
    return [pltpu.VMEM((kv_heads, seq, 2 * LANES), BF16),
            pltpu.VMEM((tq, win), F32), pltpu.VMEM((tq, win), F32),
            pltpu.VMEM((tq, 1), F32), pltpu.VMEM((tq, 1), F32),
            pltpu.VMEM((tq, win), BF16), pltpu.VMEM((tq, win), BF16)]


def _dense_attn_kernel(q_ref, k_ref, v_ref, o_ref, vext_ref, s0, s1, m0, m1, p0, p1, *, heads, group):
    _attn_pipeline(q_ref, k_ref, v_ref, o_ref, vext_ref, (s0, s1), (m0, m1), (p0, p1), heads=heads, group=group,
                   win=q_ref.shape[1], win_start=None, bias=None)


def dense_attention(q, k, v, *, n_heads, heads, k0, v0, batch, seq, group):
    T, dq = q.shape[1:]
    kvh = heads // group
    assert n_heads % heads == 0 and heads % group == 0 and k0 % kvh == 0 and v0 % kvh == 0
    return pl.pallas_call(
        functools.partial(_dense_attn_kernel, heads=heads, group=group),
        grid=(batch, n_heads // heads),
        in_specs=[pl.BlockSpec((heads, seq, dq), lambda b, g: (g, b, 0)),
                  pl.BlockSpec((kvh, seq, dq), lambda b, g: (k0 // kvh + g, b, 0)),
                  pl.BlockSpec((kvh, seq, LANES), lambda b, g: (v0 // kvh + g, b, 0))],
        out_specs=pl.BlockSpec((heads, seq, LANES), lambda b, g: (g, b, 0)),
        out_shape=jax.ShapeDtypeStruct((n_heads, T, LANES), BF16),
        scratch_shapes=_attn_scratch(kvh, seq, seq, DENSE_TQ),
        compiler_params=_params(2),
        name="dense_attention",
    )(q, k, v)


DIL_REACH = max(w // 2 for w, _ in B_BRANCHES)
DIL_WIN = 2 * DIL_REACH + ATTN_TQ
DIL_TABLE_TILES = (DIL_WIN + 2 * DIL_REACH) // LANES


def _dilated_attn_kernel(q_ref, k_ref, v_ref, bias_ref, o_ref, vext_ref, s0, s1, m0, m1, p0, p1, *, heads):
    S = q_ref.shape[1]

    def win_start(blk):
        return pl.multiple_of(jnp.clip(blk * ATTN_TQ - DIL_REACH, 0, S - DIL_WIN), ATTN_TQ)

    def bias(h, blk):
        tile0 = (win_start(blk) - blk * ATTN_TQ + 2 * DIL_REACH) // LANES
        return jnp.concatenate([bias_ref[h, tile0 + t] for t in range(DIL_WIN // LANES)], axis=1)

    _attn_pipeline(q_ref, k_ref, v_ref, o_ref, vext_ref, (s0, s1), (m0, m1), (p0, p1), heads=heads, group=1,
                   win=DIL_WIN, win_start=win_start, bias=bias)


def dilated_bias_table(n_heads):
    col = (np.arange(DIL_TABLE_TILES)[:, None, None] * LANES + np.arange(LANES)[None, None, :])
    d = np.abs(col - np.arange(ATTN_TQ)[None, :, None] - 2 * DIL_REACH)
    mult = np.zeros(d.shape, np.float64)
    for window, dil in B_BRANCHES:
        mult += ((d % dil == 0) & (d <= window // 2)).astype(np.float64)
    with np.errstate(divide="ignore"):
        logm = np.log(mult)
    slopes = jnp.asarray(2.0 ** (-8.0 * np.arange(1, n_heads + 1) / n_heads), dtype=F32)
    bias = jnp.asarray(logm, dtype=F32)[None] - slopes[:, None, None, None] * jnp.asarray(d, dtype=F32)[None]
    return bias * LOG2E


def dilated_attention(qkv, bias, *, n_heads, heads, q0, k0, v0, batch, seq):
    T = qkv.shape[1]
    assert n_heads % heads == 0 and q0 % heads == 0 and k0 % heads == 0 and v0 % heads == 0
    qkv_spec = lambda h0: pl.BlockSpec((heads, seq, LANES), lambda b, g: (h0 // heads + g, b, 0))
    return pl.pallas_call(
        functools.partial(_dilated_attn_kernel, heads=heads),
        grid=(batch, n_heads // heads),
        in_specs=[qkv_spec(q0), qkv_spec(k0), qkv_spec(v0),
                  pl.BlockSpec((heads, DIL_TABLE_TILES, ATTN_TQ, LANES), lambda b, g: (g, 0, 0, 0))],
        out_specs=pl.BlockSpec((heads, seq, LANES), lambda b, g: (g, b, 0)),
        out_shape=jax.ShapeDtypeStruct((n_heads, T, LANES), BF16),
        scratch_shapes=_attn_scratch(heads, seq, DIL_WIN, ATTN_TQ),
        compiler_params=_params(2),
        name="dilated_attention",
    )(qkv, qkv, qkv, bias)


NA_QROWS = ATTN_TQ // GRID_W
NA_KROWS = NA_QROWS + NA_ROWS
NA_WIN = NA_KROWS * GRID_W


def _na_window_row(blk, rows):
    lo = blk * NA_QROWS - NA_ROWS // 2
    return jnp.clip(lo, 0, rows - NA_KROWS) if not isinstance(blk, int) else min(max(lo, 0), rows - NA_KROWS)


def _na_attn_kernel(q_ref, k_ref, v_ref, bias_ref, o_ref, vext_ref, s0, s1, m0, m1, p0, p1, *, heads, rows):
    nblk = rows // NA_QROWS

    def win_start(blk):
        return pl.multiple_of(_na_window_row(blk, rows) * GRID_W, GRID_W)

    def bias(h, blk):
        cls = (blk > 0).astype(jnp.int32) + (blk == nblk - 1).astype(jnp.int32) if not isinstance(blk, int) \
            else int(blk > 0) + int(blk == nblk - 1)
        return bias_ref[h, cls]

    _attn_pipeline(q_ref, k_ref, v_ref, o_ref, vext_ref, (s0, s1), (m0, m1), (p0, p1), heads=heads, group=1,
                   win=NA_WIN, win_start=win_start, bias=bias)


def na_bias_table(rpb, rows):
    H = rpb.shape[0]
    col = np.arange(GRID_W)
    c0 = np.clip(col - NA_COLS // 2, 0, GRID_W - NA_COLS)
    col_ok = (col[None, :] >= c0[:, None]) & (col[None, :] < c0[:, None] + NA_COLS)
    dcol = np.clip(col[None, :] - col[:, None] + NA_COLS - 1, 0, 2 * NA_COLS - 2)
    pick = ((dcol[:, :, None] == np.arange(2 * NA_COLS - 1)) & col_ok[:, :, None]).astype(np.float32)
    t = jnp.einsum("hrd,qkd->hrqk", rpb.astype(F32), jnp.asarray(pick), precision=lax.Precision.HIGHEST)
    t = jnp.where(jnp.asarray(col_ok)[None, None], t * LOG2E, -jnp.inf)
    nblk = rows // NA_QROWS

    def geometry(blk):
        r = blk * NA_QROWS + np.arange(NA_QROWS)
        kr = _na_window_row(blk, rows) + np.arange(NA_KROWS)
        r0 = np.clip(r - NA_ROWS // 2, 0, rows - NA_ROWS)
        valid = (kr[None, :] >= r0[:, None]) & (kr[None, :] < r0[:, None] + NA_ROWS)
        drow = np.clip(kr[None, :] - r[:, None] + NA_ROWS - 1, 0, 2 * NA_ROWS - 2)
        return valid, np.where(valid, drow, 0)

    inner = geometry(1)
    assert all(np.array_equal(a, b) for blk in range(1, nblk - 1) for a, b in zip(geometry(blk), inner))
    tables = []
    for blk in (0, 1, nblk - 1):
        valid, drow = geometry(blk)
        slab = jnp.where(jnp.asarray(valid)[None, :, :, None, None], t[:, drow], -jnp.inf)
        tables.append(slab.transpose(0, 1, 3, 2, 4).reshape(H, ATTN_TQ, NA_WIN))
    return jnp.stack(tables, axis=1)


def na_attention(qkv, bias, *, n_heads, heads, q0, k0, v0, batch, seq):
    T = qkv.shape[1]
    rows = seq // GRID_W
    assert n_heads % heads == 0 and q0 % heads == 0 and k0 % heads == 0 and v0 % heads == 0
    assert rows % NA_QROWS == 0 and rows // NA_QROWS >= 3
    qkv_spec = lambda h0: pl.BlockSpec((heads, seq, LANES), lambda b, g: (h0 // heads + g, b, 0))
    return pl.pallas_call(
        functools.partial(_na_attn_kernel, heads=heads, rows=rows),
        grid=(batch, n_heads // heads),
        in_specs=[qkv_spec(q0), qkv_spec(k0), qkv_spec(v0),
                  pl.BlockSpec((heads, 3, ATTN_TQ, NA_WIN), lambda b, g: (g, 0, 0, 0))],
        out_specs=pl.BlockSpec((heads, seq, LANES), lambda b, g: (g, b, 0)),
        out_shape=jax.ShapeDtypeStruct((n_heads, T, LANES), BF16),
        scratch_shapes=_attn_scratch(heads, seq, NA_WIN, ATTN_TQ),
        compiler_params=_params(2),
        name="na_attention",
    )(qkv, qkv, qkv, bias)


def _layer_norm(z, g, b):
    mu = jnp.mean(z, axis=-1, keepdims=True)
    zc = z - mu
    var = jnp.mean(zc * zc, axis=-1, keepdims=True)
    return zc * lax.rsqrt(var + LN_EPS) * g + b


def _top2_of4(v0, v1, v2, v3):
    a = jnp.maximum(v0, v1)
    b = jnp.minimum(v0, v1)
    c = jnp.maximum(v2, v3)
    d = jnp.minimum(v2, v3)
    return jnp.maximum(a, c), jnp.maximum(jnp.minimum(a, c), jnp.maximum(b, d))


def _route(logits_t, rb):
    G = N_GROUPS
    scores = jax.nn.sigmoid(logits_t)
    sel = scores + rb
    sj = [sel[j * G:(j + 1) * G] for j in range(EXPERTS_PER_GROUP)]
    cj = [scores[j * G:(j + 1) * G] for j in range(EXPERTS_PER_GROUP)]
    t1, t2 = _top2_of4(*sj)
    grp = t1 + t2
    gi = lax.broadcasted_iota(jnp.int32, grp.shape, 0).astype(F32)
    gbest = jnp.min(jnp.where(grp == jnp.max(grp, axis=0, keepdims=True), gi, float(G)), axis=0, keepdims=True)
    pick = gi == gbest
    v = [jnp.sum(jnp.where(pick, s, 0.0), axis=0, keepdims=True) for s in sj]
    c = [jnp.sum(jnp.where(pick, s, 0.0), axis=0, keepdims=True) for s in cj]
    neg = jnp.float32(-jnp.inf)
    m1 = jnp.maximum(jnp.maximum(v[0], v[1]), jnp.maximum(v[2], v[3]))
    j1 = jnp.where(v[0] == m1, 0, jnp.where(v[1] == m1, 1, jnp.where(v[2] == m1, 2, 3)))
    w = [jnp.where(j1 == j, neg, v[j]) for j in range(4)]
    m2 = jnp.maximum(jnp.maximum(w[0], w[1]), jnp.maximum(w[2], w[3]))
    j2 = jnp.where(w[0] == m2, 0, jnp.where(w[1] == m2, 1, jnp.where(w[2] == m2, 2, 3)))
    g1 = jnp.where(j1 == 0, c[0], jnp.where(j1 == 1, c[1], jnp.where(j1 == 2, c[2], c[3])))
    g2 = jnp.where(j2 == 0, c[0], jnp.where(j2 == 1, c[1], jnp.where(j2 == 2, c[2], c[3])))
    tot = g1 + g2
    e0 = gbest.astype(jnp.int32) * EXPERTS_PER_GROUP
    return (e0 + j1, e0 + j2), (g1 / tot, g2 / tot)


OUT_SUBTILES = 2


def _out_ln_kernel(a_ref, b_ref, w_ref, x_ref, g_ref, beta_ref, x1_ref):
    sub = x_ref.shape[0] // OUT_SUBTILES
    for t in range(OUT_SUBTILES):
        rows = slice(t * sub, (t + 1) * sub)
        heads = [a_ref[h, rows, :] for h in range(a_ref.shape[0])] + [b_ref[h, rows, :] for h in range(b_ref.shape[0])]
        mix = jnp.dot(jnp.concatenate(heads, axis=1), w_ref[...], preferred_element_type=F32)
        x1_ref[rows, :] = _layer_norm(ALPHA * x_ref[rows, :] + mix, g_ref[...], beta_ref[...])


def out_ln(a, b, w, x, g, beta, *, bm=512):
    T, D = x.shape
    return pl.pallas_call(
        _out_ln_kernel,
        grid=(T // bm,),
        in_specs=[pl.BlockSpec((a.shape[0], bm, LANES), lambda i: (0, i, 0)),
                  pl.BlockSpec((b.shape[0], bm, LANES), lambda i: (0, i, 0)),
                  _resident(w.shape),
                  pl.BlockSpec((bm, D), lambda i: (i, 0)),
                  _resident(g.shape), _resident(beta.shape)],
        out_specs=pl.BlockSpec((bm, D), lambda i: (i, 0)),
        out_shape=jax.ShapeDtypeStruct((T, D), F32),
        compiler_params=_params(1),
        name="out_ln",
    )(a, b, w, x, g, beta)


def _route_kernel(x_ref, rwh_ref, rwl_ref, rb_ref, eid_ref, gate_ref):
    x = x_ref[...]
    hi = x.astype(BF16)
    lo = (x - hi.astype(F32)).astype(BF16)
    nt = (((1,), (1,)), ((), ()))
    logits_t = (lax.dot_general(rwh_ref[...], hi, nt, preferred_element_type=F32)
                + lax.dot_general(rwl_ref[...], hi, nt, preferred_element_type=F32)
                + lax.dot_general(rwh_ref[...], lo, nt, preferred_element_type=F32))
    eid, gate = _route(logits_t, rb_ref[...])
    for k in range(TOP_K):
        eid_ref[k:k + 1, :] = eid[k]
        gate_ref[k:k + 1, :] = gate[k]


def route(x1, rw_hi, rw_lo, rb, *, bm=1024):
    T, D = x1.shape
    return pl.pallas_call(
        _route_kernel,
        grid=(T // bm,),
        in_specs=[pl.BlockSpec((bm, D), lambda i: (i, 0)),
                  _resident(rw_hi.shape), _resident(rw_lo.shape), _resident(rb.shape)],
        out_specs=[pl.BlockSpec((TOP_K, bm), lambda i: (0, i)),
                   pl.BlockSpec((TOP_K, bm), lambda i: (0, i))],
        out_shape=[jax.ShapeDtypeStruct((TOP_K, T), jnp.int32),
                   jax.ShapeDtypeStruct((TOP_K, T), F32)],
        compiler_params=_params(1),
        name="route",
    )(x1, rw_hi, rw_lo, rb)


GATHER_DEPTH = 3


def _row_copy(src_hbm, row, dst_vmem, j, sem):
    return pltpu.make_async_copy(src_hbm.at[pl.ds(row, 1), :], dst_vmem.at[pl.ds(j, 1), :], sem)


def _gather_ring_step(start_gather, wait_gather, compute, active=None, idle=None):
    i = pl.program_id(0)
    n_steps = pl.num_programs(0)
    ahead = GATHER_DEPTH - 1
    slot = i % GATHER_DEPTH

    @pl.when(i == 0)
    def _():
        for a in range(ahead):
            start_gather(a, a)

    wait_gather(slot)
    more = i + ahead < n_steps
    cases = [(more, compute)] if active is None else [(more & active, compute), (more & ~active, idle)]
    for cond, work in cases:
        @pl.when(cond)
        def _(work=work):
            work(slot)
            start_gather(i + ahead, (i + ahead) % GATHER_DEPTH)
    cases = [(~more, compute)] if active is None else [(~more & active, compute), (~more & ~active, idle)]
    for cond, work in cases:
        @pl.when(cond)
        def _(work=work):
            work(slot)


def _moe_ffn_kernel(blk_lo_ref, blk_hi_ref, nblk_ref, tok_ref, x_hbm,
                    wg_lo, wu_lo, wd_lo, wg_hi, wu_hi, wd_hi, y_ref, xbuf, sem):
    def start_gather(blk, s):
        base = blk * MOE_BLOCK
        for j in range(MOE_BLOCK):
            _row_copy(x_hbm, tok_ref[base + j], xbuf.at[s], j, sem.at[s]).start()

    def wait_gather(s):
        pltpu.make_async_copy(x_hbm.at[pl.ds(0, MOE_BLOCK), :], xbuf.at[s], sem.at[s]).wait()

    def ffn(s):
        D = xbuf.shape[-1]
        xb = xbuf[s].astype(BF16)
        for half, (wg, wu, wd) in enumerate(((wg_lo, wu_lo, wd_lo), (wg_hi, wu_hi, wd_hi))):
            hg = jnp.dot(xb, wg[...], preferred_element_type=F32)
            hu = jnp.dot(xb, wu[...], preferred_element_type=F32)
            hb = (jax.nn.silu(hg) * hu).astype(BF16)
            y_ref[:, half * D:(half + 1) * D] = jnp.dot(hb, wd[...], preferred_element_type=F32)

    def no_tokens(s):
        y_ref[...] = jnp.zeros(y_ref.shape, y_ref.dtype)

    _gather_ring_step(start_gather, wait_gather, ffn, active=pl.program_id(0) < nblk_ref[0], idle=no_tokens)


def moe_ffn(x1, w_gate, w_up, w_down, blk_lo, blk_hi, nblk, row_tok):
    D = x1.shape[1]
    n_blocks = blk_lo.shape[0]
    assert n_blocks >= GATHER_DEPTH
    n_rows = n_blocks * MOE_BLOCK
    w_in = lambda which: pl.BlockSpec((None, D, D_EXPERT), lambda i, lo, hi, nb, tk: ((lo, hi)[which][i], 0, 0))
    w_out = lambda which: pl.BlockSpec((None, D_EXPERT, D), lambda i, lo, hi, nb, tk: ((lo, hi)[which][i], 0, 0))
    grid_spec = pltpu.PrefetchScalarGridSpec(
        num_scalar_prefetch=4,
        grid=(n_blocks,),
        in_specs=[pl.BlockSpec(memory_space=pl.ANY), w_in(0), w_in(0), w_out(0), w_in(1), w_in(1), w_out(1)],
        out_specs=pl.BlockSpec((MOE_BLOCK, 2 * D), lambda i, lo, hi, nb, tk: (i, 0)),
        scratch_shapes=[pltpu.VMEM((GATHER_DEPTH, MOE_BLOCK, D), F32), pltpu.SemaphoreType.DMA((GATHER_DEPTH,))],
    )
    return pl.pallas_call(
        _moe_ffn_kernel,
        grid_spec=grid_spec,
        out_shape=jax.ShapeDtypeStruct((n_rows, 2 * D), F32),
        compiler_params=_params(1),
        name="moe_ffn",
    )(blk_lo, blk_hi, nblk, row_tok, x1, w_gate, w_up, w_down, w_gate, w_up, w_down)


def _combine_ln_kernel(dest_ref, y_hbm, x1_ref, gate_ref, g_ref, beta_ref, x2_ref, x2b_ref, ybuf, sem, *, bm):
    def start_gather(tile, s):
        base = tile * bm
        for j in range(bm):
            _row_copy(y_hbm, dest_ref[base + j], ybuf.at[s], j, sem.at[s]).start()

    def wait_gather(s):
        pltpu.make_async_copy(y_hbm.at[pl.ds(0, bm), :], ybuf.at[s], sem.at[s]).wait()

    def finish(s):
        D = x1_ref.shape[-1]
        gate = gate_ref[...]
        moe = ybuf[s, :, :D] * gate[:, 0:1] + ybuf[s, :, D:] * gate[:, 1:2]
        x2 = _layer_norm(ALPHA * x1_ref[...] + moe, g_ref[...], beta_ref[...])
        x2_ref[...] = x2
        x2b_ref[...] = x2.astype(BF16)

    _gather_ring_step(start_gather, wait_gather, finish)


def combine_ln(y_rows, dest, x1, gate_t, g, beta, *, bm=256):
    T, D = x1.shape
    assert T // bm >= GATHER_DEPTH
    grid_spec = pltpu.PrefetchScalarGridSpec(
        num_scalar_prefetch=1,
        grid=(T // bm,),
        in_specs=[pl.BlockSpec(memory_space=pl.ANY),
                  pl.BlockSpec((bm, D), lambda i, d: (i, 0)),
                  pl.BlockSpec((bm, TOP_K), lambda i, d: (i, 0)),
                  pl.BlockSpec((1, D), lambda i, d: (0, 0)),
                  pl.BlockSpec((1, D), lambda i, d: (0, 0))],
        out_specs=[pl.BlockSpec((bm, D), lambda i, d: (i, 0)),
                   pl.BlockSpec((bm, D), lambda i, d: (i, 0))],
        scratch_shapes=[pltpu.VMEM((GATHER_DEPTH, bm, TOP_K * D), F32), pltpu.SemaphoreType.DMA((GATHER_DEPTH,))],
    )
    return pl.pallas_call(
        functools.partial(_combine_ln_kernel, bm=bm),
        grid_spec=grid_spec,
        out_shape=[jax.ShapeDtypeStruct((T, D), F32), jax.ShapeDtypeStruct((T, D), BF16)],
        compiler_params=_params(1),
        name="combine_ln",
    )(dest, y_rows, x1, gate_t, g, beta)


_PAIR_LO = np.array([a for a in range(EXPERTS_PER_GROUP) for b in range(a + 1, EXPERTS_PER_GROUP)], np.int32)
_PAIR_HI = np.array([b for a in range(EXPERTS_PER_GROUP) for b in range(a + 1, EXPERTS_PER_GROUP)], np.int32)
N_PAIRS = len(_PAIR_LO)
N_CLASSES = N_GROUPS * N_PAIRS


def moe_plan(eid, gate):
    T = eid.shape[1]
    n_blocks = (T + N_CLASSES * (MOE_BLOCK - 1) + MOE_BLOCK - 1) // MOE_BLOCK
    first_lower = eid[0] < eid[1]
    e_lo = jnp.minimum(eid[0], eid[1])
    e_hi = jnp.maximum(eid[0], eid[1])
    gate_t = jnp.stack([jnp.where(first_lower, gate[0], gate[1]), jnp.where(first_lower, gate[1], gate[0])], axis=1)
    a = e_lo % EXPERTS_PER_GROUP
    b = e_hi % EXPERTS_PER_GROUP
    pair = a * (2 * EXPERTS_PER_GROUP - 1 - a) // 2 + (b - a - 1)
    cls = (e_lo // EXPERTS_PER_GROUP) * N_PAIRS + pair
    onehot = (cls[:, None] == jnp.arange(N_CLASSES, dtype=jnp.int32)[None, :]).astype(jnp.int32)
    csum = jnp.cumsum(onehot, axis=0)
    rank = jnp.sum(csum * onehot, axis=1) - 1
    counts = csum[-1]
    pcounts = (counts + MOE_BLOCK - 1) // MOE_BLOCK * MOE_BLOCK
    pends = jnp.cumsum(pcounts)
    pstarts = pends - pcounts
    dest = (pstarts[cls] + rank).astype(jnp.int32)
    pad_tok = jnp.arange(n_blocks * MOE_BLOCK, dtype=jnp.int32) % T
    row_tok = pad_tok.at[dest].set(jnp.arange(T, dtype=jnp.int32))
    blk_start = jnp.arange(n_blocks, dtype=jnp.int32) * MOE_BLOCK
    blk_cls = jnp.minimum(jnp.sum((pends[None, :] <= blk_start[:, None]).astype(jnp.int32), axis=1), N_CLASSES - 1)
    blk_base = (blk_cls // N_PAIRS) * EXPERTS_PER_GROUP
    blk_lo = (blk_base + jnp.asarray(_PAIR_LO)[blk_cls % N_PAIRS]).astype(jnp.int32)
    blk_hi = (blk_base + jnp.asarray(_PAIR_HI)[blk_cls % N_PAIRS]).astype(jnp.int32)
    nblk = (pends[-1:] // MOE_BLOCK).astype(jnp.int32)
    return dest, row_tok, blk_lo, blk_hi, nblk, gate_t


def moe_layer(x1, eid, gate, w_gate, w_up, w_down, ln_g, ln_b):
    dest, row_tok, blk_lo, blk_hi, nblk, gate_t = moe_plan(eid, gate)
    y_rows = moe_ffn(x1, w_gate.astype(BF16), w_up.astype(BF16), w_down.astype(BF16), blk_lo, blk_hi, nblk, row_tok)
    return combine_ln(y_rows, dest, x1, gate_t, ln_g.reshape(1, -1), ln_b.reshape(1, -1))


def _rope_cos_sin(pos, dim):
    inv_freq = ROPE_THETA ** (-jnp.arange(0, dim, 2, dtype=F32) / dim)
    ang = pos.astype(F32)[:, None] * inv_freq[None, :]
    cos = jnp.cos(ang)
    sin = jnp.sin(ang)
    return jnp.concatenate([cos, cos], axis=-1), jnp.concatenate([-sin, sin], axis=-1)


def axial_rope_tables(seq):
    pos = jnp.arange(seq)
    half = HEAD_DIM // 2
    cr, sr = _rope_cos_sin(pos // GRID_W, half)
    cc, sc = _rope_cos_sin(pos % GRID_W, half)
    return jnp.concatenate([cr, cc], axis=-1), jnp.concatenate([sr, sc], axis=-1)


def mla_rope_tables(seq):
    c, s = _rope_cos_sin(jnp.arange(seq), D_ROPE)
    z = jnp.zeros((seq, LANES - D_ROPE), F32)
    return jnp.concatenate([c, z], axis=-1), jnp.concatenate([s, z], axis=-1)


def router_layout(router_w, router_b):
    perm = np.array([g * EXPERTS_PER_GROUP + j for j in range(EXPERTS_PER_GROUP) for g in range(N_GROUPS)])
    rw = router_w.astype(F32).T[perm]
    hi = rw.astype(BF16)
    lo = (rw - hi.astype(F32)).astype(BF16)
    return hi, lo, router_b.astype(F32)[perm].reshape(N_EXPERTS, 1)


def mla_up_layout(w_q_up, w_kv_up, w_in_tail):
    rq = w_q_up.shape[0]
    wq = w_q_up.reshape(rq, D_HEADS, D_NOPE + D_ROPE)
    wq = jnp.pad(wq, ((0, 0), (0, 0), (0, 2 * LANES - (D_NOPE + D_ROPE)))).reshape(rq, D_HEADS * 2 * LANES)
    wc = jnp.pad(w_in_tail, ((0, 0), (0, LANES - D_ROPE)))
    return wq.astype(BF16), w_kv_up.astype(BF16), wc.astype(BF16)


def kernel(x, router_w, router_b, l0_w_in, l0_a_q_norm, l0_a_k_norm, l0_w_out, l0_ln1_g, l0_ln1_b, l0_w_gate, l0_w_up, l0_w_down, l0_ln2_g, l0_ln2_b, l1_w_in, l1_c_rpb, l1_d_q_norm, l1_d_w_q_up, l1_d_kv_norm, l1_d_w_kv_up, l1_w_out, l1_ln1_g, l1_ln1_b, l1_w_gate, l1_w_up, l1_w_down, l1_ln2_g, l1_ln2_b):
    batch, seq, d = x.shape
    T = batch * seq
    xf = x.reshape(T, d)
    rw_hi, rw_lo, rb = router_layout(router_w, router_b)
    row = lambda v: v.astype(F32).reshape(1, -1)

    a_q_dim = A_HEADS * HEAD_DIM
    a_qk_dim = a_q_dim + A_KV_HEADS * HEAD_DIM
    w0 = l0_w_in.astype(BF16)
    qk_scale = LOG2E * HEAD_DIM ** -0.5
    gain = jnp.concatenate([jnp.tile(l0_a_q_norm.astype(F32), A_HEADS) * qk_scale,
                            jnp.tile(l0_a_k_norm.astype(F32), A_KV_HEADS)]).reshape(1, -1)
    cos_a, sin_a = axial_rope_tables(seq)
    qk_a = proj_norm_rope(xf, w0[:, :a_qk_dim], gain, cos_a, sin_a, seq=seq)
    nb = A_KV_HEADS
    q_scale = jnp.full((B_HEADS * HEAD_DIM,), qk_scale, F32)
    scale0 = jnp.concatenate([jnp.ones((nb * HEAD_DIM,), F32), q_scale, jnp.ones((2 * B_HEADS * HEAD_DIM,), F32)])
    rest0 = proj_heads(xf, w0[:, a_qk_dim:], scale0.reshape(1, -1), heads_per_tile=13)
    group = A_HEADS // A_KV_HEADS
    oa = dense_attention(qk_a, qk_a, rest0, n_heads=A_HEADS, heads=group, k0=A_HEADS, v0=0, batch=batch, seq=seq,
                         group=group)
    ob = dilated_attention(rest0, dilated_bias_table(B_HEADS), n_heads=B_HEADS, heads=1, q0=nb, k0=nb + B_HEADS,
                           v0=nb + 2 * B_HEADS, batch=batch, seq=seq)
    x1 = out_ln(oa, ob, l0_w_out.astype(BF16), xf, row(l0_ln1_g), row(l0_ln1_b))
    eid, gate = route(x1, rw_hi, rw_lo, rb)
    x2, x2b = moe_layer(x1, eid, gate, l0_w_gate, l0_w_up, l0_w_down, l0_ln2_g, l0_ln2_b)

    c_dim = C_HEADS * HEAD_DIM
    w1 = l1_w_in
    scale1 = jnp.concatenate([jnp.full((c_dim,), qk_scale, F32), jnp.ones((2 * c_dim,), F32)])
    qkv_c = proj_heads(x2b, w1[:, :3 * c_dim].astype(BF16), scale1.reshape(1, -1), heads_per_tile=12)
    oc = na_attention(qkv_c, na_bias_table(l1_c_rpb, seq // GRID_W), n_heads=C_HEADS, heads=2, q0=0, k0=C_HEADS,
                      v0=2 * C_HEADS, batch=batch, seq=seq)
    wq, wkv, wc = mla_up_layout(l1_d_w_q_up, l1_d_w_kv_up, w1[:, 3 * c_dim:])
    cos_d, sin_d = mla_rope_tables(seq)
    qd, kd, vd = mla_proj(x2b, wc, row(l1_d_q_norm), row(l1_d_kv_norm), wq, wkv, cos_d, sin_d, seq=seq,
                          scale=LOG2E * (D_NOPE + D_ROPE) ** -0.5)
    od = dense_attention(qd, kd, vd, n_heads=D_HEADS, heads=1, k0=0, v0=0, batch=batch, seq=seq, group=1)
    x3 = out_ln(oc, od, l1_w_out.astype(BF16), x2, row(l1_ln1_g), row(l1_ln1_b))
    eid, gate = route(x3, rw_hi, rw_lo, rb)
    x4, _ = moe_layer(x3, eid, gate, l1_w_gate, l1_w_up, l1_w_down, l1_ln2_g, l1_ln2_b)
    return x4.reshape(batch, seq, d)
```
